```python
import math
import jax, jax.numpy as jnp
from jax import lax
import numpy as np

D_MODEL = 1024
BATCH = 16
SEQ = 4096
DEPTH = 1

SSM_WIDTH = D_MODEL
SSM_GROUP = 16
SSM_GROUPS = SSM_WIDTH // SSM_GROUP
SSM_STATE = 64
SSM_CHUNK = 128
DT_MIN = 1e-3
DT_MAX = 1e-1
HEAD_DIM = 64
HEADS_PER_GROUP = D_MODEL // 128
DILATED_GROUPS = ((128, 1), (512, 4), (2048, 16))
N_DIL = len(DILATED_GROUPS)
ATTN_WIDTH = HEADS_PER_GROUP * HEAD_DIM
QKV_WIDTH = N_DIL * ATTN_WIDTH
ROPE_DIM = HEAD_DIM // 4
ROPE_THETA = 500000.0
Q_BLOCK = 128
NEG_INF = -1e30
EPS = 1e-6
SPLITS = (SSM_WIDTH, SSM_WIDTH, QKV_WIDTH, QKV_WIDTH, QKV_WIDTH, ATTN_WIDTH, D_MODEL, D_MODEL)
IN_WIDTH = sum(SPLITS)

kernel_name = "hybrid_s5_dilated_attn_gated_block"


def rms_norm(x, w):
    xf = x.astype(jnp.float32)
    y = xf * lax.rsqrt(jnp.mean(xf * xf, axis=-1, keepdims=True) + EPS)
    return y * w.astype(jnp.float32)


def apply_partial_rope(x, positions):
    half = ROPE_DIM // 2
    inv_freq = ROPE_THETA ** (-jnp.arange(0, ROPE_DIM, 2, dtype=jnp.float32) / ROPE_DIM)
    ang = positions.astype(jnp.float32)[..., None] * inv_freq
    cos = jnp.cos(ang)[:, :, None, None, :]
    sin = jnp.sin(ang)[:, :, None, None, :]
    x1 = x[..., :half]
    x2 = x[..., half:ROPE_DIM]
    return jnp.concatenate([x1 * cos - x2 * sin, x2 * cos + x1 * sin, x[..., ROPE_DIM:]], axis=-1)


def _complex_diag_combine(e1, e2):
    a1r, a1i, b1r, b1i = e1
    a2r, a2i, b2r, b2i = e2
    ar = a2r * a1r - a2i * a1i
    ai = a2r * a1i + a2i * a1r
    br = a2r * b1r - a2i * b1i + b2r
    bi = a2r * b1i + a2i * b1r + b2i
    return (ar, ai, br, bi)


def s5_branch(u, lam_re, lam_im, log_dt, b_re, b_im, c_re, c_im, d_skip, w_glu):
    f32 = jnp.float32
    uf = u.astype(f32)
    lr, li = lam_re.astype(f32), lam_im.astype(f32)
    dt = jnp.exp(log_dt.astype(f32))[:, None]
    mag = jnp.exp(lr * dt)
    ab_re = mag * jnp.cos(li * dt)
    ab_im = mag * jnp.sin(li * dt)
    den = lr * lr + li * li
    nr = ab_re - 1.0
    f_re = (nr * lr + ab_im * li) / den
    f_im = (ab_im * lr - nr * li) / den
    br, bi = b_re.astype(f32), b_im.astype(f32)
    bb_re = f_re[..., None] * br - f_im[..., None] * bi
    bb_im = f_re[..., None] * bi + f_im[..., None] * br
    cr, ci = c_re.astype(f32), c_im.astype(f32)

    bsz, length, _ = u.shape
    n_chunks = length // SSM_CHUNK
    uc = uf.reshape(bsz, n_chunks, SSM_CHUNK, SSM_GROUPS, SSM_GROUP).transpose(1, 0, 2, 3, 4)

    def chunk_step(carry, u_c):
        h0_re, h0_im = carry
        bu_re = jnp.einsum('blgp,gnp->blgn', u_c, bb_re)
        bu_im = jnp.einsum('blgp,gnp->blgn', u_c, bb_im)
        a_re = jnp.broadcast_to(ab_re, bu_re.shape)
        a_im = jnp.broadcast_to(ab_im, bu_im.shape)
        pa_re, pa_im, hs_re, hs_im = lax.associative_scan(
            _complex_diag_combine, (a_re, a_im, bu_re, bu_im), axis=1)
        h_re = hs_re + pa_re * h0_re[:, None] - pa_im * h0_im[:, None]
        h_im = hs_im + pa_re * h0_im[:, None] + pa_im * h0_re[:, None]
        y = jnp.einsum('blgn,gpn->blgp', h_re, cr) - jnp.einsum('blgn,gpn->blgp', h_im, ci)
        return (h_re[:, -1], h_im[:, -1]), y

    zeros = jnp.zeros((bsz, SSM_GROUPS, SSM_STATE), f32)
    _, y = lax.scan(chunk_step, (zeros, zeros), uc)
    y = y.transpose(1, 0, 2, 3, 4).reshape(bsz, length, SSM_WIDTH)
    y = y + d_skip.astype(f32) * uf
    g = jax.nn.gelu(y)
    return g * jax.nn.sigmoid(g @ w_glu.astype(f32))


def dilated_attention(q, k, v):
    bsz, length = q.shape[0], q.shape[1]
    n_blocks = length // Q_BLOCK
    scale = 1.0 / math.sqrt(HEAD_DIM)
    k_groups = [k[:, :, g] for g in range(N_DIL)]
    v_groups = [v[:, :, g] for g in range(N_DIL)]

    def one_block(i):
        t = i * Q_BLOCK + jnp.arange(Q_BLOCK)
        q_blk = lax.dynamic_slice_in_dim(q, i * Q_BLOCK, Q_BLOCK, axis=1)
        outs, lses = [], []
        for g, (window, dil) in enumerate(DILATED_GROUPS):
            j = jnp.arange(window // dil + 1)
            idx = t[:, None] - dil * j[None, :]
            valid = idx >= 0
            idxc = jnp.maximum(idx, 0)
            kg = k_groups[g][:, idxc]
            vg = v_groups[g][:, idxc]
            s = jnp.einsum('bqhd,bqjhd->bqhj', q_blk[:, :, g], kg) * scale
            s = jnp.where(valid[None, :, None, :], s, NEG_INF)
            lse = jax.nn.logsumexp(s, axis=-1)
            p = jnp.exp(s - lse[..., None])
            outs.append(jnp.einsum('bqhj,bqjhd->bqhd', p, vg))
            lses.append(lse)
        w = jax.nn.softmax(jnp.stack(lses), axis=0)
        return jnp.einsum('gbqh,gbqhd->bqhd', w, jnp.stack(outs))

    o = lax.map(one_block, jnp.arange(n_blocks))
    return o.transpose(1, 0, 2, 3, 4).reshape(bsz, length, HEADS_PER_GROUP * HEAD_DIM)


def setup_inputs(seed: int = 0) -> dict:
    key = jax.random.key(seed)
    ks = jax.random.split(key, 20)
    f32 = jnp.float32
    x = jax.random.normal(ks[0], (BATCH, SEQ, D_MODEL), f32)
    start = jax.random.randint(ks[1], (BATCH, 1), 0, 1024, dtype=jnp.int32)
    positions = (start + jnp.arange(SEQ, dtype=jnp.int32)[None, :]).astype(jnp.int32)
    norm_w = 1.0 + 0.02 * jax.random.normal(ks[2], (DEPTH, D_MODEL), f32)
    w_in = jax.random.normal(ks[3], (DEPTH, D_MODEL, IN_WIDTH), f32) * D_MODEL ** -0.5
    n_idx = jnp.arange(SSM_STATE, dtype=f32)
    lam_re = -0.5 + 0.01 * jax.random.normal(ks[4], (DEPTH, SSM_GROUPS, SSM_STATE), f32)
    lam_im = math.pi * n_idx + 0.01 * jax.random.normal(ks[5], (DEPTH, SSM_GROUPS, SSM_STATE), f32)
    log_dt = jax.random.uniform(ks[6], (DEPTH, SSM_GROUPS), f32, math.log(DT_MIN), math.log(DT_MAX))
    b_scale = (2.0 * SSM_GROUP) ** -0.5
    b_re = jax.random.normal(ks[7], (DEPTH, SSM_GROUPS, SSM_STATE, SSM_GROUP), f32) * b_scale
    b_im = jax.random.normal(ks[8], (DEPTH, SSM_GROUPS, SSM_STATE, SSM_GROUP), f32) * b_scale
    c_scale = (2.0 * SSM_STATE) ** -0.5
    c_re = jax.random.normal(ks[9], (DEPTH, SSM_GROUPS, SSM_GROUP, SSM_STATE), f32) * c_scale
    c_im = jax.random.normal(ks[10], (DEPTH, SSM_GROUPS, SSM_GROUP, SSM_STATE), f32) * c_scale
    d_skip = jax.random.normal(ks[11], (DEPTH, SSM_WIDTH), f32)
    w_glu = jax.random.normal(ks[12], (DEPTH, SSM_WIDTH, SSM_WIDTH), f32) * SSM_WIDTH ** -0.5
    q_norm_w = 1.0 + 0.02 * jax.random.normal(ks[13], (DEPTH, N_DIL, HEAD_DIM), f32)
    k_norm_w = 1.0 + 0.02 * jax.random.normal(ks[14], (DEPTH, N_DIL, HEAD_DIM), f32)
    w_ssm_out = jax.random.normal(ks[15], (DEPTH, SSM_WIDTH, D_MODEL), f32) * SSM_WIDTH ** -0.5
    w_attn_out = jax.random.normal(ks[16], (DEPTH, ATTN_WIDTH, D_MODEL), f32) * ATTN_WIDTH ** -0.5
    w_o = jax.random.normal(ks[17], (DEPTH, D_MODEL, D_MODEL), f32) * D_MODEL ** -0.5
    return {"x": x, "positions": positions, "norm_w": norm_w, "w_in": w_in,
            "lam_re": lam_re, "lam_im": lam_im, "log_dt": log_dt,
            "b_re": b_re, "b_im": b_im, "c_re": c_re, "c_im": c_im,
            "d_skip": d_skip, "w_glu": w_glu, "q_norm_w": q_norm_w, "k_norm_w": k_norm_w,
            "w_ssm_out": w_ssm_out, "w_attn_out": w_attn_out, "w_o": w_o}


def reference(x, positions, norm_w, w_in, lam_re, lam_im, log_dt, b_re, b_im, c_re, c_im,
              d_skip, w_glu, q_norm_w, k_norm_w, w_ssm_out, w_attn_out, w_o):
    f32 = jnp.float32
    bsz, length, _ = x.shape
    offsets = np.cumsum(SPLITS)[:-1].tolist()
    for layer in range(DEPTH):
        h = rms_norm(x, norm_w[layer])
        proj = h @ w_in[layer].astype(f32)
        u_s, z_s, q, k, v, z_a, r_s, r_a = jnp.split(proj, offsets, axis=-1)

        y_s = s5_branch(u_s, lam_re[layer], lam_im[layer], log_dt[layer], b_re[layer], b_im[layer],
                        c_re[layer], c_im[layer], d_skip[layer], w_glu[layer])
        y_s = (y_s * jax.nn.silu(z_s)) @ w_ssm_out[layer].astype(f32)

        head_shape = (bsz, length, N_DIL, HEADS_PER_GROUP, HEAD_DIM)
        q = rms_norm(q.reshape(head_shape), q_norm_w[layer][None, None, :, None, :])
        k = rms_norm(k.reshape(head_shape), k_norm_w[layer][None, None, :, None, :])
        v = v.reshape(head_shape)
        q = apply_partial_rope(q, positions)
        k = apply_partial_rope(k, positions)
        y_a = dilated_attention(q, k, v)
        y_a = (y_a * jax.nn.silu(z_a)) @ w_attn_out[layer].astype(f32)

        m = jax.nn.sigmoid(r_s) * y_s + jax.nn.sigmoid(r_a) * y_a
        x = (x.astype(f32) + m @ w_o[layer].astype(f32)).astype(x.dtype)
    return x
```

```python
import functools
import math

import jax
import jax.numpy as jnp
from jax import lax
from jax.experimental import pallas as pl
from jax.experimental.pallas import tpu as pltpu

F32 = jnp.float32
BF16 = jnp.bfloat16

D_MODEL = 1024
SSM_GROUP = 16
SSM_GROUPS = D_MODEL // SSM_GROUP
SSM_STATE = 64
CHUNK = 16
CHUNK_W = CHUNK * SSM_GROUP
HEAD_DIM = 64
HEADS = 8
ATTN_W = HEADS * HEAD_DIM
DILATIONS = (1, 4, 16)
WINDOW_KEYS = 128
ROPE_DIM = HEAD_DIM // 4
ROPE_THETA = 500000.0
EPS = 1e-6
NEG_INF = -1e30
COL = 512
U_BLK, ZS_BLK, Q_BLK, K_BLK, V_BLK, ZA_BLK, RS_BLK, RA_BLK = 0, 2, 4, 7, 10, 13, 14, 16
N_BLK = 18
IN_WIDTH = N_BLK * COL
VMEM_LIMIT = 56 * 1024 * 1024


def _full_spec(shape):
    nd = len(shape)
    return pl.BlockSpec(shape, lambda *_: (0,) * nd, pipeline_mode=pl.Buffered(1))


def _ssm_prep_kernel(lam_ref, lrc_ref, lic_ref, ldtc_ref, btr_ref, bti_ref, crt_ref, cit_ref,
                     st_ref, mt_ref, rt_ref, coef_ref):
    lam = lam_ref[0]
    lr, li, dt = lam[0:1, :], lam[1:2, :], jnp.exp(lam[2:3, :])
    mag = jnp.exp(lr * dt)
    ar = mag * jnp.cos(li * dt)
    ai = mag * jnp.sin(li * dt)
    den = lr * lr + li * li
    nr = ar - 1.0
    fr = (nr * lr + ai * li) / den
    fi = (ai * lr - nr * li) / den
    btr, bti = btr_ref[0], bti_ref[0]
    bbr = fr * btr - fi * bti
    bbi = fr * bti + fi * btr
    rows = lax.broadcasted_iota(jnp.int32, (CHUNK_W, 2 * SSM_STATE), 0)
    lanes = lax.broadcasted_iota(jnp.int32, (CHUNK_W, 2 * SSM_STATE), 1)
    im_part = lanes >= SSM_STATE
    spow = (CHUNK - 1 - rows // SSM_GROUP).astype(F32)
    pmag = jnp.exp(lr * dt * spow)
    apr = pmag * jnp.cos(li * dt * spow)
    api = pmag * jnp.sin(li * dt * spow)
    wre = apr * bbr - api * bbi
    wim = apr * bbi + api * bbr
    st = jnp.where(im_part, wim, wre)
    st_sw = jnp.where(im_part, wre, wim)
    st_ref[0] = jnp.concatenate([st, st_sw], axis=1).astype(BF16)

    lrd = lrc_ref[0] * jnp.exp(ldtc_ref[0])
    lid = lic_ref[0] * jnp.exp(ldtc_ref[0])
    tl = lax.broadcasted_iota(jnp.int32, (SSM_STATE, CHUNK_W), 1)
    tpow = (tl // SSM_GROUP + 1).astype(F32)
    tmag = jnp.exp(lrd * tpow)
    arw = tmag * jnp.cos(lid * tpow)
    aiw = tmag * jnp.sin(lid * tpow)
    crt, cit = crt_ref[0], cit_ref[0]
    rt_re = crt * arw - cit * aiw
    rt_im = -(crt * aiw + cit * arw)
    rt_ref[0] = jnp.concatenate([rt_re, rt_im], axis=0).astype(BF16)

    ccw = jnp.concatenate([crt, -cit], axis=0)
    krw = jnp.dot(st, ccw, preferred_element_type=F32, precision=lax.Precision.HIGHEST)
    lane_blk = lax.broadcasted_iota(jnp.int32, (CHUNK_W, CHUNK_W), 1) // SSM_GROUP
    mt = jnp.zeros((CHUNK_W, CHUNK_W), F32)
    for t in range(CHUNK):
        sh = SSM_GROUP * (CHUNK - 1 - t)
        if sh == 0:
            shifted = krw
        else:
            shifted = jnp.concatenate([krw[sh:, :], jnp.zeros((sh, CHUNK_W), F32)], axis=0)
        mt = jnp.where(lane_blk == t, shifted, mt)
    mt_ref[0] = mt.astype(BF16)

    cmag = jnp.exp(lr * dt * float(CHUNK))
    a16 = cmag * jnp.cos(li * dt * float(CHUNK))
    ai16 = cmag * jnp.sin(li * dt * float(CHUNK))
    im_row = lax.broadcasted_iota(jnp.int32, (1, 2 * SSM_STATE), 1) >= SSM_STATE
    bco = jnp.where(im_row, ai16, -ai16)
    coef_ref[0] = jnp.concatenate([a16, bco, jnp.zeros((6, 2 * SSM_STATE), F32)], axis=0)


def _ssm_prep_call(lam_re, lam_im, log_dt, b_re, b_im, c_re, c_im):
    g, n, p = SSM_GROUPS, SSM_STATE, SSM_GROUP
    f = lambda a: a.astype(F32)
    row = lambda a: jnp.tile(f(a), (1, 2))[:, None, :]
    ldt_b = jnp.broadcast_to(f(log_dt)[:, None], (g, n))
    lam = jnp.concatenate([row(lam_re), row(lam_im), row(ldt_b), jnp.zeros((g, 5, 2 * n), F32)], axis=1)
    col = lambda a: f(a)[:, :, None]
    bt = lambda b: jnp.tile(f(b).transpose(0, 2, 1), (1, CHUNK, 2))
    ct = lambda c: jnp.tile(f(c).transpose(0, 2, 1), (1, 1, CHUNK))
    gspec = lambda shape: pl.BlockSpec((1,) + shape, lambda i: (i, 0, 0))
    return pl.pallas_call(
        _ssm_prep_kernel,
        grid=(g,),
        in_specs=[gspec((8, 2 * n)), gspec((n, 1)), gspec((n, 1)), gspec((n, 1)),
                  gspec((CHUNK_W, 2 * n)), gspec((CHUNK_W, 2 * n)),
                  gspec((n, CHUNK_W)), gspec((n, CHUNK_W))],
        out_specs=[gspec((CHUNK_W, 4 * n)), gspec((CHUNK_W, CHUNK_W)), gspec((2 * n, CHUNK_W)),
                   gspec((8, 2 * n))],
        out_shape=[jax.ShapeDtypeStruct((g, CHUNK_W, 4 * n), BF16),
                   jax.ShapeDtypeStruct((g, CHUNK_W, CHUNK_W), BF16),
                   jax.ShapeDtypeStruct((g, 2 * n, CHUNK_W), BF16),
                   jax.ShapeDtypeStruct((g, 8, 2 * n), F32)],
        compiler_params=pltpu.CompilerParams(dimension_semantics=("arbitrary",)),
        name="ssm_prep",
    )(lam, col(lam_re), col(lam_im), col(ldt_b), bt(b_re), bt(b_im), ct(c_re), ct(c_im))


def _rope_tables():
    lane = jnp.arange(128) % HEAD_DIM
    inv = ROPE_THETA ** (-jnp.arange(0, ROPE_DIM, 2, dtype=F32) / ROPE_DIM)
    half = ROPE_DIM // 2
    freq = jnp.where(lane < ROPE_DIM, inv[lane % half], 0.0)
    s_lo = jnp.where(lane < half, -1.0, 0.0)
    s_hi = jnp.where((lane >= half) & (lane < ROPE_DIM), 1.0, 0.0)
    return jnp.concatenate([freq[None], s_lo[None], s_hi[None], jnp.zeros((5, 128), F32)], axis=0).astype(F32)


def _proj_kernel(x_ref, pos_ref, nw_ref, w_ref, qkw_ref, ones_ref, rope_ref, o_ref):
    x = x_ref[...]
    ms = jnp.mean(x * x, axis=-1, keepdims=True)
    h = (x * lax.rsqrt(ms + EPS) * nw_ref[...]).astype(BF16)
    ang = pos_ref[...].astype(F32) * rope_ref[0:1, :]
    cosv = jnp.cos(ang)
    sinv = jnp.sin(ang)
    s_lo = sinv * rope_ref[1:2, :]
    s_hi = sinv * rope_ref[2:3, :]
    for j in range(N_BLK):
        acc = jnp.dot(h, w_ref[:, j * COL:(j + 1) * COL], preferred_element_type=F32)
        if ZS_BLK <= j < Q_BLK or j == ZA_BLK:
            res = acc * jax.nn.sigmoid(acc)
        elif j >= RS_BLK:
            res = jax.nn.sigmoid(acc)
        elif Q_BLK <= j < V_BLK:
            ss = jnp.dot((acc * acc).astype(BF16), ones_ref[...], preferred_element_type=F32)
            y = acc * lax.rsqrt(ss * (1.0 / HEAD_DIM) + EPS) * qkw_ref[j - Q_BLK:j - Q_BLK + 1, :]
            parts = []
            for c in range(COL // 128):
                yc = y[:, c * 128:(c + 1) * 128]
                parts.append(yc * cosv + pltpu.roll(yc, 128 - ROPE_DIM // 2, 1) * s_lo
                             + pltpu.roll(yc, ROPE_DIM // 2, 1) * s_hi)
            res = jnp.concatenate(parts, axis=1)
        else:
            res = acc
        o_ref[:, j * COL:(j + 1) * COL] = res.astype(BF16)


def _proj_call(x2d, pos2d, norm_w, w_in, q_norm_w, k_norm_w, tm=256):
    t = x2d.shape[0]
    scale = 1.0 / math.sqrt(HEAD_DIM)
    qkw = jnp.concatenate([jnp.tile(q_norm_w.astype(F32) * scale, (1, HEADS)),
                           jnp.tile(k_norm_w.astype(F32), (1, HEADS)),
                           jnp.zeros((2, ATTN_W), F32)], axis=0)
    hid = jnp.arange(ATTN_W) // HEAD_DIM
    ones = (hid[:, None] == hid[None, :]).astype(BF16)
    return pl.pallas_call(
        _proj_kernel,
        grid=(t // tm,),
        in_specs=[pl.BlockSpec((tm, D_MODEL), lambda i: (i, 0)),
                  pl.BlockSpec((tm, 1), lambda i: (i, 0)),
                  _full_spec((1, D_MODEL)),
                  _full_spec((D_MODEL, IN_WIDTH)),
                  _full_spec((8, ATTN_W)),
                  _full_spec((ATTN_W, ATTN_W)),
                  _full_spec((8, 128))],
        out_specs=pl.BlockSpec((tm, IN_WIDTH), lambda i: (i, 0)),
        out_shape=jax.ShapeDtypeStruct((t, IN_WIDTH), BF16),
        compiler_params=pltpu.CompilerParams(dimension_semantics=("arbitrary",),
                                             vmem_limit_bytes=VMEM_LIMIT),
        name="proj",
    )(x2d, pos2d, norm_w.astype(F32)[None, :], w_in.astype(BF16), qkw, ones, _rope_tables())


def _ssm_kernel(u_ref, st_ref, mt_ref, rt_ref, coef_ref, y_ref, hloc_ref, hprev_ref):
    nc, bsz = u_ref.shape[1], u_ref.shape[2]
    ns2 = 2 * SSM_STATE
    u = u_ref[0].reshape(nc * bsz, CHUNK_W)
    hloc_ref[...] = jnp.dot(u, st_ref[0], preferred_element_type=F32)
    a = jnp.broadcast_to(coef_ref[0, 0:1, :], (bsz, ns2))
    b = jnp.broadcast_to(coef_ref[0, 1:2, :], (bsz, ns2))

    def step(c, carry):
        hx, hy = carry
        r = pl.multiple_of(c * bsz, bsz)
        hprev_ref[pl.ds(r, bsz), :] = hx.astype(BF16)
        px = hloc_ref[pl.ds(r, bsz), 0:ns2]
        py = hloc_ref[pl.ds(r, bsz), ns2:2 * ns2]
        return a * hx + b * hy + px, a * hy - b * hx + py

    zero = jnp.zeros((bsz, ns2), F32)
    lax.fori_loop(0, nc, step, (zero, zero), unroll=8)
    y = jnp.dot(u, mt_ref[0], preferred_element_type=F32)
    y = y + jnp.dot(hprev_ref[...], rt_ref[0], preferred_element_type=F32)
    y_ref[0] = y.reshape(nc, bsz, CHUNK_W).astype(BF16)


def _ssm_call(u2, st, mt, rt, coef):
    g, nc, bsz, _ = u2.shape
    gspec = lambda shape: pl.BlockSpec((1,) + shape, lambda i: (i,) + (0,) * len(shape))
    return pl.pallas_call(
        _ssm_kernel,
        grid=(g,),
        in_specs=[gspec((nc, bsz, CHUNK_W)), gspec((CHUNK_W, 4 * SSM_STATE)), gspec((CHUNK_W, CHUNK_W)),
                  gspec((2 * SSM_STATE, CHUNK_W)), gspec((8, 2 * SSM_STATE))],
        out_specs=gspec((nc, bsz, CHUNK_W)),
        out_shape=jax.ShapeDtypeStruct(u2.shape, BF16),
        scratch_shapes=[pltpu.VMEM((nc * bsz, 4 * SSM_STATE), F32),
                        pltpu.VMEM((nc * bsz, 2 * SSM_STATE), BF16)],
        compiler_params=pltpu.CompilerParams(dimension_semantics=("arbitrary",),
                                             vmem_limit_bytes=VMEM_LIMIT),
        name="ssm_scan",
    )(u2, st, mt, rt, coef)


def _attn_kernel(q_ref, kc_ref, kp_ref, vc_ref, vp_ref, o_ref, lse_ref, kbuf, vbuf):
    tq = q_ref.shape[1]
    qb = 128
    slab = pl.program_id(2)
    kbuf[0:qb, :] = kp_ref[0]
    kbuf[qb:, :] = kc_ref[0]
    vbuf[0:qb, :] = vp_ref[0]
    vbuf[qb:, :] = vc_ref[0]
    qi = lax.broadcasted_iota(jnp.int32, (qb, 2 * qb), 0)
    kk = lax.broadcasted_iota(jnp.int32, (qb, 2 * qb), 1)
    band = jnp.where((kk >= qi) & (kk <= qi + WINDOW_KEYS), 0.0, NEG_INF).astype(F32)
    before_start = jnp.where(kk < qb, NEG_INF, 0.0).astype(F32)
    lo_q = lax.broadcasted_iota(jnp.int32, (qb, 128), 1) < HEAD_DIM
    lo_kv = lax.broadcasted_iota(jnp.int32, (2 * qb, 128), 1) < HEAD_DIM

    def block(j, carry):
        r0 = pl.multiple_of(j * qb, qb)
        first = jnp.logical_and(slab == 0, j == 0)
        madd = band + jnp.where(first, before_start, 0.0)
        for hp in range(ATTN_W // 128):
            cs = slice(hp * 128, (hp + 1) * 128)
            qp = q_ref[0, pl.ds(r0, qb), cs]
            kp = kbuf[pl.ds(r0, 2 * qb), cs]
            vp = vbuf[pl.ds(r0, 2 * qb), cs]
            acc = jnp.zeros((qb, 128), F32)
            lse = jnp.zeros((qb, 128), F32)
            for side in range(2):
                sel_q = lo_q if side == 0 else jnp.logical_not(lo_q)
                sel_kv = lo_kv if side == 0 else jnp.logical_not(lo_kv)
                qm = jnp.where(sel_q, qp, jnp.zeros_like(qp))
                s = lax.dot_general(qm, kp, (((1,), (1,)), ((), ())), preferred_element_type=F32)
                s = s + madd
                m = jnp.max(s, axis=1, keepdims=True)
                p = jnp.exp(s - m)
                l = jnp.sum(p, axis=1, keepdims=True)
                vm = jnp.where(sel_kv, vp, jnp.zeros_like(vp))
                pv = jnp.dot(p.astype(BF16), vm, preferred_element_type=F32)
                acc = acc + pv * (1.0 / l)
                lse = jnp.where(sel_q, m + jnp.log(l), lse)
            o_ref[0, pl.ds(r0, qb), cs] = acc.astype(BF16)
            lse_ref[0, pl.ds(r0, qb), cs] = lse
        return carry

    lax.fori_loop(0, tq // qb, block, 0)


def _attn_call(proj, bsz, length, group):
    d = DILATIONS[group]
    ld = length // d
    tq = min(512, ld)
    nslab = ld // tq
    per = tq // 128
    qkv = proj.reshape(bsz, ld, d * IN_WIDTH)
    cur = lambda blk: pl.BlockSpec((1, tq, COL), lambda b, r, i: (b, i, r * N_BLK + blk + group))
    prev = lambda blk: pl.BlockSpec(
        (1, 128, COL), lambda b, r, i: (b, jnp.maximum(i * per - 1, 0), r * N_BLK + blk + group))
    ospec = pl.BlockSpec((1, tq, COL), lambda b, r, i: (b, i, r))
    o, lse = pl.pallas_call(
        _attn_kernel,
        grid=(bsz, d, nslab),
        in_specs=[cur(Q_BLK), cur(K_BLK), prev(K_BLK), cur(V_BLK), prev(V_BLK)],
        out_specs=[ospec, ospec],
        out_shape=[jax.ShapeDtypeStruct((bsz, ld, d * ATTN_W), BF16),
                   jax.ShapeDtypeStruct((bsz, ld, d * ATTN_W), F32)],
        scratch_shapes=[pltpu.VMEM((tq + 128, COL), BF16), pltpu.VMEM((tq + 128, COL), BF16)],
        compiler_params=pltpu.CompilerParams(dimension_semantics=("arbitrary",) * 3,
                                             vmem_limit_bytes=VMEM_LIMIT),
        name=f"attn_d{d}",
    )(qkv, qkv, qkv, qkv, qkv)
    t = bsz * length
    return o.reshape(t, ATTN_W), lse.reshape(t, ATTN_W)


def _merge_kernel(x_ref, y_ref, u_ref, zs_ref, za_ref, rs_ref, ra_ref,
                  a0_ref, a1_ref, a2_ref, l0_ref, l1_ref, l2_ref,
                  dskip_ref, wglu_ref, wso_ref, wao_ref, wo_ref, o_ref):
    y = y_ref[...].astype(F32) + dskip_ref[...] * u_ref[...].astype(F32)
    g = jax.nn.gelu(y, approximate=True)
    gate = jax.nn.sigmoid(jnp.dot(g.astype(BF16), wglu_ref[...], preferred_element_type=F32))
    ys_in = (g * gate * zs_ref[...].astype(F32)).astype(BF16)
    ys = jnp.dot(ys_in, wso_ref[...], preferred_element_type=F32)

    l0, l1, l2 = l0_ref[...], l1_ref[...], l2_ref[...]
    lm = jnp.maximum(jnp.maximum(l0, l1), l2)
    e0, e1, e2 = jnp.exp(l0 - lm), jnp.exp(l1 - lm), jnp.exp(l2 - lm)
    att = (e0 * a0_ref[...].astype(F32) + e1 * a1_ref[...].astype(F32)
           + e2 * a2_ref[...].astype(F32)) / (e0 + e1 + e2)
    ya_in = (att * za_ref[...].astype(F32)).astype(BF16)
    ya = jnp.dot(ya_in, wao_ref[...], preferred_element_type=F32)

    m = rs_ref[...].astype(F32) * ys + ra_ref[...].astype(F32) * ya
    o_ref[...] = x_ref[...] + jnp.dot(m.astype(BF16), wo_ref[...], preferred_element_type=F32)


def _merge_call(x2d, y2d, proj, attn, d_skip, w_glu, w_ssm_out, w_attn_out, w_o, tm=512):
    t = x2d.shape[0]
    wide = lambda blk: pl.BlockSpec((tm, D_MODEL), lambda i: (i, blk))
    half = lambda blk: pl.BlockSpec((tm, ATTN_W), lambda i: (i, blk))
    (a0, l0), (a1, l1), (a2, l2) = attn
    return pl.pallas_call(
        _merge_kernel,
        grid=(t // tm,),
        in_specs=[wide(0), wide(0),
                  wide(U_BLK // 2), wide(ZS_BLK // 2), half(ZA_BLK), wide(RS_BLK // 2), wide(RA_BLK // 2),
                  half(0), half(0), half(0), half(0), half(0), half(0),
                  _full_spec((1, D_MODEL)), _full_spec((D_MODEL, D_MODEL)), _full_spec((D_MODEL, D_MODEL)),
                  _full_spec((ATTN_W, D_MODEL)), _full_spec((D_MODEL, D_MODEL))],
        out_specs=wide(0),
        out_shape=jax.ShapeDtypeStruct((t, D_MODEL), F32),
        compiler_params=pltpu.CompilerParams(dimension_semantics=("arbitrary",),
                                             vmem_limit_bytes=VMEM_LIMIT),
        name="merge",
    )(x2d, y2d, proj, proj, proj, proj, proj, a0, a1, a2, l0, l1, l2,
      d_skip.astype(F32)[None, :], w_glu.astype(BF16), w_ssm_out.astype(BF16),
      w_attn_out.astype(BF16), w_o.astype(BF16))


def kernel(x, positions, norm_w, w_in, lam_re, lam_im, log_dt, b_re, b_im, c_re, c_im, d_skip, w_glu,
           q_norm_w, k_norm_w, w_ssm_out, w_attn_out, w_o):
    bsz, length, _ = x.shape
    t = bsz * length
    nc = length // CHUNK
    pos2d = positions.reshape(t, 1).astype(jnp.int32)
    x2d = x.reshape(t, D_MODEL).astype(F32)
    for layer in range(norm_w.shape[0]):
        st, mt, rt, coef = _ssm_prep_call(lam_re[layer], lam_im[layer], log_dt[layer], b_re[layer],
                                          b_im[layer], c_re[layer], c_im[layer])
        proj = _proj_call(x2d, pos2d, norm_w[layer], w_in[layer], q_norm_w[layer], k_norm_w[layer])
        u2 = proj[:, :D_MODEL].reshape(bsz, nc, CHUNK, SSM_GROUPS, SSM_GROUP)
        u2 = u2.transpose(3, 1, 0, 2, 4).reshape(SSM_GROUPS, nc, bsz, CHUNK_W)
        y2 = _ssm_call(u2, st, mt, rt, coef)
        y2d = y2.reshape(SSM_GROUPS, nc, bsz, CHUNK, SSM_GROUP).transpose(2, 1, 3, 0, 4).reshape(t, D_MODEL)
        attn = [_attn_call(proj, bsz, length, gi) for gi in range(len(DILATIONS))]
        x2d = _merge_call(x2d, y2d, proj, attn, d_skip[layer], w_glu[layer], w_ssm_out[layer],
                          w_attn_out[layer], w_o[layer])
    return x2d.reshape(bsz, length, D_MODEL).astype(x.dtype)
```

```python
import functools
import math

import jax
import jax.numpy as jnp
from jax import lax
from jax.experimental import pallas as pl
from jax.experimental.pallas import tpu as pltpu

F32 = jnp.float32
BF16 = jnp.bfloat16

D_MODEL = 1024
SSM_GROUP = 16
SSM_GROUPS = D_MODEL // SSM_GROUP
SSM_STATE = 64
CHUNK = 16
CHUNK_W = CHUNK * SSM_GROUP
HEAD_DIM = 64
HEADS = 8
ATTN_W = HEADS * HEAD_DIM
DILATIONS = (1, 4, 16)
WINDOW_KEYS = 128
ROPE_DIM = HEAD_DIM // 4
ROPE_THETA = 500000.0
EPS = 1e-6
NEG_INF = -1e30
COL = 512
U_BLK, ZS_BLK, Q_BLK, K_BLK, V_BLK, ZA_BLK, RS_BLK, RA_BLK = 0, 2, 4, 7, 10, 13, 14, 16
N_BLK = 18
IN_WIDTH = N_BLK * COL
NAT_U, NAT_ZS, NAT_Q, NAT_K, NAT_V, NAT_ZA, NAT_RS, NAT_RA = 0, 2, 4, 5, 6, 7, 8, 10
NAT_W = 12 * COL
QKV_W = 3 * COL
LANES = 128
VMEM_LIMIT = 56 * 1024 * 1024


def _full_spec(shape):
    nd = len(shape)
    return pl.BlockSpec(shape, lambda *_: (0,) * nd, pipeline_mode=pl.Buffered(1))


def _ssm_prep_kernel(lam_ref, lrc_ref, lic_ref, ldtc_ref, btr_ref, bti_ref, crt_ref, cit_ref,
                     st_ref, mt_ref, rt_ref, coef_ref):
    lam = lam_ref[0]
    lr, li, dt = lam[0:1, :], lam[1:2, :], jnp.exp(lam[2:3, :])
    mag = jnp.exp(lr * dt)
    ar = mag * jnp.cos(li * dt)
    ai = mag * jnp.sin(li * dt)
    den = lr * lr + li * li
    nr = ar - 1.0
    fr = (nr * lr + ai * li) / den
    fi = (ai * lr - nr * li) / den
    btr, bti = btr_ref[0], bti_ref[0]
    bbr = fr * btr - fi * bti
    bbi = fr * bti + fi * btr
    rows = lax.broadcasted_iota(jnp.int32, (CHUNK_W, 2 * SSM_STATE), 0)
    lanes = lax.broadcasted_iota(jnp.int32, (CHUNK_W, 2 * SSM_STATE), 1)
    im_part = lanes >= SSM_STATE
    spow = (CHUNK - 1 - rows // SSM_GROUP).astype(F32)
    pmag = jnp.exp(lr * dt * spow)
    apr = pmag * jnp.cos(li * dt * spow)
    api = pmag * jnp.sin(li * dt * spow)
    wre = apr * bbr - api * bbi
    wim = apr * bbi + api * bbr
    st = jnp.where(im_part, wim, wre)
    st_sw = jnp.where(im_part, wre, wim)
    st_ref[0] = jnp.concatenate([st, st_sw], axis=1).astype(BF16)

    lrd = lrc_ref[0] * jnp.exp(ldtc_ref[0])
    lid = lic_ref[0] * jnp.exp(ldtc_ref[0])
    tl = lax.broadcasted_iota(jnp.int32, (SSM_STATE, CHUNK_W), 1)
    tpow = (tl // SSM_GROUP + 1).astype(F32)
    tmag = jnp.exp(lrd * tpow)
    arw = tmag * jnp.cos(lid * tpow)
    aiw = tmag * jnp.sin(lid * tpow)
    crt, cit = crt_ref[0], cit_ref[0]
    rt_re = crt * arw - cit * aiw
    rt_im = -(crt * aiw + cit * arw)
    rt_ref[0] = jnp.concatenate([rt_re, rt_im], axis=0).astype(BF16)

    ccw = jnp.concatenate([crt, -cit], axis=0)
    krw = jnp.dot(st, ccw, preferred_element_type=F32, precision=lax.Precision.HIGHEST)
    lane_blk = lax.broadcasted_iota(jnp.int32, (CHUNK_W, CHUNK_W), 1) // SSM_GROUP
    mt = jnp.zeros((CHUNK_W, CHUNK_W), F32)
    for t in range(CHUNK):
        sh = SSM_GROUP * (CHUNK - 1 - t)
        if sh == 0:
            shifted = krw
        else:
            shifted = jnp.concatenate([krw[sh:, :], jnp.zeros((sh, CHUNK_W), F32)], axis=0)
        mt = jnp.where(lane_blk == t, shifted, mt)
    mt_ref[0] = mt.astype(BF16)

    cmag = jnp.exp(lr * dt * float(CHUNK))
    a16 = cmag * jnp.cos(li * dt * float(CHUNK))
    ai16 = cmag * jnp.sin(li * dt * float(CHUNK))
    im_row = lax.broadcasted_iota(jnp.int32, (1, 2 * SSM_STATE), 1) >= SSM_STATE
    bco = jnp.where(im_row, ai16, -ai16)
    coef_ref[0] = jnp.concatenate([a16, bco, jnp.zeros((6, 2 * SSM_STATE), F32)], axis=0)


def _ssm_prep_call(lam_re, lam_im, log_dt, b_re, b_im, c_re, c_im):
    g, n, p = SSM_GROUPS, SSM_STATE, SSM_GROUP
    f = lambda a: a.astype(F32)
    row = lambda a: jnp.tile(f(a), (1, 2))[:, None, :]
    ldt_b = jnp.broadcast_to(f(log_dt)[:, None], (g, n))
    lam = jnp.concatenate([row(lam_re), row(lam_im), row(ldt_b), jnp.zeros((g, 5, 2 * n), F32)], axis=1)
    col = lambda a: f(a)[:, :, None]
    bt = lambda b: jnp.tile(f(b).transpose(0, 2, 1), (1, CHUNK, 2))
    ct = lambda c: jnp.tile(f(c).transpose(0, 2, 1), (1, 1, CHUNK))
    gspec = lambda shape: pl.BlockSpec((1,) + shape, lambda i: (i, 0, 0))
    return pl.pallas_call(
        _ssm_prep_kernel,
        grid=(g,),
        in_specs=[gspec((8, 2 * n)), gspec((n, 1)), gspec((n, 1)), gspec((n, 1)),
                  gspec((CHUNK_W, 2 * n)), gspec((CHUNK_W, 2 * n)),
                  gspec((n, CHUNK_W)), gspec((n, CHUNK_W))],
        out_specs=[gspec((CHUNK_W, 4 * n)), gspec((CHUNK_W, CHUNK_W)), gspec((2 * n, CHUNK_W)),
                   gspec((8, 2 * n))],
        out_shape=[jax.ShapeDtypeStruct((g, CHUNK_W, 4 * n), BF16),
                   jax.ShapeDtypeStruct((g, CHUNK_W, CHUNK_W), BF16),
                   jax.ShapeDtypeStruct((g, 2 * n, CHUNK_W), BF16),
                   jax.ShapeDtypeStruct((g, 8, 2 * n), F32)],
        compiler_params=pltpu.CompilerParams(dimension_semantics=("arbitrary",)),
        name="ssm_prep",
    )(lam, col(lam_re), col(lam_im), col(ldt_b), bt(b_re), bt(b_im), ct(c_re), ct(c_im))


def _rope_tables():
    lane = jnp.arange(128) % HEAD_DIM
    inv = ROPE_THETA ** (-jnp.arange(0, ROPE_DIM, 2, dtype=F32) / ROPE_DIM)
    half = ROPE_DIM // 2
    freq = jnp.where(lane < ROPE_DIM, inv[lane % half], 0.0)
    s_lo = jnp.where(lane < half, -1.0, 0.0)
    s_hi = jnp.where((lane >= half) & (lane < ROPE_DIM), 1.0, 0.0)
    return jnp.concatenate([freq[None], s_lo[None], s_hi[None], jnp.zeros((5, 128), F32)], axis=0).astype(F32)


_NAT_DST = {0: NAT_U, 1: NAT_U + 1, 2: NAT_ZS, 3: NAT_ZS + 1, Q_BLK: NAT_Q, K_BLK: NAT_K, V_BLK: NAT_V,
            ZA_BLK: NAT_ZA, 14: NAT_RS, 15: NAT_RS + 1, 16: NAT_RA, 17: NAT_RA + 1}


def _proj_kernel(x_ref, pos_ref, nw_ref, w_ref, qkw_ref, ones_ref, rope_ref, nat_ref, d4_ref, d16_ref, perm_ref):
    tm = x_ref.shape[0]
    x = x_ref[...]
    ms = jnp.mean(x * x, axis=-1, keepdims=True)
    h = (x * lax.rsqrt(ms + EPS) * nw_ref[...]).astype(BF16)
    ang = pos_ref[...].astype(F32) * rope_ref[0:1, :]
    cosv = jnp.cos(ang)
    sinv = jnp.sin(ang)
    s_lo = sinv * rope_ref[1:2, :]
    s_hi = sinv * rope_ref[2:3, :]
    for j in range(N_BLK):
        acc = jnp.dot(h, w_ref[:, j * COL:(j + 1) * COL], preferred_element_type=F32)
        if ZS_BLK <= j < Q_BLK or j == ZA_BLK:
            res = acc * jax.nn.sigmoid(acc)
        elif j >= RS_BLK:
            res = jax.nn.sigmoid(acc)
        elif Q_BLK <= j < V_BLK:
            ss = jnp.dot((acc * acc).astype(BF16), ones_ref[...], preferred_element_type=F32)
            y = acc * lax.rsqrt(ss * (1.0 / HEAD_DIM) + EPS) * qkw_ref[j - Q_BLK:j - Q_BLK + 1, :]
            parts = []
            for c in range(COL // LANES):
                yc = y[:, c * LANES:(c + 1) * LANES]
                parts.append(yc * cosv + pltpu.roll(yc, LANES - ROPE_DIM // 2, 1) * s_lo
                             + pltpu.roll(yc, ROPE_DIM // 2, 1) * s_hi)
            res = jnp.concatenate(parts, axis=1)
        else:
            res = acc
        if j in _NAT_DST:
            dst = _NAT_DST[j]
            nat_ref[:, dst * COL:(dst + 1) * COL] = res.astype(BF16)
        else:
            kind, group = divmod(j - Q_BLK, len(DILATIONS))
            d = DILATIONS[group]
            out_ref = d4_ref if group == 1 else d16_ref
            for c in range(COL // LANES):
                perm_ref[c] = res[:, c * LANES:(c + 1) * LANES]
            for r in range(d):
                rows = [perm_ref[c, pl.ds(r, tm // d, stride=d), :] for c in range(COL // LANES)]
                out_ref[r, :, kind * COL:(kind + 1) * COL] = jnp.concatenate(rows, axis=1).astype(BF16)


def _proj_call(x, positions, norm_w, w_in, q_norm_w, k_norm_w, tm=256):
    bsz, length, _ = x.shape
    scale = 1.0 / math.sqrt(HEAD_DIM)
    qkw = jnp.concatenate([jnp.tile(q_norm_w.astype(F32) * scale, (1, HEADS)),
                           jnp.tile(k_norm_w.astype(F32), (1, HEADS)),
                           jnp.zeros((2, ATTN_W), F32)], axis=0)
    hid = jnp.arange(ATTN_W) // HEAD_DIM
    ones = (hid[:, None] == hid[None, :]).astype(BF16)
    d4, d16 = DILATIONS[1], DILATIONS[2]
    return pl.pallas_call(
        _proj_kernel,
        grid=(bsz, length // tm),
        in_specs=[pl.BlockSpec((None, tm, D_MODEL), lambda b, i: (b, i, 0)),
                  pl.BlockSpec((None, tm, 1), lambda b, i: (b, i, 0)),
                  _full_spec((1, D_MODEL)),
                  _full_spec((D_MODEL, IN_WIDTH)),
                  _full_spec((8, ATTN_W)),
                  _full_spec((ATTN_W, ATTN_W)),
                  _full_spec((8, LANES))],
        out_specs=[pl.BlockSpec((None, tm, NAT_W), lambda b, i: (b, i, 0)),
                   pl.BlockSpec((None, d4, tm // d4, QKV_W), lambda b, i: (b, 0, i, 0)),
                   pl.BlockSpec((None, d16, tm // d16, QKV_W), lambda b, i: (b, 0, i, 0))],
        out_shape=[jax.ShapeDtypeStruct((bsz, length, NAT_W), BF16),
                   jax.ShapeDtypeStruct((bsz, d4, length // d4, QKV_W), BF16),
                   jax.ShapeDtypeStruct((bsz, d16, length // d16, QKV_W), BF16)],
        scratch_shapes=[pltpu.VMEM((COL // LANES, tm, LANES), F32)],
        compiler_params=pltpu.CompilerParams(dimension_semantics=("arbitrary", "arbitrary"),
                                             vmem_limit_bytes=VMEM_LIMIT),
        name="proj",
    )(x.astype(F32), positions.astype(jnp.int32)[:, :, None], norm_w.astype(F32)[None, :],
      w_in.astype(BF16), qkw, ones, _rope_tables())


def _ssm_kernel(u_ref, st_ref, mt_ref, rt_ref, coef_ref, y_ref, hloc_ref, hprev_ref):
    nc, bsz = u_ref.shape[1], u_ref.shape[2]
    ns2 = 2 * SSM_STATE
    u = u_ref[0].reshape(nc * bsz, CHUNK_W)
    hloc_ref[...] = jnp.dot(u, st_ref[0], preferred_element_type=F32)
    a = jnp.broadcast_to(coef_ref[0, 0:1, :], (bsz, ns2))
    b = jnp.broadcast_to(coef_ref[0, 1:2, :], (bsz, ns2))

    def step(c, carry):
        hx, hy = carry
        r = pl.multiple_of(c * bsz, bsz)
        hprev_ref[pl.ds(r, bsz), :] = hx.astype(BF16)
        px = hloc_ref[pl.ds(r, bsz), 0:ns2]
        py = hloc_ref[pl.ds(r, bsz), ns2:2 * ns2]
        return a * hx + b * hy + px, a * hy - b * hx + py

    zero = jnp.zeros((bsz, ns2), F32)
    lax.fori_loop(0, nc, step, (zero, zero), unroll=8)
    y = jnp.dot(u, mt_ref[0], preferred_element_type=F32)
    y = y + jnp.dot(hprev_ref[...], rt_ref[0], preferred_element_type=F32)
    y_ref[0] = y.reshape(nc, bsz, CHUNK_W).astype(BF16)


def _ssm_call(u2, st, mt, rt, coef):
    g, nc, bsz, _ = u2.shape
    gspec = lambda shape: pl.BlockSpec((1,) + shape, lambda i: (i,) + (0,) * len(shape))
    return pl.pallas_call(
        _ssm_kernel,
        grid=(g,),
        in_specs=[gspec((nc, bsz, CHUNK_W)), gspec((CHUNK_W, 4 * SSM_STATE)), gspec((CHUNK_W, CHUNK_W)),
                  gspec((2 * SSM_STATE, CHUNK_W)), gspec((8, 2 * SSM_STATE))],
        out_specs=gspec((nc, bsz, CHUNK_W)),
        out_shape=jax.ShapeDtypeStruct(u2.shape, BF16),
        scratch_shapes=[pltpu.VMEM((nc * bsz, 4 * SSM_STATE), F32),
                        pltpu.VMEM((nc * bsz, 2 * SSM_STATE), BF16)],
        compiler_params=pltpu.CompilerParams(dimension_semantics=("arbitrary",),
                                             vmem_limit_bytes=VMEM_LIMIT),
        name="ssm_scan",
    )(u2, st, mt, rt, coef)


def _attn_kernel(q_ref, kc_ref, kp_ref, vc_ref, vp_ref, o_ref, lse_ref, kbuf, vbuf):
    tq = q_ref.shape[0]
    qb = 128
    slab = pl.program_id(2)
    kbuf[0:qb, :] = kp_ref[...]
    kbuf[qb:, :] = kc_ref[...]
    vbuf[0:qb, :] = vp_ref[...]
    vbuf[qb:, :] = vc_ref[...]
    qi = lax.broadcasted_iota(jnp.int32, (qb, 2 * qb), 0)
    kk = lax.broadcasted_iota(jnp.int32, (qb, 2 * qb), 1)
    band = jnp.where((kk >= qi) & (kk <= qi + WINDOW_KEYS), 0.0, NEG_INF).astype(F32)
    before_start = jnp.where(kk < qb, NEG_INF, 0.0).astype(F32)
    lo_q = lax.broadcasted_iota(jnp.int32, (qb, 128), 1) < HEAD_DIM
    lo_kv = lax.broadcasted_iota(jnp.int32, (2 * qb, 128), 1) < HEAD_DIM

    def block(j, carry):
        r0 = pl.multiple_of(j * qb, qb)
        first = jnp.logical_and(slab == 0, j == 0)
        madd = band + jnp.where(first, before_start, 0.0)
        for hp in range(ATTN_W // 128):
            cs = slice(hp * 128, (hp + 1) * 128)
            qp = q_ref[pl.ds(r0, qb), cs]
            kp = kbuf[pl.ds(r0, 2 * qb), cs]
            vp = vbuf[pl.ds(r0, 2 * qb), cs]
            acc = jnp.zeros((qb, 128), F32)
            lse = jnp.zeros((qb, 128), F32)
            for side in range(2):
                sel_q = lo_q if side == 0 else jnp.logical_not(lo_q)
                sel_kv = lo_kv if side == 0 else jnp.logical_not(lo_kv)
                qm = jnp.where(sel_q, qp, jnp.zeros_like(qp))
                s = lax.dot_general(qm, kp, (((1,), (1,)), ((), ())), preferred_element_type=F32)
                s = s + madd
                m = jnp.max(s, axis=1, keepdims=True)
                p = jnp.exp(s - m)
                l = jnp.sum(p, axis=1, keepdims=True)
                vm = jnp.where(sel_kv, vp, jnp.zeros_like(vp))
                pv = jnp.dot(p.astype(BF16), vm, preferred_element_type=F32)
                acc = acc + pv * (1.0 / l)
                lse = jnp.where(sel_q, m + jnp.log(l), lse)
            o_ref[pl.ds(r0, qb), cs] = acc.astype(BF16)
            lse_ref[pl.ds(r0, qb), cs] = lse
        return carry

    lax.fori_loop(0, tq // qb, block, 0)


def _attn_call(qkv, group, blocks):
    bsz, d, ld, _ = qkv.shape
    tq = min(512, ld)
    nslab = ld // tq
    per = tq // 128
    cur = lambda blk: pl.BlockSpec((None, None, tq, COL), lambda b, r, i: (b, r, i, blk))
    prev = lambda blk: pl.BlockSpec(
        (None, None, 128, COL), lambda b, r, i: (b, r, jnp.maximum(i * per - 1, 0), blk))
    ospec = pl.BlockSpec((None, None, tq, COL), lambda b, r, i: (b, r, i, 0))
    qb, kb, vb = blocks
    return pl.pallas_call(
        _attn_kernel,
        grid=(bsz, d, nslab),
        in_specs=[cur(qb), cur(kb), prev(kb), cur(vb), prev(vb)],
        out_specs=[ospec, ospec],
        out_shape=[jax.ShapeDtypeStruct((bsz, d, ld, ATTN_W), BF16),
                   jax.ShapeDtypeStruct((bsz, d, ld, ATTN_W), F32)],
        scratch_shapes=[pltpu.VMEM((tq + 128, COL), BF16), pltpu.VMEM((tq + 128, COL), BF16)],
        compiler_params=pltpu.CompilerParams(dimension_semantics=("arbitrary",) * 3,
                                             vmem_limit_bytes=VMEM_LIMIT),
        name=f"attn_d{DILATIONS[group]}",
    )(qkv, qkv, qkv, qkv, qkv)


def _merge_kernel(x_ref, y_ref, u_ref, zs_ref, za_ref, rs_ref, ra_ref,
                  a0_ref, a1_ref, a2_ref, l0_ref, l1_ref, l2_ref,
                  dskip_ref, wglu_ref, wso_ref, wao_ref, wo_ref, o_ref, il_ref):
    tm = x_ref.shape[0]

    def natural_order(ref):
        d = ref.shape[0]
        for r in range(d):
            blk = ref[r].astype(F32)
            for c in range(ATTN_W // LANES):
                il_ref[c, pl.ds(r, tm // d, stride=d), :] = blk[:, c * LANES:(c + 1) * LANES]
        return jnp.concatenate([il_ref[c] for c in range(ATTN_W // LANES)], axis=1)

    y = y_ref[...].astype(F32) + dskip_ref[...] * u_ref[...].astype(F32)
    g = jax.nn.gelu(y, approximate=True)
    gate = jax.nn.sigmoid(jnp.dot(g.astype(BF16), wglu_ref[...], preferred_element_type=F32))
    ys_in = (g * gate * zs_ref[...].astype(F32)).astype(BF16)
    ys = jnp.dot(ys_in, wso_ref[...], preferred_element_type=F32)

    l0, l1, l2 = l0_ref[...], natural_order(l1_ref), natural_order(l2_ref)
    lm = jnp.maximum(jnp.maximum(l0, l1), l2)
    e0, e1, e2 = jnp.exp(l0 - lm), jnp.exp(l1 - lm), jnp.exp(l2 - lm)
    att = (e0 * a0_ref[...].astype(F32) + e1 * natural_order(a1_ref)
           + e2 * natural_order(a2_ref)) / (e0 + e1 + e2)
    ya_in = (att * za_ref[...].astype(F32)).astype(BF16)
    ya = jnp.dot(ya_in, wao_ref[...], preferred_element_type=F32)

    m = rs_ref[...].astype(F32) * ys + ra_ref[...].astype(F32) * ya
    o_ref[...] = x_ref[...] + jnp.dot(m.astype(BF16), wo_ref[...], preferred_element_type=F32)


def _merge_call(x, y, nat, attn, d_skip, w_glu, w_ssm_out, w_attn_out, w_o, tm=512):
    bsz, length, _ = x.shape
    wide = lambda blk: pl.BlockSpec((None, tm, D_MODEL), lambda b, i: (b, i, blk))
    half = lambda blk: pl.BlockSpec((None, tm, ATTN_W), lambda b, i: (b, i, blk))
    res = lambda d: pl.BlockSpec((None, d, tm // d, ATTN_W), lambda b, i: (b, 0, i, 0))
    (a0, l0), (a1, l1), (a2, l2) = attn
    d4, d16 = DILATIONS[1], DILATIONS[2]
    a0, l0 = a0.reshape(bsz, length, ATTN_W), l0.reshape(bsz, length, ATTN_W)
    return pl.pallas_call(
        _merge_kernel,
        grid=(bsz, length // tm),
        in_specs=[wide(0), wide(0),
                  wide(NAT_U // 2), wide(NAT_ZS // 2), half(NAT_ZA), wide(NAT_RS // 2), wide(NAT_RA // 2),
                  half(0), res(d4), res(d16), half(0), res(d4), res(d16),
                  _full_spec((1, D_MODEL)), _full_spec((D_MODEL, D_MODEL)), _full_spec((D_MODEL, D_MODEL)),
                  _full_spec((ATTN_W, D_MODEL)), _full_spec((D_MODEL, D_MODEL))],
        out_specs=wide(0),
        out_shape=jax.ShapeDtypeStruct((bsz, length, D_MODEL), F32),
        scratch_shapes=[pltpu.VMEM((ATTN_W // LANES, tm, LANES), F32)],
        compiler_params=pltpu.CompilerParams(dimension_semantics=("arbitrary", "arbitrary"),
                                             vmem_limit_bytes=VMEM_LIMIT),
        name="merge",
    )(x, y, nat, nat, nat, nat, nat, a0, a1, a2, l0, l1, l2,
      d_skip.astype(F32)[None, :], w_glu.astype(BF16), w_ssm_out.astype(BF16),
      w_attn_out.astype(BF16), w_o.astype(BF16))


def kernel(x, positions, norm_w, w_in, lam_re, lam_im, log_dt, b_re, b_im, c_re, c_im, d_skip, w_glu,
           q_norm_w, k_norm_w, w_ssm_out, w_attn_out, w_o):
    bsz, length, _ = x.shape
    nc = length // CHUNK
    xf = x.astype(F32)
    for layer in range(norm_w.shape[0]):
        st, mt, rt, coef = _ssm_prep_call(lam_re[layer], lam_im[layer], log_dt[layer], b_re[layer],
                                          b_im[layer], c_re[layer], c_im[layer])
        nat, qkv4, qkv16 = _proj_call(xf, positions, norm_w[layer], w_in[layer], q_norm_w[layer],
                                      k_norm_w[layer])
        u2 = nat[:, :, :D_MODEL].reshape(bsz, nc, CHUNK, SSM_GROUPS, SSM_GROUP)
        u2 = u2.transpose(3, 1, 0, 2, 4).reshape(SSM_GROUPS, nc, bsz, CHUNK_W)
        y2 = _ssm_call(u2, st, mt, rt, coef)
        y = y2.reshape(SSM_GROUPS, nc, bsz, CHUNK, SSM_GROUP).transpose(2, 1, 3, 0, 4)
        y = y.reshape(bsz, length, D_MODEL)
        attn = [_attn_call(nat[:, None], 0, (NAT_Q, NAT_K, NAT_V)),
                _attn_call(qkv4, 1, (0, 1, 2)),
                _attn_call(qkv16, 2, (0, 1, 2))]
        xf = _merge_call(xf, y, nat, attn, d_skip[layer], w_glu[layer], w_ssm_out[layer],
                         w_attn_out[layer], w_o[layer])
    return xf.astype(x.dtype)
```

```python
import functools
import math

import jax
import jax.numpy as jnp
from jax import lax
from jax.experimental import pallas as pl
from jax.experimental.pallas import tpu as pltpu

F32 = jnp.float32
BF16 = jnp.bfloat16

D_MODEL = 1024
SSM_GROUP = 16
SSM_GROUPS = D_MODEL // SSM_GROUP
SSM_STATE = 64
CHUNK = 16
CHUNK_W = CHUNK * SSM_GROUP
HEAD_DIM = 64
HEADS = 8
ATTN_W = HEADS * HEAD_DIM
DILATIONS = (1, 4, 16)
WINDOW_KEYS = 128
ROPE_DIM = HEAD_DIM // 4
ROPE_THETA = 500000.0
EPS = 1e-6
NEG_INF = -1e30
COL = 512
U_BLK, ZS_BLK, Q_BLK, K_BLK, V_BLK, ZA_BLK, RS_BLK, RA_BLK = 0, 2, 4, 7, 10, 13, 14, 16
N_BLK = 18
IN_WIDTH = N_BLK * COL
NAT_ZS, NAT_Q, NAT_K, NAT_V, NAT_ZA, NAT_RS, NAT_RA = 0, 2, 3, 4, 5, 6, 8
NAT_W = 10 * COL
QKV_W = 3 * COL
LANES = 128
LANE_BLKS = LANES // SSM_GROUP
SSM_CB = 2
SSM_PITCH = 16 * 3
VMEM_LIMIT = 56 * 1024 * 1024


def _full_spec(shape):
    nd = len(shape)
    return pl.BlockSpec(shape, lambda *_: (0,) * nd, pipeline_mode=pl.Buffered(1))


def _ssm_prep_kernel(lam_ref, lrc_ref, lic_ref, ldtc_ref, btr_ref, bti_ref, crt_ref, cit_ref,
                     st_ref, mt_ref, rt_ref, coef_ref):
    lam = lam_ref[0]
    lr, li, dt = lam[0:1, :], lam[1:2, :], jnp.exp(lam[2:3, :])
    mag = jnp.exp(lr * dt)
    ar = mag * jnp.cos(li * dt)
    ai = mag * jnp.sin(li * dt)
    den = lr * lr + li * li
    nr = ar - 1.0
    fr = (nr * lr + ai * li) / den
    fi = (ai * lr - nr * li) / den
    btr, bti = btr_ref[0], bti_ref[0]
    bbr = fr * btr - fi * bti
    bbi = fr * bti + fi * btr
    rows = lax.broadcasted_iota(jnp.int32, (CHUNK_W, 2 * SSM_STATE), 0)
    lanes = lax.broadcasted_iota(jnp.int32, (CHUNK_W, 2 * SSM_STATE), 1)
    im_part = lanes >= SSM_STATE
    spow = (CHUNK - 1 - rows // SSM_GROUP).astype(F32)
    pmag = jnp.exp(lr * dt * spow)
    apr = pmag * jnp.cos(li * dt * spow)
    api = pmag * jnp.sin(li * dt * spow)
    wre = apr * bbr - api * bbi
    wim = apr * bbi + api * bbr
    st = jnp.where(im_part, wim, wre)
    st_sw = jnp.where(im_part, wre, wim)
    st_ref[0] = jnp.concatenate([st, st_sw], axis=1).astype(BF16)

    lrd = lrc_ref[0] * jnp.exp(ldtc_ref[0])
    lid = lic_ref[0] * jnp.exp(ldtc_ref[0])
    tl = lax.broadcasted_iota(jnp.int32, (SSM_STATE, CHUNK_W), 1)
    tpow = (tl // SSM_GROUP + 1).astype(F32)
    tmag = jnp.exp(lrd * tpow)
    arw = tmag * jnp.cos(lid * tpow)
    aiw = tmag * jnp.sin(lid * tpow)
    crt, cit = crt_ref[0], cit_ref[0]
    rt_re = crt * arw - cit * aiw
    rt_im = -(crt * aiw + cit * arw)
    rt_ref[0] = jnp.concatenate([rt_re, rt_im], axis=0).astype(BF16)

    ccw = jnp.concatenate([crt, -cit], axis=0)
    krw = jnp.dot(st, ccw, preferred_element_type=F32, precision=lax.Precision.HIGHEST)
    lane_blk = lax.broadcasted_iota(jnp.int32, (CHUNK_W, CHUNK_W), 1) // SSM_GROUP
    mt = jnp.zeros((CHUNK_W, CHUNK_W), F32)
    for t in range(CHUNK):
        sh = SSM_GROUP * (CHUNK - 1 - t)
        if sh == 0:
            shifted = krw
        else:
            shifted = jnp.concatenate([krw[sh:, :], jnp.zeros((sh, CHUNK_W), F32)], axis=0)
        mt = jnp.where(lane_blk == t, shifted, mt)
    mt_ref[0] = mt.astype(BF16)

    cmag = jnp.exp(lr * dt * float(CHUNK))
    a16 = cmag * jnp.cos(li * dt * float(CHUNK))
    ai16 = cmag * jnp.sin(li * dt * float(CHUNK))
    im_row = lax.broadcasted_iota(jnp.int32, (1, 2 * SSM_STATE), 1) >= SSM_STATE
    bco = jnp.where(im_row, ai16, -ai16)
    coef_ref[0] = jnp.concatenate([a16, bco, jnp.zeros((6, 2 * SSM_STATE), F32)], axis=0)


def _ssm_prep_call(lam_re, lam_im, log_dt, b_re, b_im, c_re, c_im):
    g, n, p = SSM_GROUPS, SSM_STATE, SSM_GROUP
    f = lambda a: a.astype(F32)
    row = lambda a: jnp.tile(f(a), (1, 2))[:, None, :]
    ldt_b = jnp.broadcast_to(f(log_dt)[:, None], (g, n))
    lam = jnp.concatenate([row(lam_re), row(lam_im), row(ldt_b), jnp.zeros((g, 5, 2 * n), F32)], axis=1)
    col = lambda a: f(a)[:, :, None]
    bt = lambda b: jnp.tile(f(b).transpose(0, 2, 1), (1, CHUNK, 2))
    ct = lambda c: jnp.tile(f(c).transpose(0, 2, 1), (1, 1, CHUNK))
    gspec = lambda shape: pl.BlockSpec((1,) + shape, lambda i: (i, 0, 0))
    return pl.pallas_call(
        _ssm_prep_kernel,
        grid=(g,),
        in_specs=[gspec((8, 2 * n)), gspec((n, 1)), gspec((n, 1)), gspec((n, 1)),
                  gspec((CHUNK_W, 2 * n)), gspec((CHUNK_W, 2 * n)),
                  gspec((n, CHUNK_W)), gspec((n, CHUNK_W))],
        out_specs=[gspec((CHUNK_W, 4 * n)), gspec((CHUNK_W, CHUNK_W)), gspec((2 * n, CHUNK_W)),
                   gspec((8, 2 * n))],
        out_shape=[jax.ShapeDtypeStruct((g, CHUNK_W, 4 * n), BF16),
                   jax.ShapeDtypeStruct((g, CHUNK_W, CHUNK_W), BF16),
                   jax.ShapeDtypeStruct((g, 2 * n, CHUNK_W), BF16),
                   jax.ShapeDtypeStruct((g, 8, 2 * n), F32)],
        compiler_params=pltpu.CompilerParams(dimension_semantics=("arbitrary",)),
        name="ssm_prep",
    )(lam, col(lam_re), col(lam_im), col(ldt_b), bt(b_re), bt(b_im), ct(c_re), ct(c_im))


def _rope_tables():
    lane = jnp.arange(128) % HEAD_DIM
    inv = ROPE_THETA ** (-jnp.arange(0, ROPE_DIM, 2, dtype=F32) / ROPE_DIM)
    half = ROPE_DIM // 2
    freq = jnp.where(lane < ROPE_DIM, inv[lane % half], 0.0)
    s_lo = jnp.where(lane < half, -1.0, 0.0)
    s_hi = jnp.where((lane >= half) & (lane < ROPE_DIM), 1.0, 0.0)
    return jnp.concatenate([freq[None], s_lo[None], s_hi[None], jnp.zeros((5, 128), F32)], axis=0).astype(F32)


_NAT_DST = {2: NAT_ZS, 3: NAT_ZS + 1, Q_BLK: NAT_Q, K_BLK: NAT_K, V_BLK: NAT_V,
            ZA_BLK: NAT_ZA, 14: NAT_RS, 15: NAT_RS + 1, 16: NAT_RA, 17: NAT_RA + 1}


def _proj_kernel(x_ref, pos_ref, nw_ref, w_ref, qkw_ref, ones_ref, rope_ref, nat_ref, d4_ref, d16_ref, perm_ref):
    tm = x_ref.shape[0]
    x = x_ref[...]
    ms = jnp.mean(x * x, axis=-1, keepdims=True)
    h = (x * lax.rsqrt(ms + EPS) * nw_ref[...]).astype(BF16)
    ang = pos_ref[...].astype(F32) * rope_ref[0:1, :]
    cosv = jnp.cos(ang)
    sinv = jnp.sin(ang)
    s_lo = sinv * rope_ref[1:2, :]
    s_hi = sinv * rope_ref[2:3, :]
    for j in range(ZS_BLK, N_BLK):
        acc = jnp.dot(h, w_ref[:, j * COL:(j + 1) * COL], preferred_element_type=F32)
        if ZS_BLK <= j < Q_BLK or j == ZA_BLK:
            res = acc * jax.nn.sigmoid(acc)
        elif j >= RS_BLK:
            res = jax.nn.sigmoid(acc)
        elif Q_BLK <= j < V_BLK:
            ss = jnp.dot((acc * acc).astype(BF16), ones_ref[...], preferred_element_type=F32)
            y = acc * lax.rsqrt(ss * (1.0 / HEAD_DIM) + EPS) * qkw_ref[j - Q_BLK:j - Q_BLK + 1, :]
            parts = []
            for c in range(COL // LANES):
                yc = y[:, c * LANES:(c + 1) * LANES]
                parts.append(yc * cosv + pltpu.roll(yc, LANES - ROPE_DIM // 2, 1) * s_lo
                             + pltpu.roll(yc, ROPE_DIM // 2, 1) * s_hi)
            res = jnp.concatenate(parts, axis=1)
        else:
            res = acc
        if j in _NAT_DST:
            dst = _NAT_DST[j]
            nat_ref[:, dst * COL:(dst + 1) * COL] = res.astype(BF16)
        else:
            kind, group = divmod(j - Q_BLK, len(DILATIONS))
            d = DILATIONS[group]
            out_ref = d4_ref if group == 1 else d16_ref
            for c in range(COL // LANES):
                perm_ref[c] = res[:, c * LANES:(c + 1) * LANES]
            for r in range(d):
                rows = [perm_ref[c, pl.ds(r, tm // d, stride=d), :] for c in range(COL // LANES)]
                out_ref[r, :, kind * COL:(kind + 1) * COL] = jnp.concatenate(rows, axis=1).astype(BF16)


def _proj_call(x, positions, norm_w, w_in, q_norm_w, k_norm_w, tm=256):
    bsz, length, _ = x.shape
    scale = 1.0 / math.sqrt(HEAD_DIM)
    qkw = jnp.concatenate([jnp.tile(q_norm_w.astype(F32) * scale, (1, HEADS)),
                           jnp.tile(k_norm_w.astype(F32), (1, HEADS)),
                           jnp.zeros((2, ATTN_W), F32)], axis=0)
    hid = jnp.arange(ATTN_W) // HEAD_DIM
    ones = (hid[:, None] == hid[None, :]).astype(BF16)
    d4, d16 = DILATIONS[1], DILATIONS[2]
    return pl.pallas_call(
        _proj_kernel,
        grid=(bsz, length // tm),
        in_specs=[pl.BlockSpec((None, tm, D_MODEL), lambda b, i: (b, i, 0)),
                  pl.BlockSpec((None, tm, 1), lambda b, i: (b, i, 0)),
                  _full_spec((1, D_MODEL)),
                  _full_spec((D_MODEL, IN_WIDTH)),
                  _full_spec((8, ATTN_W)),
                  _full_spec((ATTN_W, ATTN_W)),
                  _full_spec((8, LANES))],
        out_specs=[pl.BlockSpec((None, tm, NAT_W), lambda b, i: (b, i, 0)),
                   pl.BlockSpec((None, d4, tm // d4, QKV_W), lambda b, i: (b, 0, i, 0)),
                   pl.BlockSpec((None, d16, tm // d16, QKV_W), lambda b, i: (b, 0, i, 0))],
        out_shape=[jax.ShapeDtypeStruct((bsz, length, NAT_W), BF16),
                   jax.ShapeDtypeStruct((bsz, d4, length // d4, QKV_W), BF16),
                   jax.ShapeDtypeStruct((bsz, d16, length // d16, QKV_W), BF16)],
        scratch_shapes=[pltpu.VMEM((COL // LANES, tm, LANES), F32)],
        compiler_params=pltpu.CompilerParams(dimension_semantics=("arbitrary", "arbitrary"),
                                             vmem_limit_bytes=VMEM_LIMIT),
        name="proj",
    )(x.astype(F32), positions.astype(jnp.int32)[:, :, None], norm_w.astype(F32)[None, :],
      w_in.astype(BF16), qkw, ones, _rope_tables())


def _block_transpose(xs):
    xs = list(xs)
    blk = lax.broadcasted_iota(jnp.int32, xs[0].shape, 1) // SSM_GROUP
    dist = 1
    while dist < LANE_BLKS:
        upper = (blk & dist) != 0
        shift = dist * SSM_GROUP
        for i in range(LANE_BLKS):
            if i & dist:
                continue
            a, b = xs[i], xs[i + dist]
            xs[i] = jnp.where(upper, pltpu.roll(b, shift, 1), a)
            xs[i + dist] = jnp.where(upper, b, pltpu.roll(a, LANES - shift, 1))
        dist *= 2
    return xs


def _ssm_in_kernel(x_ref, nw_ref, w_ref, unat_ref, u2_ref, rows_ref):
    bsz, tl = x_ref.shape[0], x_ref.shape[1]
    x = x_ref[...].reshape(bsz * tl, D_MODEL)
    ms = jnp.mean(x * x, axis=-1, keepdims=True)
    h = (x * lax.rsqrt(ms + EPS) * nw_ref[...]).astype(BF16)
    u = jnp.dot(h, w_ref[...], preferred_element_type=F32)
    unat_ref[...] = u.reshape(bsz, tl, D_MODEL).astype(BF16)
    for b in range(bsz):
        for s8 in range(D_MODEL // LANES):
            rows_ref[s8, b * SSM_PITCH:b * SSM_PITCH + tl, :] = u[b * tl:(b + 1) * tl, s8 * LANES:(s8 + 1) * LANES]
    for c in range(tl // CHUNK):
        for s8 in range(D_MODEL // LANES):
            for half in range(CHUNK // LANE_BLKS):
                xs = []
                for k in range(LANE_BLKS):
                    tok = c * CHUNK + half * LANE_BLKS + k
                    v = rows_ref[s8, pl.ds(tok, bsz, stride=SSM_PITCH), :].astype(BF16)
                    xs.append(pltpu.bitcast(v, jnp.uint32))
                ys = _block_transpose(xs)
                for g in range(LANE_BLKS):
                    u2_ref[s8 * LANE_BLKS + g, c, :, half * LANES:(half + 1) * LANES] = pltpu.bitcast(ys[g], BF16)


def _ssm_in_call(x, norm_w, w_u):
    bsz, length, _ = x.shape
    tl = SSM_CB * CHUNK
    nc = length // CHUNK
    return pl.pallas_call(
        _ssm_in_kernel,
        grid=(length // tl,),
        in_specs=[pl.BlockSpec((bsz, tl, D_MODEL), lambda i: (0, i, 0)),
                  _full_spec((1, D_MODEL)), _full_spec((D_MODEL, D_MODEL))],
        out_specs=[pl.BlockSpec((bsz, tl, D_MODEL), lambda i: (0, i, 0)),
                   pl.BlockSpec((SSM_GROUPS, SSM_CB, bsz, CHUNK_W), lambda i: (0, i, 0, 0))],
        out_shape=[jax.ShapeDtypeStruct((bsz, length, D_MODEL), BF16),
                   jax.ShapeDtypeStruct((SSM_GROUPS, nc, bsz, CHUNK_W), BF16)],
        scratch_shapes=[pltpu.VMEM((D_MODEL // LANES, bsz * SSM_PITCH, LANES), F32)],
        compiler_params=pltpu.CompilerParams(dimension_semantics=("arbitrary",),
                                             vmem_limit_bytes=VMEM_LIMIT),
        name="ssm_in",
    )(x, norm_w.astype(F32)[None, :], w_u.astype(BF16))


def _ssm_out_kernel(y2_ref, u_ref, zs_ref, dskip_ref, wglu_ref, wso_ref, o_ref, rows_ref):
    bsz, tl = u_ref.shape[0], u_ref.shape[1]
    for c in range(tl // CHUNK):
        for s8 in range(D_MODEL // LANES):
            for half in range(CHUNK // LANE_BLKS):
                xs = [pltpu.bitcast(y2_ref[s8 * LANE_BLKS + g, c, :, half * LANES:(half + 1) * LANES], jnp.uint32)
                      for g in range(LANE_BLKS)]
                ys = _block_transpose(xs)
                for k in range(LANE_BLKS):
                    tok = c * CHUNK + half * LANE_BLKS + k
                    rows_ref[s8, pl.ds(tok, bsz, stride=SSM_PITCH), :] = pltpu.bitcast(ys[k], BF16).astype(F32)
    y = jnp.concatenate(
        [jnp.concatenate([rows_ref[s8, b * SSM_PITCH:b * SSM_PITCH + tl, :] for s8 in range(D_MODEL // LANES)],
                         axis=1) for b in range(bsz)], axis=0)
    u = u_ref[...].reshape(bsz * tl, D_MODEL).astype(F32)
    zs = zs_ref[...].reshape(bsz * tl, D_MODEL).astype(F32)
    y = y + dskip_ref[...] * u
    g = jax.nn.gelu(y, approximate=True)
    gate = jax.nn.sigmoid(jnp.dot(g.astype(BF16), wglu_ref[...], preferred_element_type=F32))
    ys_in = (g * gate * zs).astype(BF16)
    ys = jnp.dot(ys_in, wso_ref[...], preferred_element_type=F32)
    o_ref[...] = ys.reshape(bsz, tl, D_MODEL).astype(BF16)


def _ssm_out_call(y2, unat, nat, d_skip, w_glu, w_ssm_out):
    bsz, length, _ = unat.shape
    tl = SSM_CB * CHUNK
    tok = lambda blk: pl.BlockSpec((bsz, tl, D_MODEL), lambda i: (0, i, blk))
    return pl.pallas_call(
        _ssm_out_kernel,
        grid=(length // tl,),
        in_specs=[pl.BlockSpec((SSM_GROUPS, SSM_CB, bsz, CHUNK_W), lambda i: (0, i, 0, 0)),
                  tok(0), tok(NAT_ZS // 2),
                  _full_spec((1, D_MODEL)), _full_spec((D_MODEL, D_MODEL)), _full_spec((D_MODEL, D_MODEL))],
        out_specs=tok(0),
        out_shape=jax.ShapeDtypeStruct((bsz, length, D_MODEL), BF16),
        scratch_shapes=[pltpu.VMEM((D_MODEL // LANES, bsz * SSM_PITCH, LANES), F32)],
        compiler_params=pltpu.CompilerParams(dimension_semantics=("arbitrary",),
                                             vmem_limit_bytes=VMEM_LIMIT),
        name="ssm_out",
    )(y2, unat, nat, d_skip.astype(F32)[None, :], w_glu.astype(BF16), w_ssm_out.astype(BF16))


def _ssm_kernel(u_ref, st_ref, mt_ref, rt_ref, coef_ref, y_ref, hloc_ref, hprev_ref):
    nc, bsz = u_ref.shape[1], u_ref.shape[2]
    ns2 = 2 * SSM_STATE
    u = u_ref[0].reshape(nc * bsz, CHUNK_W)
    hloc_ref[...] = jnp.dot(u, st_ref[0], preferred_element_type=F32)
    a = jnp.broadcast_to(coef_ref[0, 0:1, :], (bsz, ns2))
    b = jnp.broadcast_to(coef_ref[0, 1:2, :], (bsz, ns2))

    def step(c, carry):
        hx, hy = carry
        r = pl.multiple_of(c * bsz, bsz)
        hprev_ref[pl.ds(r, bsz), :] = hx.astype(BF16)
        px = hloc_ref[pl.ds(r, bsz), 0:ns2]
        py = hloc_ref[pl.ds(r, bsz), ns2:2 * ns2]
        return a * hx + b * hy + px, a * hy - b * hx + py

    zero = jnp.zeros((bsz, ns2), F32)
    lax.fori_loop(0, nc, step, (zero, zero), unroll=8)
    y = jnp.dot(u, mt_ref[0], preferred_element_type=F32)
    y = y + jnp.dot(hprev_ref[...], rt_ref[0], preferred_element_type=F32)
    y_ref[0] = y.reshape(nc, bsz, CHUNK_W).astype(BF16)


def _ssm_call(u2, st, mt, rt, coef):
    g, nc, bsz, _ = u2.shape
    gspec = lambda shape: pl.BlockSpec((1,) + shape, lambda i: (i,) + (0,) * len(shape))
    return pl.pallas_call(
        _ssm_kernel,
        grid=(g,),
        in_specs=[gspec((nc, bsz, CHUNK_W)), gspec((CHUNK_W, 4 * SSM_STATE)), gspec((CHUNK_W, CHUNK_W)),
                  gspec((2 * SSM_STATE, CHUNK_W)), gspec((8, 2 * SSM_STATE))],
        out_specs=gspec((nc, bsz, CHUNK_W)),
        out_shape=jax.ShapeDtypeStruct(u2.shape, BF16),
        scratch_shapes=[pltpu.VMEM((nc * bsz, 4 * SSM_STATE), F32),
                        pltpu.VMEM((nc * bsz, 2 * SSM_STATE), BF16)],
        compiler_params=pltpu.CompilerParams(dimension_semantics=("arbitrary",),
                                             vmem_limit_bytes=VMEM_LIMIT),
        name="ssm_scan",
    )(u2, st, mt, rt, coef)


def _attn_kernel(q_ref, kc_ref, kp_ref, vc_ref, vp_ref, o_ref, lse_ref, kbuf, vbuf):
    tq = q_ref.shape[0]
    qb = 128
    slab = pl.program_id(2)
    kbuf[0:qb, :] = kp_ref[...]
    kbuf[qb:, :] = kc_ref[...]
    vbuf[0:qb, :] = vp_ref[...]
    vbuf[qb:, :] = vc_ref[...]
    qi = lax.broadcasted_iota(jnp.int32, (qb, 2 * qb), 0)
    kk = lax.broadcasted_iota(jnp.int32, (qb, 2 * qb), 1)
    band = jnp.where((kk >= qi) & (kk <= qi + WINDOW_KEYS), 0.0, NEG_INF).astype(F32)
    before_start = jnp.where(kk < qb, NEG_INF, 0.0).astype(F32)
    lo_q = lax.broadcasted_iota(jnp.int32, (qb, 128), 1) < HEAD_DIM
    lo_kv = lax.broadcasted_iota(jnp.int32, (2 * qb, 128), 1) < HEAD_DIM

    def block(j, carry):
        r0 = pl.multiple_of(j * qb, qb)
        first = jnp.logical_and(slab == 0, j == 0)
        madd = band + jnp.where(first, before_start, 0.0)
        for hp in range(ATTN_W // 128):
            cs = slice(hp * 128, (hp + 1) * 128)
            qp = q_ref[pl.ds(r0, qb), cs]
            kp = kbuf[pl.ds(r0, 2 * qb), cs]
            vp = vbuf[pl.ds(r0, 2 * qb), cs]
            acc = jnp.zeros((qb, 128), F32)
            lse = jnp.zeros((qb, 128), F32)
            for side in range(2):
                sel_q = lo_q if side == 0 else jnp.logical_not(lo_q)
                sel_kv = lo_kv if side == 0 else jnp.logical_not(lo_kv)
                qm = jnp.where(sel_q, qp, jnp.zeros_like(qp))
                s = lax.dot_general(qm, kp, (((1,), (1,)), ((), ())), preferred_element_type=F32)
                s = s + madd
                m = jnp.max(s, axis=1, keepdims=True)
                p = jnp.exp(s - m)
                l = jnp.sum(p, axis=1, keepdims=True)
                vm = jnp.where(sel_kv, vp, jnp.zeros_like(vp))
                pv = jnp.dot(p.astype(BF16), vm, preferred_element_type=F32)
                acc = acc + pv * (1.0 / l)
                lse = jnp.where(sel_q, m + jnp.log(l), lse)
            o_ref[pl.ds(r0, qb), cs] = acc.astype(BF16)
            lse_ref[pl.ds(r0, qb), cs] = lse
        return carry

    lax.fori_loop(0, tq // qb, block, 0)


def _attn_call(qkv, group, blocks):
    bsz, d, ld, _ = qkv.shape
    tq = min(512, ld)
    nslab = ld // tq
    per = tq // 128
    cur = lambda blk: pl.BlockSpec((None, None, tq, COL), lambda b, r, i: (b, r, i, blk))
    prev = lambda blk: pl.BlockSpec(
        (None, None, 128, COL), lambda b, r, i: (b, r, jnp.maximum(i * per - 1, 0), blk))
    ospec = pl.BlockSpec((None, None, tq, COL), lambda b, r, i: (b, r, i, 0))
    qb, kb, vb = blocks
    return pl.pallas_call(
        _attn_kernel,
        grid=(bsz, d, nslab),
        in_specs=[cur(qb), cur(kb), prev(kb), cur(vb), prev(vb)],
        out_specs=[ospec, ospec],
        out_shape=[jax.ShapeDtypeStruct((bsz, d, ld, ATTN_W), BF16),
                   jax.ShapeDtypeStruct((bsz, d, ld, ATTN_W), F32)],
        scratch_shapes=[pltpu.VMEM((tq + 128, COL), BF16), pltpu.VMEM((tq + 128, COL), BF16)],
        compiler_params=pltpu.CompilerParams(dimension_semantics=("arbitrary",) * 3,
                                             vmem_limit_bytes=VMEM_LIMIT),
        name=f"attn_d{DILATIONS[group]}",
    )(qkv, qkv, qkv, qkv, qkv)


def _merge_kernel(x_ref, ys_ref, za_ref, rs_ref, ra_ref,
                  a0_ref, a1_ref, a2_ref, l0_ref, l1_ref, l2_ref,
                  wao_ref, wo_ref, o_ref, il_ref):
    tm = x_ref.shape[0]

    def natural_order(ref):
        d = ref.shape[0]
        for r in range(d):
            blk = ref[r].astype(F32)
            for c in range(ATTN_W // LANES):
                il_ref[c, pl.ds(r, tm // d, stride=d), :] = blk[:, c * LANES:(c + 1) * LANES]
        return jnp.concatenate([il_ref[c] for c in range(ATTN_W // LANES)], axis=1)

    l0, l1, l2 = l0_ref[...], natural_order(l1_ref), natural_order(l2_ref)
    lm = jnp.maximum(jnp.maximum(l0, l1), l2)
    e0, e1, e2 = jnp.exp(l0 - lm), jnp.exp(l1 - lm), jnp.exp(l2 - lm)
    att = (e0 * a0_ref[...].astype(F32) + e1 * natural_order(a1_ref)
           + e2 * natural_order(a2_ref)) / (e0 + e1 + e2)
    ya_in = (att * za_ref[...].astype(F32)).astype(BF16)
    ya = jnp.dot(ya_in, wao_ref[...], preferred_element_type=F32)

    m = rs_ref[...].astype(F32) * ys_ref[...].astype(F32) + ra_ref[...].astype(F32) * ya
    o_ref[...] = x_ref[...] + jnp.dot(m.astype(BF16), wo_ref[...], preferred_element_type=F32)


def _merge_call(x, ys, nat, attn, w_attn_out, w_o, tm=512):
    bsz, length, _ = x.shape
    wide = lambda blk: pl.BlockSpec((None, tm, D_MODEL), lambda b, i: (b, i, blk))
    half = lambda blk: pl.BlockSpec((None, tm, ATTN_W), lambda b, i: (b, i, blk))
    res = lambda d: pl.BlockSpec((None, d, tm // d, ATTN_W), lambda b, i: (b, 0, i, 0))
    (a0, l0), (a1, l1), (a2, l2) = attn
    d4, d16 = DILATIONS[1], DILATIONS[2]
    a0, l0 = a0.reshape(bsz, length, ATTN_W), l0.reshape(bsz, length, ATTN_W)
    return pl.pallas_call(
        _merge_kernel,
        grid=(bsz, length // tm),
        in_specs=[wide(0), wide(0), half(NAT_ZA), wide(NAT_RS // 2), wide(NAT_RA // 2),
                  half(0), res(d4), res(d16), half(0), res(d4), res(d16),
                  _full_spec((ATTN_W, D_MODEL)), _full_spec((D_MODEL, D_MODEL))],
        out_specs=wide(0),
        out_shape=jax.ShapeDtypeStruct((bsz, length, D_MODEL), F32),
        scratch_shapes=[pltpu.VMEM((ATTN_W // LANES, tm, LANES), F32)],
        compiler_params=pltpu.CompilerParams(dimension_semantics=("arbitrary", "arbitrary"),
                                             vmem_limit_bytes=VMEM_LIMIT),
        name="merge",
    )(x, ys, nat, nat, nat, a0, a1, a2, l0, l1, l2, w_attn_out.astype(BF16), w_o.astype(BF16))


def kernel(x, positions, norm_w, w_in, lam_re, lam_im, log_dt, b_re, b_im, c_re, c_im, d_skip, w_glu,
           q_norm_w, k_norm_w, w_ssm_out, w_attn_out, w_o):
    xf = x.astype(F32)
    for layer in range(norm_w.shape[0]):
        st, mt, rt, coef = _ssm_prep_call(lam_re[layer], lam_im[layer], log_dt[layer], b_re[layer],
                                          b_im[layer], c_re[layer], c_im[layer])
        nat, qkv4, qkv16 = _proj_call(xf, positions, norm_w[layer], w_in[layer], q_norm_w[layer],
                                      k_norm_w[layer])
        unat, u2 = _ssm_in_call(xf, norm_w[layer], w_in[layer][:, :D_MODEL])
        y2 = _ssm_call(u2, st, mt, rt, coef)
        ys = _ssm_out_call(y2, unat, nat, d_skip[layer], w_glu[layer], w_ssm_out[layer])
        attn = [_attn_call(nat[:, None], 0, (NAT_Q, NAT_K, NAT_V)),
                _attn_call(qkv4, 1, (0, 1, 2)),
                _attn_call(qkv16, 2, (0, 1, 2))]
        xf = _merge_call(xf, ys, nat, attn, w_attn_out[layer], w_o[layer])
    return xf.astype(x.dtype)
```

```python
import functools
import math

import jax
import jax.numpy as jnp
from jax import lax
from jax.experimental import pallas as pl
from jax.experimental.pallas import tpu as pltpu

F32 = jnp.float32
BF16 = jnp.bfloat16

D_MODEL = 1024
SSM_GROUP = 16
SSM_GROUPS = D_MODEL // SSM_GROUP
SSM_STATE = 64
CHUNK = 16
CHUNK_W = CHUNK * SSM_GROUP
HEAD_DIM = 64
HEADS = 8
ATTN_W = HEADS * HEAD_DIM
DILATIONS = (1, 4, 16)
WINDOW_KEYS = 128
ROPE_DIM = HEAD_DIM // 4
ROPE_THETA = 500000.0
EPS = 1e-6
NEG_INF = -1e30
LOG2E = math.log2(math.e)
COL = 512
U_BLK, ZS_BLK, Q_BLK, K_BLK, V_BLK, ZA_BLK, RS_BLK, RA_BLK = 0, 2, 4, 7, 10, 13, 14, 16
N_BLK = 18
IN_WIDTH = N_BLK * COL
NAT_ZS, NAT_Q, NAT_K, NAT_V, NAT_ZA, NAT_RS, NAT_RA = 0, 2, 3, 4, 5, 6, 8
NAT_W = 10 * COL
QKV_W = 3 * COL
LANES = 128
MXU_DIM = 256
LANE_BLKS = LANES // SSM_GROUP
SSM_CB = 2
SSM_PITCH = 16 * 3
VMEM_LIMIT = 56 * 1024 * 1024


def _full_spec(shape):
    nd = len(shape)
    return pl.BlockSpec(shape, lambda *_: (0,) * nd, pipeline_mode=pl.Buffered(1))


def _ssm_prep_kernel(lam_ref, lrc_ref, lic_ref, ldtc_ref, btr_ref, bti_ref, crt_ref, cit_ref,
                     st_ref, mt_ref, rt_ref, coef_ref):
    lam = lam_ref[0]
    lr, li, dt = lam[0:1, :], lam[1:2, :], jnp.exp(lam[2:3, :])
    mag = jnp.exp(lr * dt)
    ar = mag * jnp.cos(li * dt)
    ai = mag * jnp.sin(li * dt)
    den = lr * lr + li * li
    nr = ar - 1.0
    fr = (nr * lr + ai * li) / den
    fi = (ai * lr - nr * li) / den
    btr, bti = btr_ref[0], bti_ref[0]
    bbr = fr * btr - fi * bti
    bbi = fr * bti + fi * btr
    rows = lax.broadcasted_iota(jnp.int32, (CHUNK_W, 2 * SSM_STATE), 0)
    lanes = lax.broadcasted_iota(jnp.int32, (CHUNK_W, 2 * SSM_STATE), 1)
    im_part = lanes >= SSM_STATE
    spow = (CHUNK - 1 - rows // SSM_GROUP).astype(F32)
    pmag = jnp.exp(lr * dt * spow)
    apr = pmag * jnp.cos(li * dt * spow)
    api = pmag * jnp.sin(li * dt * spow)
    wre = apr * bbr - api * bbi
    wim = apr * bbi + api * bbr
    st = jnp.where(im_part, wim, wre)
    st_sw = jnp.where(im_part, wre, wim)
    st_ref[0] = jnp.concatenate([st, st_sw], axis=1).astype(BF16)

    lrd = lrc_ref[0] * jnp.exp(ldtc_ref[0])
    lid = lic_ref[0] * jnp.exp(ldtc_ref[0])
    tl = lax.broadcasted_iota(jnp.int32, (SSM_STATE, CHUNK_W), 1)
    tpow = (tl // SSM_GROUP + 1).astype(F32)
    tmag = jnp.exp(lrd * tpow)
    arw = tmag * jnp.cos(lid * tpow)
    aiw = tmag * jnp.sin(lid * tpow)
    crt, cit = crt_ref[0], cit_ref[0]
    rt_re = crt * arw - cit * aiw
    rt_im = -(crt * aiw + cit * arw)
    rt_ref[0] = jnp.concatenate([rt_re, rt_im], axis=0).astype(BF16)

    ccw = jnp.concatenate([crt, -cit], axis=0)
    krw = jnp.dot(st, ccw, preferred_element_type=F32, precision=lax.Precision.HIGHEST)
    lane_blk = lax.broadcasted_iota(jnp.int32, (CHUNK_W, CHUNK_W), 1) // SSM_GROUP
    mt = jnp.zeros((CHUNK_W, CHUNK_W), F32)
    for t in range(CHUNK):
        sh = SSM_GROUP * (CHUNK - 1 - t)
        if sh == 0:
            shifted = krw
        else:
            shifted = jnp.concatenate([krw[sh:, :], jnp.zeros((sh, CHUNK_W), F32)], axis=0)
        mt = jnp.where(lane_blk == t, shifted, mt)
    mt_ref[0] = mt.astype(BF16)

    cmag = jnp.exp(lr * dt * float(CHUNK))
    a16 = cmag * jnp.cos(li * dt * float(CHUNK))
    ai16 = cmag * jnp.sin(li * dt * float(CHUNK))
    im_row = lax.broadcasted_iota(jnp.int32, (1, 2 * SSM_STATE), 1) >= SSM_STATE
    bco = jnp.where(im_row, ai16, -ai16)
    coef_ref[0] = jnp.concatenate([a16, bco, jnp.zeros((6, 2 * SSM_STATE), F32)], axis=0)


def _ssm_prep_call(lam_re, lam_im, log_dt, b_re, b_im, c_re, c_im):
    g, n, p = SSM_GROUPS, SSM_STATE, SSM_GROUP
    f = lambda a: a.astype(F32)
    row = lambda a: jnp.tile(f(a), (1, 2))[:, None, :]
    ldt_b = jnp.broadcast_to(f(log_dt)[:, None], (g, n))
    lam = jnp.concatenate([row(lam_re), row(lam_im), row(ldt_b), jnp.zeros((g, 5, 2 * n), F32)], axis=1)
    col = lambda a: f(a)[:, :, None]
    bt = lambda b: jnp.tile(f(b).transpose(0, 2, 1), (1, CHUNK, 2))
    ct = lambda c: jnp.tile(f(c).transpose(0, 2, 1), (1, 1, CHUNK))
    gspec = lambda shape: pl.BlockSpec((1,) + shape, lambda i: (i, 0, 0))
    return pl.pallas_call(
        _ssm_prep_kernel,
        grid=(g,),
        in_specs=[gspec((8, 2 * n)), gspec((n, 1)), gspec((n, 1)), gspec((n, 1)),
                  gspec((CHUNK_W, 2 * n)), gspec((CHUNK_W, 2 * n)),
                  gspec((n, CHUNK_W)), gspec((n, CHUNK_W))],
        out_specs=[gspec((CHUNK_W, 4 * n)), gspec((CHUNK_W, CHUNK_W)), gspec((2 * n, CHUNK_W)),
                   gspec((8, 2 * n))],
        out_shape=[jax.ShapeDtypeStruct((g, CHUNK_W, 4 * n), BF16),
                   jax.ShapeDtypeStruct((g, CHUNK_W, CHUNK_W), BF16),
                   jax.ShapeDtypeStruct((g, 2 * n, CHUNK_W), BF16),
                   jax.ShapeDtypeStruct((g, 8, 2 * n), F32)],
        compiler_params=pltpu.CompilerParams(dimension_semantics=("arbitrary",)),
        name="ssm_prep",
    )(lam, col(lam_re), col(lam_im), col(ldt_b), bt(b_re), bt(b_im), ct(c_re), ct(c_im))


def _rope_tables():
    lane = jnp.arange(LANES) % HEAD_DIM
    half = ROPE_DIM // 2
    inv = ROPE_THETA ** (-jnp.arange(0, ROPE_DIM, 2, dtype=F32) / ROPE_DIM)
    expand = ((lane[None, :] < ROPE_DIM) & (lane[None, :] % half == jnp.arange(half)[:, None])).astype(F32)
    s_lo = jnp.where(lane < half, -1.0, 0.0)
    s_hi = jnp.where((lane >= half) & (lane < ROPE_DIM), 1.0, 0.0)
    unrot = jnp.where(lane >= ROPE_DIM, 1.0, 0.0)
    rows = jnp.concatenate([s_lo[None], s_hi[None], unrot[None], jnp.zeros((5, LANES), F32)], axis=0)
    return inv[:, None].astype(F32), expand, rows.astype(F32)


def _sigmoid(v):
    return 1.0 / (1.0 + jnp.exp(-v))


_NAT_DST = {2: NAT_ZS, 3: NAT_ZS + 1, Q_BLK: NAT_Q, K_BLK: NAT_K, V_BLK: NAT_V,
            ZA_BLK: NAT_ZA, 14: NAT_RS, 15: NAT_RS + 1, 16: NAT_RA, 17: NAT_RA + 1}


def _proj_kernel(x_ref, pos_ref, nw_ref, w_ref, qkw_ref, ones_ref, freq_ref, expand_ref, rope_ref,
                 nat_ref, d4_ref, d16_ref, perm_ref, h_ref):
    tm = x_ref.shape[0]
    x = x_ref[...]
    ms = jnp.mean(x * x, axis=-1, keepdims=True)
    h_ref[...] = (x * lax.rsqrt(ms + EPS) * nw_ref[...]).astype(BF16)
    ang = freq_ref[...] * pos_ref[...].astype(F32)

    def spread(v):
        hi = v.astype(BF16)
        lo = (v - hi.astype(F32)).astype(BF16)
        dims = (((0,), (0,)), ((), ()))
        e = expand_ref[...]
        return (lax.dot_general(hi, e, dims, preferred_element_type=F32)
                + lax.dot_general(lo, e, dims, preferred_element_type=F32))

    cosv = spread(jnp.cos(ang)) + rope_ref[2:3, :]
    sinv = spread(jnp.sin(ang))
    s_lo = sinv * rope_ref[0:1, :]
    s_hi = sinv * rope_ref[1:2, :]
    for j in range(ZS_BLK, N_BLK):
        acc = jnp.dot(h_ref[...], w_ref[:, j * COL:(j + 1) * COL], preferred_element_type=F32)
        if ZS_BLK <= j < Q_BLK or j == ZA_BLK:
            res = acc * _sigmoid(acc)
        elif j >= RS_BLK:
            res = _sigmoid(acc)
        elif Q_BLK <= j < V_BLK:
            sq = (acc * acc).astype(BF16)
            hw = ones_ref.shape[0]
            ss = jnp.concatenate([jnp.dot(sq[:, c * hw:(c + 1) * hw], ones_ref[...], preferred_element_type=F32)
                                  for c in range(COL // hw)], axis=1)
            y = acc * lax.rsqrt(ss * (1.0 / HEAD_DIM) + EPS) * qkw_ref[j - Q_BLK:j - Q_BLK + 1, :]
            parts = []
            for c in range(COL // LANES):
                yc = y[:, c * LANES:(c + 1) * LANES]
                parts.append(yc * cosv + pltpu.roll(yc, LANES - ROPE_DIM // 2, 1) * s_lo
                             + pltpu.roll(yc, ROPE_DIM // 2, 1) * s_hi)
            res = jnp.concatenate(parts, axis=1)
        else:
            res = acc
        if j in _NAT_DST:
            dst = _NAT_DST[j]
            nat_ref[:, dst * COL:(dst + 1) * COL] = res.astype(BF16)
        else:
            kind, group = divmod(j - Q_BLK, len(DILATIONS))
            d = DILATIONS[group]
            out_ref = d4_ref if group == 1 else d16_ref
            for c in range(COL // LANES):
                perm_ref[c] = res[:, c * LANES:(c + 1) * LANES]
            for r in range(d):
                rows = [perm_ref[c, pl.ds(r, tm // d, stride=d), :] for c in range(COL // LANES)]
                out_ref[r, :, kind * COL:(kind + 1) * COL] = jnp.concatenate(rows, axis=1).astype(BF16)


def _proj_call(x, positions, norm_w, w_in, q_norm_w, k_norm_w, tm=512):
    bsz, length, _ = x.shape
    scale = LOG2E / math.sqrt(HEAD_DIM)
    qkw = jnp.concatenate([jnp.tile(q_norm_w.astype(F32) * scale, (1, HEADS)),
                           jnp.tile(k_norm_w.astype(F32), (1, HEADS)),
                           jnp.zeros((2, ATTN_W), F32)], axis=0)
    hid = jnp.arange(MXU_DIM) // HEAD_DIM
    ones = (hid[:, None] == hid[None, :]).astype(BF16)
    d4, d16 = DILATIONS[1], DILATIONS[2]
    freq, expand, rope_rows = _rope_tables()
    return pl.pallas_call(
        _proj_kernel,
        grid=(bsz, length // tm),
        in_specs=[pl.BlockSpec((None, tm, D_MODEL), lambda b, i: (b, i, 0)),
                  pl.BlockSpec((None, 1, tm), lambda b, i: (b, 0, i)),
                  _full_spec((1, D_MODEL)),
                  _full_spec((D_MODEL, IN_WIDTH)),
                  _full_spec((8, ATTN_W)),
                  _full_spec((MXU_DIM, MXU_DIM)),
                  _full_spec((ROPE_DIM // 2, 1)),
                  _full_spec((ROPE_DIM // 2, LANES)),
                  _full_spec((8, LANES))],
        out_specs=[pl.BlockSpec((None, tm, NAT_W), lambda b, i: (b, i, 0)),
                   pl.BlockSpec((None, d4, tm // d4, QKV_W), lambda b, i: (b, 0, i, 0)),
                   pl.BlockSpec((None, d16, tm // d16, QKV_W), lambda b, i: (b, 0, i, 0))],
        out_shape=[jax.ShapeDtypeStruct((bsz, length, NAT_W), BF16),
                   jax.ShapeDtypeStruct((bsz, d4, length // d4, QKV_W), BF16),
                   jax.ShapeDtypeStruct((bsz, d16, length // d16, QKV_W), BF16)],
        scratch_shapes=[pltpu.VMEM((COL // LANES, tm, LANES), F32), pltpu.VMEM((tm, D_MODEL), BF16)],
        compiler_params=pltpu.CompilerParams(dimension_semantics=("arbitrary", "arbitrary"),
                                             vmem_limit_bytes=VMEM_LIMIT),
        name="proj",
    )(x.astype(F32), positions.astype(jnp.int32)[:, None, :], norm_w.astype(F32)[None, :],
      w_in.astype(BF16), qkw, ones, freq, expand, rope_rows)


def _block_transpose(xs):
    xs = list(xs)
    blk = lax.broadcasted_iota(jnp.int32, xs[0].shape, 1) // SSM_GROUP
    dist = 1
    while dist < LANE_BLKS:
        upper = (blk & dist) != 0
        shift = dist * SSM_GROUP
        for i in range(LANE_BLKS):
            if i & dist:
                continue
            a, b = xs[i], xs[i + dist]
            xs[i] = jnp.where(upper, pltpu.roll(b, shift, 1), a)
            xs[i + dist] = jnp.where(upper, b, pltpu.roll(a, LANES - shift, 1))
        dist *= 2
    return xs


def _ssm_in_kernel(x_ref, nw_ref, w_ref, unat_ref, u2_ref, rows_ref):
    bsz, tl = x_ref.shape[0], x_ref.shape[1]
    x = x_ref[...].reshape(bsz * tl, D_MODEL)
    ms = jnp.mean(x * x, axis=-1, keepdims=True)
    h = (x * lax.rsqrt(ms + EPS) * nw_ref[...]).astype(BF16)
    u = jnp.dot(h, w_ref[...], preferred_element_type=F32)
    unat_ref[...] = u.reshape(bsz, tl, D_MODEL).astype(BF16)
    for b in range(bsz):
        for s8 in range(D_MODEL // LANES):
            rows_ref[s8, b * SSM_PITCH:b * SSM_PITCH + tl, :] = u[b * tl:(b + 1) * tl, s8 * LANES:(s8 + 1) * LANES]
    for c in range(tl // CHUNK):
        for s8 in range(D_MODEL // LANES):
            for half in range(CHUNK // LANE_BLKS):
                xs = []
                for k in range(LANE_BLKS):
                    tok = c * CHUNK + half * LANE_BLKS + k
                    v = rows_ref[s8, pl.ds(tok, bsz, stride=SSM_PITCH), :].astype(BF16)
                    xs.append(pltpu.bitcast(v, jnp.uint32))
                ys = _block_transpose(xs)
                for g in range(LANE_BLKS):
                    u2_ref[s8 * LANE_BLKS + g, c, :, half * LANES:(half + 1) * LANES] = pltpu.bitcast(ys[g], BF16)


def _ssm_in_call(x, norm_w, w_u):
    bsz, length, _ = x.shape
    tl = SSM_CB * CHUNK
    nc = length // CHUNK
    return pl.pallas_call(
        _ssm_in_kernel,
        grid=(length // tl,),
        in_specs=[pl.BlockSpec((bsz, tl, D_MODEL), lambda i: (0, i, 0)),
                  _full_spec((1, D_MODEL)), _full_spec((D_MODEL, D_MODEL))],
        out_specs=[pl.BlockSpec((bsz, tl, D_MODEL), lambda i: (0, i, 0)),
                   pl.BlockSpec((SSM_GROUPS, SSM_CB, bsz, CHUNK_W), lambda i: (0, i, 0, 0))],
        out_shape=[jax.ShapeDtypeStruct((bsz, length, D_MODEL), BF16),
                   jax.ShapeDtypeStruct((SSM_GROUPS, nc, bsz, CHUNK_W), BF16)],
        scratch_shapes=[pltpu.VMEM((D_MODEL // LANES, bsz * SSM_PITCH, LANES), F32)],
        compiler_params=pltpu.CompilerParams(dimension_semantics=("arbitrary",),
                                             vmem_limit_bytes=VMEM_LIMIT),
        name="ssm_in",
    )(x, norm_w.astype(F32)[None, :], w_u.astype(BF16))


def _ssm_out_kernel(y2_ref, u_ref, zs_ref, dskip_ref, wglu_ref, wso_ref, o_ref, rows_ref):
    bsz, tl = u_ref.shape[0], u_ref.shape[1]
    for c in range(tl // CHUNK):
        for s8 in range(D_MODEL // LANES):
            for half in range(CHUNK // LANE_BLKS):
                xs = [pltpu.bitcast(y2_ref[s8 * LANE_BLKS + g, c, :, half * LANES:(half + 1) * LANES], jnp.uint32)
                      for g in range(LANE_BLKS)]
                ys = _block_transpose(xs)
                for k in range(LANE_BLKS):
                    tok = c * CHUNK + half * LANE_BLKS + k
                    rows_ref[s8, pl.ds(tok, bsz, stride=SSM_PITCH), :] = pltpu.bitcast(ys[k], BF16).astype(F32)
    y = jnp.concatenate(
        [jnp.concatenate([rows_ref[s8, b * SSM_PITCH:b * SSM_PITCH + tl, :] for s8 in range(D_MODEL // LANES)],
                         axis=1) for b in range(bsz)], axis=0)
    u = u_ref[...].reshape(bsz * tl, D_MODEL).astype(F32)
    zs = zs_ref[...].reshape(bsz * tl, D_MODEL).astype(F32)
    y = y + dskip_ref[...] * u
    g = jax.nn.gelu(y, approximate=True)
    gate = jax.nn.sigmoid(jnp.dot(g.astype(BF16), wglu_ref[...], preferred_element_type=F32))
    ys_in = (g * gate * zs).astype(BF16)
    ys = jnp.dot(ys_in, wso_ref[...], preferred_element_type=F32)
    o_ref[...] = ys.reshape(bsz, tl, D_MODEL).astype(BF16)


def _ssm_out_call(y2, unat, nat, d_skip, w_glu, w_ssm_out):
    bsz, length, _ = unat.shape
    tl = SSM_CB * CHUNK
    tok = lambda blk: pl.BlockSpec((bsz, tl, D_MODEL), lambda i: (0, i, blk))
    return pl.pallas_call(
        _ssm_out_kernel,
        grid=(length // tl,),
        in_specs=[pl.BlockSpec((SSM_GROUPS, SSM_CB, bsz, CHUNK_W), lambda i: (0, i, 0, 0)),
                  tok(0), tok(NAT_ZS // 2),
                  _full_spec((1, D_MODEL)), _full_spec((D_MODEL, D_MODEL)), _full_spec((D_MODEL, D_MODEL))],
        out_specs=tok(0),
        out_shape=jax.ShapeDtypeStruct((bsz, length, D_MODEL), BF16),
        scratch_shapes=[pltpu.VMEM((D_MODEL // LANES, bsz * SSM_PITCH, LANES), F32)],
        compiler_params=pltpu.CompilerParams(dimension_semantics=("arbitrary",),
                                             vmem_limit_bytes=VMEM_LIMIT),
        name="ssm_out",
    )(y2, unat, nat, d_skip.astype(F32)[None, :], w_glu.astype(BF16), w_ssm_out.astype(BF16))


def _ssm_kernel(u_ref, st_ref, mt_ref, rt_ref, coef_ref, y_ref, hloc_ref, hprev_ref):
    nc, bsz = u_ref.shape[1], u_ref.shape[2]
    ns2 = 2 * SSM_STATE
    u = u_ref[0].reshape(nc * bsz, CHUNK_W)
    hloc_ref[...] = jnp.dot(u, st_ref[0], preferred_element_type=F32)
    a = jnp.broadcast_to(coef_ref[0, 0:1, :], (bsz, ns2))
    b = jnp.broadcast_to(coef_ref[0, 1:2, :], (bsz, ns2))

    def step(c, carry):
        hx, hy = carry
        r = pl.multiple_of(c * bsz, bsz)
        hprev_ref[pl.ds(r, bsz), :] = hx.astype(BF16)
        px = hloc_ref[pl.ds(r, bsz), 0:ns2]
        py = hloc_ref[pl.ds(r, bsz), ns2:2 * ns2]
        return a * hx + b * hy + px, a * hy - b * hx + py

    zero = jnp.zeros((bsz, ns2), F32)
    lax.fori_loop(0, nc, step, (zero, zero), unroll=8)
    y = jnp.dot(u, mt_ref[0], preferred_element_type=F32)
    y = y + jnp.dot(hprev_ref[...], rt_ref[0], preferred_element_type=F32)
    y_ref[0] = y.reshape(nc, bsz, CHUNK_W).astype(BF16)


def _ssm_call(u2, st, mt, rt, coef):
    g, nc, bsz, _ = u2.shape
    gspec = lambda shape: pl.BlockSpec((1,) + shape, lambda i: (i,) + (0,) * len(shape))
    return pl.pallas_call(
        _ssm_kernel,
        grid=(g,),
        in_specs=[gspec((nc, bsz, CHUNK_W)), gspec((CHUNK_W, 4 * SSM_STATE)), gspec((CHUNK_W, CHUNK_W)),
                  gspec((2 * SSM_STATE, CHUNK_W)), gspec((8, 2 * SSM_STATE))],
        out_specs=gspec((nc, bsz, CHUNK_W)),
        out_shape=jax.ShapeDtypeStruct(u2.shape, BF16),
        scratch_shapes=[pltpu.VMEM((nc * bsz, 4 * SSM_STATE), F32),
                        pltpu.VMEM((nc * bsz, 2 * SSM_STATE), BF16)],
        compiler_params=pltpu.CompilerParams(dimension_semantics=("arbitrary",),
                                             vmem_limit_bytes=VMEM_LIMIT),
        name="ssm_scan",
    )(u2, st, mt, rt, coef)


def _attn_kernel(q_ref, kc_ref, kp_ref, vc_ref, vp_ref, o_ref, lse_ref):
    nres, tq = q_ref.shape[0], q_ref.shape[1]
    qb = 128
    slab = pl.program_id(2)
    qi = lax.broadcasted_iota(jnp.int32, (qb, 2 * qb), 0)
    kk = lax.broadcasted_iota(jnp.int32, (qb, 2 * qb), 1)
    band = jnp.where((kk >= qi) & (kk <= qi + WINDOW_KEYS), 0.0, NEG_INF).astype(F32)
    band0 = band + jnp.where(jnp.logical_and(slab == 0, kk < qb), NEG_INF, 0.0).astype(F32)
    band = jnp.concatenate([band, band], axis=0)
    band0 = jnp.concatenate([band0, band0], axis=0)
    lo_q = lax.broadcasted_iota(jnp.int32, (qb, LANES), 1) < HEAD_DIM
    ones_kv = jnp.ones((2 * qb, LANES), BF16)

    for r, j in [(r, j) for r in range(nres) for j in range(tq // qb)]:
        madd = band0 if j == 0 else band
        rows = slice(j * qb, (j + 1) * qb)
        for hp in range(ATTN_W // LANES):
            cs = slice(hp * LANES, (hp + 1) * LANES)
            qp = q_ref[r, rows, cs]
            if j == 0:
                kp = jnp.concatenate([kp_ref[r, :, cs], kc_ref[r, 0:qb, cs]], axis=0)
                vp = jnp.concatenate([vp_ref[r, :, cs], vc_ref[r, 0:qb, cs]], axis=0)
            else:
                kp = kc_ref[r, (j - 1) * qb:(j + 1) * qb, cs]
                vp = vc_ref[r, (j - 1) * qb:(j + 1) * qb, cs]
            zero = jnp.zeros_like(qp)
            q2 = jnp.concatenate([jnp.where(lo_q, qp, zero), jnp.where(lo_q, zero, qp)], axis=0)
            s = lax.dot_general(q2, kp, (((1,), (1,)), ((), ())), preferred_element_type=F32)
            s = s + madd
            m = jnp.max(s, axis=1, keepdims=True)
            p = jnp.exp2(s - m)
            pv = jnp.dot(p.astype(BF16), jnp.concatenate([vp, ones_kv], axis=1), preferred_element_type=F32)
            num = jnp.where(lo_q, pv[:qb, :LANES], pv[qb:, :LANES])
            den = jnp.where(lo_q, pv[:qb, LANES:], pv[qb:, LANES:])
            o_ref[r, rows, cs] = (num / den).astype(BF16)
            lse_ref[r, rows, cs] = jnp.where(lo_q, m[:qb], m[qb:]) + jnp.log2(den)


def _attn_call(qkv, group, blocks, rows_per_step=1024):
    bsz, d, ld, _ = qkv.shape
    tq = min(rows_per_step, ld)
    nres = min(d, rows_per_step // tq)
    nslab = ld // tq
    per = tq // 128
    cur = lambda blk: pl.BlockSpec((None, nres, tq, COL), lambda b, r, i: (b, r, i, blk))
    prev = lambda blk: pl.BlockSpec(
        (None, nres, 128, COL), lambda b, r, i: (b, r, jnp.maximum(i * per - 1, 0), blk))
    ospec = pl.BlockSpec((None, nres, tq, COL), lambda b, r, i: (b, r, i, 0))
    qb, kb, vb = blocks
    return pl.pallas_call(
        _attn_kernel,
        grid=(bsz, d // nres, nslab),
        in_specs=[cur(qb), cur(kb), prev(kb), cur(vb), prev(vb)],
        out_specs=[ospec, ospec],
        out_shape=[jax.ShapeDtypeStruct((bsz, d, ld, ATTN_W), BF16),
                   jax.ShapeDtypeStruct((bsz, d, ld, ATTN_W), F32)],
        compiler_params=pltpu.CompilerParams(dimension_semantics=("arbitrary",) * 3,
                                             vmem_limit_bytes=VMEM_LIMIT),
        name=f"attn_d{DILATIONS[group]}",
    )(qkv, qkv, qkv, qkv, qkv)


def _merge_kernel(x_ref, ys_ref, za_ref, rs_ref, ra_ref,
                  a0_ref, a1_ref, a2_ref, l0_ref, l1_ref, l2_ref,
                  wao_ref, wo_ref, o_ref, il_ref):
    tm = x_ref.shape[0]

    def natural_order(ref):
        d = ref.shape[0]
        for r in range(d):
            blk = ref[r].astype(F32)
            for c in range(ATTN_W // LANES):
                il_ref[c, pl.ds(r, tm // d, stride=d), :] = blk[:, c * LANES:(c + 1) * LANES]
        return jnp.concatenate([il_ref[c] for c in range(ATTN_W // LANES)], axis=1)

    l0, l1, l2 = l0_ref[...], natural_order(l1_ref), natural_order(l2_ref)
    lm = jnp.maximum(jnp.maximum(l0, l1), l2)
    e0, e1, e2 = jnp.exp2(l0 - lm), jnp.exp2(l1 - lm), jnp.exp2(l2 - lm)
    att = (e0 * a0_ref[...].astype(F32) + e1 * natural_order(a1_ref)
           + e2 * natural_order(a2_ref)) / (e0 + e1 + e2)
    ya_in = (att * za_ref[...].astype(F32)).astype(BF16)
    ya = jnp.dot(ya_in, wao_ref[...], preferred_element_type=F32)

    m = rs_ref[...].astype(F32) * ys_ref[...].astype(F32) + ra_ref[...].astype(F32) * ya
    o_ref[...] = x_ref[...] + jnp.dot(m.astype(BF16), wo_ref[...], preferred_element_type=F32)


def _merge_call(x, ys, nat, attn, w_attn_out, w_o, tm=512):
    bsz, length, _ = x.shape
    wide = lambda blk: pl.BlockSpec((None, tm, D_MODEL), lambda b, i: (b, i, blk))
    half = lambda blk: pl.BlockSpec((None, tm, ATTN_W), lambda b, i: (b, i, blk))
    res = lambda d: pl.BlockSpec((None, d, tm // d, ATTN_W), lambda b, i: (b, 0, i, 0))
    (a0, l0), (a1, l1), (a2, l2) = attn
    d4, d16 = DILATIONS[1], DILATIONS[2]
    a0, l0 = a0.reshape(bsz, length, ATTN_W), l0.reshape(bsz, length, ATTN_W)
    return pl.pallas_call(
        _merge_kernel,
        grid=(bsz, length // tm),
        in_specs=[wide(0), wide(0), half(NAT_ZA), wide(NAT_RS // 2), wide(NAT_RA // 2),
                  half(0), res(d4), res(d16), half(0), res(d4), res(d16),
                  _full_spec((ATTN_W, D_MODEL)), _full_spec((D_MODEL, D_MODEL))],
        out_specs=wide(0),
        out_shape=jax.ShapeDtypeStruct((bsz, length, D_MODEL), F32),
        scratch_shapes=[pltpu.VMEM((ATTN_W // LANES, tm, LANES), F32)],
        compiler_params=pltpu.CompilerParams(dimension_semantics=("arbitrary", "arbitrary"),
                                             vmem_limit_bytes=VMEM_LIMIT),
        name="merge",
    )(x, ys, nat, nat, nat, a0, a1, a2, l0, l1, l2, w_attn_out.astype(BF16), w_o.astype(BF16))


def kernel(x, positions, norm_w, w_in, lam_re, lam_im, log_dt, b_re, b_im, c_re, c_im, d_skip, w_glu,
           q_norm_w, k_norm_w, w_ssm_out, w_attn_out, w_o):
    xf = x.astype(F32)
    for layer in range(norm_w.shape[0]):
        st, mt, rt, coef = _ssm_prep_call(lam_re[layer], lam_im[layer], log_dt[layer], b_re[layer],
                                          b_im[layer], c_re[layer], c_im[layer])
        nat, qkv4, qkv16 = _proj_call(xf, positions, norm_w[layer], w_in[layer], q_norm_w[layer],
                                      k_norm_w[layer])
        unat, u2 = _ssm_in_call(xf, norm_w[layer], w_in[layer][:, :D_MODEL])
        y2 = _ssm_call(u2, st, mt, rt, coef)
        ys = _ssm_out_call(y2, unat, nat, d_skip[layer], w_glu[layer], w_ssm_out[layer])
        attn = [_attn_call(nat[:, None], 0, (NAT_Q, NAT_K, NAT_V)),
                _attn_call(qkv4, 1, (0, 1, 2)),
                _attn_call(qkv16, 2, (0, 1, 2))]
        xf = _merge_call(xf, ys, nat, attn, w_attn_out[layer], w_o[layer])
    return xf.astype(x.dtype)
```

```python
import functools
import math

import jax
import jax.numpy as jnp
from jax import lax
from jax.experimental import pallas as pl
from jax.experimental.pallas import tpu as pltpu

F32 = jnp.float32
BF16 = jnp.bfloat16

D_MODEL = 1024
SSM_GROUP = 16
SSM_GROUPS = D_MODEL // SSM_GROUP
SSM_STATE = 64
CHUNK = 16
CHUNK_W = CHUNK * SSM_GROUP
HEAD_DIM = 64
HEADS = 8
ATTN_W = HEADS * HEAD_DIM
DILATIONS = (1, 4, 16)
WINDOW_KEYS = 128
ROPE_DIM = HEAD_DIM // 4
ROPE_THETA = 500000.0
EPS = 1e-6
NEG_INF = -1e30
LOG2E = math.log2(math.e)
COL = 512
U_BLK, ZS_BLK, Q_BLK, K_BLK, V_BLK, ZA_BLK, RS_BLK, RA_BLK = 0, 2, 4, 7, 10, 13, 14, 16
N_BLK = 18
IN_WIDTH = N_BLK * COL
NAT_ZS, NAT_Q, NAT_K, NAT_V, NAT_ZA, NAT_RS, NAT_RA = 0, 2, 3, 4, 5, 6, 8
NAT_W = 10 * COL
QKV_W = 3 * COL
LANES = 128
MXU_DIM = 256
LSE_LANES = LANES // HEADS
LANE_BLKS = LANES // SSM_GROUP
SSM_CB = 2
SSM_PITCH = 8 * 5
VMEM_LIMIT = 56 * 1024 * 1024


def _full_spec(shape):
    nd = len(shape)
    return pl.BlockSpec(shape, lambda *_: (0,) * nd, pipeline_mode=pl.Buffered(1))


def _ssm_prep_kernel(lam_ref, btr_ref, bti_ref, cr_ref, ci_ref, st_ref, mt_ref, rtt_ref, coef_ref):
    ns2 = 2 * SSM_STATE
    lam = lam_ref[0]
    lr, li, dt = lam[0:1, :], lam[1:2, :], jnp.exp(lam[2:3, :])
    mag = jnp.exp(lr * dt)
    ar = mag * jnp.cos(li * dt)
    ai = mag * jnp.sin(li * dt)
    den = lr * lr + li * li
    nr = ar - 1.0
    fr = (nr * lr + ai * li) / den
    fi = (ai * lr - nr * li) / den
    pr, pi = [jnp.ones_like(ar)], [jnp.zeros_like(ar)]
    for _ in range(CHUNK):
        pr, pi = pr + [pr[-1] * ar - pi[-1] * ai], pi + [pr[-1] * ai + pi[-1] * ar]
    by_step = lambda vals: jnp.concatenate([jnp.broadcast_to(v, (SSM_GROUP, ns2)) for v in vals], axis=0)
    per_step = lambda a: jnp.concatenate([a] * CHUNK, axis=0)
    im_part = lax.broadcasted_iota(jnp.int32, (CHUNK_W, ns2), 1) >= SSM_STATE

    btr, bti = per_step(btr_ref[0]), per_step(bti_ref[0])
    bbr = fr * btr - fi * bti
    bbi = fr * bti + fi * btr
    apr = by_step([pr[CHUNK - 1 - s] for s in range(CHUNK)])
    api = by_step([pi[CHUNK - 1 - s] for s in range(CHUNK)])
    wre = apr * bbr - api * bbi
    wim = apr * bbi + api * bbr
    st = jnp.where(im_part, wim, wre)
    st_sw = jnp.where(im_part, wre, wim)
    st_ref[0] = jnp.concatenate([st, st_sw], axis=1).astype(BF16)

    cr, ci = per_step(cr_ref[0]), per_step(ci_ref[0])
    qr = by_step([pr[t + 1] for t in range(CHUNK)])
    qi = by_step([pi[t + 1] for t in range(CHUNK)])
    rtt_ref[0] = jnp.where(im_part, -(cr * qi + ci * qr), cr * qr - ci * qi).astype(BF16)

    cct = jnp.where(im_part, -ci, cr)
    krw = lax.dot_general(st, cct, (((1,), (1,)), ((), ())), preferred_element_type=F32,
                          precision=lax.Precision.HIGHEST)
    lane_blk = lax.broadcasted_iota(jnp.int32, (CHUNK_W, CHUNK_W), 1) // SSM_GROUP
    mt = jnp.zeros((CHUNK_W, CHUNK_W), F32)
    for t in range(CHUNK):
        sh = SSM_GROUP * (CHUNK - 1 - t)
        if sh == 0:
            shifted = krw
        else:
            shifted = jnp.concatenate([krw[sh:, :], jnp.zeros((sh, CHUNK_W), F32)], axis=0)
        mt = jnp.where(lane_blk == t, shifted, mt)
    mt_ref[0] = mt.astype(BF16)

    im_row = lax.broadcasted_iota(jnp.int32, (1, ns2), 1) >= SSM_STATE
    bco = jnp.where(im_row, pi[CHUNK], -pi[CHUNK])
    coef_ref[0] = jnp.concatenate([pr[CHUNK], bco, jnp.zeros((6, ns2), F32)], axis=0)


def _ssm_prep_call(lam_re, lam_im, log_dt, b_re, b_im, c_re, c_im):
    g, n, p = SSM_GROUPS, SSM_STATE, SSM_GROUP
    f = lambda a: a.astype(F32)
    twice = lambda a: jnp.tile(f(a), (1,) * (a.ndim - 1) + (2,))
    ldt_b = jnp.broadcast_to(f(log_dt)[:, None], (g, n))
    lam = jnp.concatenate([twice(lam_re)[:, None], twice(lam_im)[:, None], twice(ldt_b)[:, None],
                           jnp.zeros((g, 5, 2 * n), F32)], axis=1)
    bt = lambda b: twice(b.transpose(0, 2, 1))
    gspec = lambda shape: pl.BlockSpec((1,) + shape, lambda i: (i, 0, 0))
    return pl.pallas_call(
        _ssm_prep_kernel,
        grid=(g,),
        in_specs=[gspec((8, 2 * n))] + [gspec((p, 2 * n))] * 4,
        out_specs=[gspec((CHUNK_W, 4 * n)), gspec((CHUNK_W, CHUNK_W)), gspec((CHUNK_W, 2 * n)),
                   gspec((8, 2 * n))],
        out_shape=[jax.ShapeDtypeStruct((g, CHUNK_W, 4 * n), BF16),
                   jax.ShapeDtypeStruct((g, CHUNK_W, CHUNK_W), BF16),
                   jax.ShapeDtypeStruct((g, CHUNK_W, 2 * n), BF16),
                   jax.ShapeDtypeStruct((g, 8, 2 * n), F32)],
        compiler_params=pltpu.CompilerParams(dimension_semantics=("arbitrary",)),
        name="ssm_prep",
    )(lam, bt(b_re), bt(b_im), twice(c_re), twice(c_im))


def _rope_tables():
    lane = jnp.arange(LANES) % HEAD_DIM
    half = ROPE_DIM // 2
    inv = ROPE_THETA ** (-jnp.arange(0, ROPE_DIM, 2, dtype=F32) / ROPE_DIM)
    expand = ((lane[None, :] < ROPE_DIM) & (lane[None, :] % half == jnp.arange(half)[:, None])).astype(F32)
    s_lo = jnp.where(lane < half, -1.0, 0.0)
    s_hi = jnp.where((lane >= half) & (lane < ROPE_DIM), 1.0, 0.0)
    unrot = jnp.where(lane >= ROPE_DIM, 1.0, 0.0)
    rows = jnp.concatenate([s_lo[None], s_hi[None], unrot[None], jnp.zeros((5, LANES), F32)], axis=0)
    return inv[:, None].astype(F32), expand, rows.astype(F32)


def _sigmoid(v):
    return 1.0 / (1.0 + jnp.exp(-v))


_NAT_DST = {2: NAT_ZS, 3: NAT_ZS + 1, Q_BLK: NAT_Q, K_BLK: NAT_K, V_BLK: NAT_V,
            ZA_BLK: NAT_ZA, 14: NAT_RS, 15: NAT_RS + 1, 16: NAT_RA, 17: NAT_RA + 1}


def _proj_kernel(x_ref, pos_ref, nw_ref, w_ref, qkw_ref, ones_ref, freq_ref, expand_ref, rope_ref,
                 nat_ref, d4_ref, d16_ref, perm_ref, h_ref):
    tm = x_ref.shape[0]
    x = x_ref[...]
    ms = jnp.mean(x * x, axis=-1, keepdims=True)
    h_ref[...] = (x * lax.rsqrt(ms + EPS) * nw_ref[...]).astype(BF16)
    ang = freq_ref[...] * pos_ref[...].astype(F32)

    def spread(v):
        hi = v.astype(BF16)
        lo = (v - hi.astype(F32)).astype(BF16)
        dims = (((0,), (0,)), ((), ()))
        e = expand_ref[...]
        return (lax.dot_general(hi, e, dims, preferred_element_type=F32)
                + lax.dot_general(lo, e, dims, preferred_element_type=F32))

    cosv = spread(jnp.cos(ang)) + rope_ref[2:3, :]
    sinv = spread(jnp.sin(ang))
    s_lo = sinv * rope_ref[0:1, :]
    s_hi = sinv * rope_ref[1:2, :]
    for j in range(ZS_BLK, N_BLK):
        acc = jnp.dot(h_ref[...], w_ref[:, j * COL:(j + 1) * COL], preferred_element_type=F32)
        if ZS_BLK <= j < Q_BLK or j == ZA_BLK:
            res = acc * _sigmoid(acc)
        elif j >= RS_BLK:
            res = _sigmoid(acc)
        elif Q_BLK <= j < V_BLK:
            sq = (acc * acc).astype(BF16)
            hw = ones_ref.shape[0]
            ss = jnp.concatenate([jnp.dot(sq[:, c * hw:(c + 1) * hw], ones_ref[...], preferred_element_type=F32)
                                  for c in range(COL // hw)], axis=1)
            y = acc * lax.rsqrt(ss * (1.0 / HEAD_DIM) + EPS) * qkw_ref[j - Q_BLK:j - Q_BLK + 1, :]
            parts = []
            for c in range(COL // LANES):
                yc = y[:, c * LANES:(c + 1) * LANES]
                parts.append(yc * cosv + pltpu.roll(yc, LANES - ROPE_DIM // 2, 1) * s_lo
                             + pltpu.roll(yc, ROPE_DIM // 2, 1) * s_hi)
            res = jnp.concatenate(parts, axis=1)
        else:
            res = acc
        if j in _NAT_DST:
            dst = _NAT_DST[j]
            nat_ref[:, dst * COL:(dst + 1) * COL] = res.astype(BF16)
        else:
            kind, group = divmod(j - Q_BLK, len(DILATIONS))
            d = DILATIONS[group]
            out_ref = d4_ref if group == 1 else d16_ref
            for c in range(COL // LANES):
                perm_ref[c] = res[:, c * LANES:(c + 1) * LANES]
            for r in range(d):
                rows = [perm_ref[c, pl.ds(r, tm // d, stride=d), :] for c in range(COL // LANES)]
                out_ref[r, :, kind * COL:(kind + 1) * COL] = jnp.concatenate(rows, axis=1).astype(BF16)


def _proj_call(x, positions, norm_w, w_in, q_norm_w, k_norm_w, tm=512):
    bsz, length, _ = x.shape
    scale = LOG2E / math.sqrt(HEAD_DIM)
    qkw = jnp.concatenate([jnp.tile(q_norm_w.astype(F32) * scale, (1, HEADS)),
                           jnp.tile(k_norm_w.astype(F32), (1, HEADS)),
                           jnp.zeros((2, ATTN_W), F32)], axis=0)
    hid = jnp.arange(MXU_DIM) // HEAD_DIM
    ones = (hid[:, None] == hid[None, :]).astype(BF16)
    d4, d16 = DILATIONS[1], DILATIONS[2]
    freq, expand, rope_rows = _rope_tables()
    return pl.pallas_call(
        _proj_kernel,
        grid=(bsz, length // tm),
        in_specs=[pl.BlockSpec((None, tm, D_MODEL), lambda b, i: (b, i, 0)),
                  pl.BlockSpec((None, 1, tm), lambda b, i: (b, 0, i)),
                  _full_spec((1, D_MODEL)),
                  _full_spec((D_MODEL, IN_WIDTH)),
                  _full_spec((8, ATTN_W)),
                  _full_spec((MXU_DIM, MXU_DIM)),
                  _full_spec((ROPE_DIM // 2, 1)),
                  _full_spec((ROPE_DIM // 2, LANES)),
                  _full_spec((8, LANES))],
        out_specs=[pl.BlockSpec((None, tm, NAT_W), lambda b, i: (b, i, 0)),
                   pl.BlockSpec((None, d4, tm // d4, QKV_W), lambda b, i: (b, 0, i, 0)),
                   pl.BlockSpec((None, d16, tm // d16, QKV_W), lambda b, i: (b, 0, i, 0))],
        out_shape=[jax.ShapeDtypeStruct((bsz, length, NAT_W), BF16),
                   jax.ShapeDtypeStruct((bsz, d4, length // d4, QKV_W), BF16),
                   jax.ShapeDtypeStruct((bsz, d16, length // d16, QKV_W), BF16)],
        scratch_shapes=[pltpu.VMEM((COL // LANES, tm, LANES), F32), pltpu.VMEM((tm, D_MODEL), BF16)],
        compiler_params=pltpu.CompilerParams(dimension_semantics=("arbitrary", "arbitrary"),
                                             vmem_limit_bytes=VMEM_LIMIT),
        name="proj",
    )(x.astype(F32), positions.astype(jnp.int32)[:, None, :], norm_w.astype(F32)[None, :],
      w_in.astype(BF16), qkw, ones, freq, expand, rope_rows)


def _block_transpose(xs):
    xs = list(xs)
    blk = lax.broadcasted_iota(jnp.int32, xs[0].shape, 1) // SSM_GROUP
    dist = 1
    while dist < LANE_BLKS:
        upper = (blk & dist) != 0
        shift = dist * SSM_GROUP
        for i in range(LANE_BLKS):
            if i & dist:
                continue
            a, b = xs[i], xs[i + dist]
            xs[i] = jnp.where(upper, pltpu.roll(b, shift, 1), a)
            xs[i + dist] = jnp.where(upper, b, pltpu.roll(a, LANES - shift, 1))
        dist *= 2
    return xs


def _ssm_in_kernel(x_ref, nw_ref, w_ref, unat_ref, u2_ref, rows_ref):
    bsz, tl = x_ref.shape[0], x_ref.shape[1]
    x = x_ref[...].reshape(bsz * tl, D_MODEL)
    ms = jnp.mean(x * x, axis=-1, keepdims=True)
    h = (x * lax.rsqrt(ms + EPS) * nw_ref[...]).astype(BF16)
    u = jnp.dot(h, w_ref[...], preferred_element_type=F32)
    unat_ref[...] = u.reshape(bsz, tl, D_MODEL).astype(BF16)
    for b in range(bsz):
        for s8 in range(D_MODEL // LANES):
            rows_ref[s8, b * SSM_PITCH:b * SSM_PITCH + tl, :] = u[b * tl:(b + 1) * tl, s8 * LANES:(s8 + 1) * LANES]
    for c in range(tl // CHUNK):
        for s8 in range(D_MODEL // LANES):
            for half in range(CHUNK // LANE_BLKS):
                xs = []
                for k in range(LANE_BLKS):
                    tok = c * CHUNK + half * LANE_BLKS + k
                    v = rows_ref[s8, pl.ds(tok, bsz, stride=SSM_PITCH), :].astype(BF16)
                    xs.append(pltpu.bitcast(v, jnp.uint32))
                ys = _block_transpose(xs)
                for g in range(LANE_BLKS):
                    u2_ref[s8 * LANE_BLKS + g, c, :, half * LANES:(half + 1) * LANES] = pltpu.bitcast(ys[g], BF16)


def _ssm_in_call(x, norm_w, w_u):
    bsz, length, _ = x.shape
    tl = SSM_CB * CHUNK
    nc = length // CHUNK
    return pl.pallas_call(
        _ssm_in_kernel,
        grid=(length // tl,),
        in_specs=[pl.BlockSpec((bsz, tl, D_MODEL), lambda i: (0, i, 0)),
                  _full_spec((1, D_MODEL)), _full_spec((D_MODEL, D_MODEL))],
        out_specs=[pl.BlockSpec((bsz, tl, D_MODEL), lambda i: (0, i, 0)),
                   pl.BlockSpec((SSM_GROUPS, SSM_CB, bsz, CHUNK_W), lambda i: (0, i, 0, 0))],
        out_shape=[jax.ShapeDtypeStruct((bsz, length, D_MODEL), BF16),
                   jax.ShapeDtypeStruct((SSM_GROUPS, nc, bsz, CHUNK_W), BF16)],
        scratch_shapes=[pltpu.VMEM((D_MODEL // LANES, bsz * SSM_PITCH, LANES), F32)],
        compiler_params=pltpu.CompilerParams(dimension_semantics=("arbitrary",),
                                             vmem_limit_bytes=VMEM_LIMIT),
        name="ssm_in",
    )(x, norm_w.astype(F32)[None, :], w_u.astype(BF16))


def _ssm_out_kernel(y2_ref, u_ref, zs_ref, dskip_ref, wglu_ref, wso_ref, o_ref, rows_ref):
    bsz, tl = u_ref.shape[0], u_ref.shape[1]
    for c in range(tl // CHUNK):
        for s8 in range(D_MODEL // LANES):
            for half in range(CHUNK // LANE_BLKS):
                xs = [pltpu.bitcast(y2_ref[s8 * LANE_BLKS + g, c, :, half * LANES:(half + 1) * LANES], jnp.uint32)
                      for g in range(LANE_BLKS)]
                ys = _block_transpose(xs)
                for k in range(LANE_BLKS):
                    tok = c * CHUNK + half * LANE_BLKS + k
                    rows_ref[s8, pl.ds(tok, bsz, stride=SSM_PITCH), :] = pltpu.bitcast(ys[k], BF16).astype(F32)
    y = jnp.concatenate(
        [jnp.concatenate([rows_ref[s8, b * SSM_PITCH:b * SSM_PITCH + tl, :] for s8 in range(D_MODEL // LANES)],
                         axis=1) for b in range(bsz)], axis=0)
    u = u_ref[...].reshape(bsz * tl, D_MODEL).astype(F32)
    zs = zs_ref[...].reshape(bsz * tl, D_MODEL).astype(F32)
    y = y + dskip_ref[...] * u
    g = jax.nn.gelu(y, approximate=True)
    gate = jax.nn.sigmoid(jnp.dot(g.astype(BF16), wglu_ref[...], preferred_element_type=F32))
    ys_in = (g * gate * zs).astype(BF16)
    ys = jnp.dot(ys_in, wso_ref[...], preferred_element_type=F32)
    o_ref[...] = ys.reshape(bsz, tl, D_MODEL).astype(BF16)


def _ssm_out_call(y2, unat, nat, d_skip, w_glu, w_ssm_out):
    bsz, length, _ = unat.shape
    tl = SSM_CB * CHUNK
    tok = lambda blk: pl.BlockSpec((bsz, tl, D_MODEL), lambda i: (0, i, blk))
    return pl.pallas_call(
        _ssm_out_kernel,
        grid=(length // tl,),
        in_specs=[pl.BlockSpec((SSM_GROUPS, SSM_CB, bsz, CHUNK_W), lambda i: (0, i, 0, 0)),
                  tok(0), tok(NAT_ZS // 2),
                  _full_spec((1, D_MODEL)), _full_spec((D_MODEL, D_MODEL)), _full_spec((D_MODEL, D_MODEL))],
        out_specs=tok(0),
        out_shape=jax.ShapeDtypeStruct((bsz, length, D_MODEL), BF16),
        scratch_shapes=[pltpu.VMEM((D_MODEL // LANES, bsz * SSM_PITCH, LANES), F32)],
        compiler_params=pltpu.CompilerParams(dimension_semantics=("arbitrary",),
                                             vmem_limit_bytes=VMEM_LIMIT),
        name="ssm_out",
    )(y2, unat, nat, d_skip.astype(F32)[None, :], w_glu.astype(BF16), w_ssm_out.astype(BF16))


def _ssm_kernel(u_ref, st_ref, mt_ref, rtt_ref, coef_ref, y_ref, hloc_ref, hprev_ref):
    nc, bsz = u_ref.shape[1], u_ref.shape[2]
    ns2 = 2 * SSM_STATE
    u = u_ref[0].reshape(nc * bsz, CHUNK_W)
    hloc_ref[...] = jnp.dot(u, st_ref[0], preferred_element_type=F32)
    a = jnp.broadcast_to(coef_ref[0, 0:1, :], (bsz, ns2))
    b = jnp.broadcast_to(coef_ref[0, 1:2, :], (bsz, ns2))

    def step(c, carry):
        hx, hy = carry
        r = pl.multiple_of(c * bsz, bsz)
        hprev_ref[pl.ds(r, bsz), :] = hx.astype(BF16)
        px = hloc_ref[pl.ds(r, bsz), 0:ns2]
        py = hloc_ref[pl.ds(r, bsz), ns2:2 * ns2]
        return a * hx + b * hy + px, a * hy - b * hx + py

    zero = jnp.zeros((bsz, ns2), F32)
    lax.fori_loop(0, nc, step, (zero, zero), unroll=8)
    y = jnp.dot(u, mt_ref[0], preferred_element_type=F32)
    y = y + lax.dot_general(hprev_ref[...], rtt_ref[0], (((1,), (1,)), ((), ())), preferred_element_type=F32)
    y_ref[0] = y.reshape(nc, bsz, CHUNK_W).astype(BF16)


def _ssm_call(u2, st, mt, rt, coef):
    g, nc, bsz, _ = u2.shape
    gspec = lambda shape: pl.BlockSpec((1,) + shape, lambda i: (i,) + (0,) * len(shape))
    return pl.pallas_call(
        _ssm_kernel,
        grid=(g,),
        in_specs=[gspec((nc, bsz, CHUNK_W)), gspec((CHUNK_W, 4 * SSM_STATE)), gspec((CHUNK_W, CHUNK_W)),
                  gspec((CHUNK_W, 2 * SSM_STATE)), gspec((8, 2 * SSM_STATE))],
        out_specs=gspec((nc, bsz, CHUNK_W)),
        out_shape=jax.ShapeDtypeStruct(u2.shape, BF16),
        scratch_shapes=[pltpu.VMEM((nc * bsz, 4 * SSM_STATE), F32),
                        pltpu.VMEM((nc * bsz, 2 * SSM_STATE), BF16)],
        compiler_params=pltpu.CompilerParams(dimension_semantics=("arbitrary",),
                                             vmem_limit_bytes=VMEM_LIMIT),
        name="ssm_scan",
    )(u2, st, mt, rt, coef)


def _attn_kernel(q_ref, kc_ref, kp_ref, vc_ref, vp_ref, o_ref, lse_ref):
    nres, tq = q_ref.shape[0], q_ref.shape[1]
    qb = 128
    slab = pl.program_id(2)
    qi = lax.broadcasted_iota(jnp.int32, (qb, 2 * qb), 0)
    kk = lax.broadcasted_iota(jnp.int32, (qb, 2 * qb), 1)
    band = jnp.where((kk >= qi) & (kk <= qi + WINDOW_KEYS), 0.0, NEG_INF).astype(F32)
    band0 = band + jnp.where(jnp.logical_and(slab == 0, kk < qb), NEG_INF, 0.0).astype(F32)
    band = jnp.concatenate([band, band], axis=0)
    band0 = jnp.concatenate([band0, band0], axis=0)
    lo_q = lax.broadcasted_iota(jnp.int32, (qb, LANES), 1) < HEAD_DIM
    ones_kv = jnp.ones((2 * qb, LANES), BF16)

    head_slot = lax.broadcasted_iota(jnp.int32, (qb, LANES), 1) // LSE_LANES

    for r, j in [(r, j) for r in range(nres) for j in range(tq // qb)]:
        madd = band0 if j == 0 else band
        rows = slice(j * qb, (j + 1) * qb)
        m_all = jnp.zeros((qb, LANES), F32)
        den_all = jnp.ones((qb, LANES), F32)
        for hp in range(ATTN_W // LANES):
            cs = slice(hp * LANES, (hp + 1) * LANES)
            qp = q_ref[r, rows, cs]
            if j == 0:
                kp = jnp.concatenate([kp_ref[r, :, cs], kc_ref[r, 0:qb, cs]], axis=0)
                vp = jnp.concatenate([vp_ref[r, :, cs], vc_ref[r, 0:qb, cs]], axis=0)
            else:
                kp = kc_ref[r, (j - 1) * qb:(j + 1) * qb, cs]
                vp = vc_ref[r, (j - 1) * qb:(j + 1) * qb, cs]
            zero = jnp.zeros_like(qp)
            q2 = jnp.concatenate([jnp.where(lo_q, qp, zero), jnp.where(lo_q, zero, qp)], axis=0)
            s = lax.dot_general(q2, kp, (((1,), (1,)), ((), ())), preferred_element_type=F32)
            s = s + madd
            m = jnp.max(s, axis=1, keepdims=True)
            p = jnp.exp2(s - m)
            pv = jnp.dot(p.astype(BF16), jnp.concatenate([vp, ones_kv], axis=1), preferred_element_type=F32)
            num = jnp.where(lo_q, pv[:qb, :LANES], pv[qb:, :LANES])
            den = jnp.where(lo_q, pv[:qb, LANES:], pv[qb:, LANES:])
            o_ref[r, rows, cs] = (num / den).astype(BF16)
            for side in range(2):
                slot = head_slot == 2 * hp + side
                half = slice(side * qb, (side + 1) * qb)
                m_all = jnp.where(slot, m[half], m_all)
                den_all = jnp.where(slot, pv[half, LANES:], den_all)
        lse_ref[r, rows, :] = m_all + jnp.log2(den_all)


def _attn_call(qkv, group, blocks, rows_per_step=1024):
    bsz, d, ld, _ = qkv.shape
    tq = min(rows_per_step, ld)
    nres = min(d, rows_per_step // tq)
    nslab = ld // tq
    per = tq // 128
    cur = lambda blk: pl.BlockSpec((None, nres, tq, COL), lambda b, r, i: (b, r, i, blk))
    prev = lambda blk: pl.BlockSpec(
        (None, nres, 128, COL), lambda b, r, i: (b, r, jnp.maximum(i * per - 1, 0), blk))
    ospec = lambda w: pl.BlockSpec((None, nres, tq, w), lambda b, r, i: (b, r, i, 0))
    qb, kb, vb = blocks
    return pl.pallas_call(
        _attn_kernel,
        grid=(bsz, d // nres, nslab),
        in_specs=[cur(qb), cur(kb), prev(kb), cur(vb), prev(vb)],
        out_specs=[ospec(ATTN_W), ospec(LANES)],
        out_shape=[jax.ShapeDtypeStruct((bsz, d, ld, ATTN_W), BF16),
                   jax.ShapeDtypeStruct((bsz, d, ld, LANES), F32)],
        compiler_params=pltpu.CompilerParams(dimension_semantics=("arbitrary",) * 3,
                                             vmem_limit_bytes=VMEM_LIMIT),
        name=f"attn_d{DILATIONS[group]}",
    )(qkv, qkv, qkv, qkv, qkv)


def _merge_kernel(x_ref, ys_ref, za_ref, rs_ref, ra_ref,
                  a0_ref, a1_ref, a2_ref, l0_ref, l1_ref, l2_ref,
                  spread_ref, wao_ref, wo_ref, o_ref, il_ref):
    tm = x_ref.shape[0]

    def natural_order(ref):
        d, nslab = ref.shape[0], ref.shape[2] // LANES
        for r in range(d):
            blk = ref[r].astype(F32)
            for c in range(nslab):
                il_ref[c, pl.ds(r, tm // d, stride=d), :] = blk[:, c * LANES:(c + 1) * LANES]
        return jnp.concatenate([il_ref[c] for c in range(nslab)], axis=1)

    def per_head_to_lanes(w):
        hi = w.astype(BF16)
        lo = (w - hi.astype(F32)).astype(BF16)
        return (jnp.dot(hi, spread_ref[...], preferred_element_type=F32)
                + jnp.dot(lo, spread_ref[...], preferred_element_type=F32))

    l0, l1, l2 = l0_ref[...], natural_order(l1_ref), natural_order(l2_ref)
    lm = jnp.maximum(jnp.maximum(l0, l1), l2)
    e0, e1, e2 = jnp.exp2(l0 - lm), jnp.exp2(l1 - lm), jnp.exp2(l2 - lm)
    inv = 1.0 / (e0 + e1 + e2)
    att = (per_head_to_lanes(e0 * inv) * a0_ref[...].astype(F32)
           + per_head_to_lanes(e1 * inv) * natural_order(a1_ref)
           + per_head_to_lanes(e2 * inv) * natural_order(a2_ref))
    ya_in = (att * za_ref[...].astype(F32)).astype(BF16)
    ya = jnp.dot(ya_in, wao_ref[...], preferred_element_type=F32)

    m = rs_ref[...].astype(F32) * ys_ref[...].astype(F32) + ra_ref[...].astype(F32) * ya
    o_ref[...] = x_ref[...] + jnp.dot(m.astype(BF16), wo_ref[...], preferred_element_type=F32)


def _merge_call(x, ys, nat, attn, w_attn_out, w_o, tm=512):
    bsz, length, _ = x.shape
    wide = lambda blk: pl.BlockSpec((None, tm, D_MODEL), lambda b, i: (b, i, blk))
    half = lambda blk: pl.BlockSpec((None, tm, ATTN_W), lambda b, i: (b, i, blk))
    res = lambda d, w: pl.BlockSpec((None, d, tm // d, w), lambda b, i: (b, 0, i, 0))
    (a0, l0), (a1, l1), (a2, l2) = attn
    d4, d16 = DILATIONS[1], DILATIONS[2]
    a0, l0 = a0.reshape(bsz, length, ATTN_W), l0.reshape(bsz, length, LANES)
    spread = (jnp.arange(LANES)[:, None] == (jnp.arange(ATTN_W)[None, :] // HEAD_DIM) * LSE_LANES).astype(BF16)
    return pl.pallas_call(
        _merge_kernel,
        grid=(bsz, length // tm),
        in_specs=[wide(0), wide(0), half(NAT_ZA), wide(NAT_RS // 2), wide(NAT_RA // 2),
                  half(0), res(d4, ATTN_W), res(d16, ATTN_W),
                  pl.BlockSpec((None, tm, LANES), lambda b, i: (b, i, 0)), res(d4, LANES), res(d16, LANES),
                  _full_spec((LANES, ATTN_W)), _full_spec((ATTN_W, D_MODEL)), _full_spec((D_MODEL, D_MODEL))],
        out_specs=wide(0),
        out_shape=jax.ShapeDtypeStruct((bsz, length, D_MODEL), F32),
        scratch_shapes=[pltpu.VMEM((ATTN_W // LANES, tm, LANES), F32)],
        compiler_params=pltpu.CompilerParams(dimension_semantics=("arbitrary", "arbitrary"),
                                             vmem_limit_bytes=VMEM_LIMIT),
        name="merge",
    )(x, ys, nat, nat, nat, a0, a1, a2, l0, l1, l2, spread, w_attn_out.astype(BF16), w_o.astype(BF16))


def kernel(x, positions, norm_w, w_in, lam_re, lam_im, log_dt, b_re, b_im, c_re, c_im, d_skip, w_glu,
           q_norm_w, k_norm_w, w_ssm_out, w_attn_out, w_o):
    xf = x.astype(F32)
    for layer in range(norm_w.shape[0]):
        st, mt, rt, coef = _ssm_prep_call(lam_re[layer], lam_im[layer], log_dt[layer], b_re[layer],
                                          b_im[layer], c_re[layer], c_im[layer])
        nat, qkv4, qkv16 = _proj_call(xf, positions, norm_w[layer], w_in[layer], q_norm_w[layer],
                                      k_norm_w[layer])
        unat, u2 = _ssm_in_call(xf, norm_w[layer], w_in[layer][:, :D_MODEL])
        y2 = _ssm_call(u2, st, mt, rt, coef)
        ys = _ssm_out_call(y2, unat, nat, d_skip[layer], w_glu[layer], w_ssm_out[layer])
        attn = [_attn_call(nat[:, None], 0, (NAT_Q, NAT_K, NAT_V)),
                _attn_call(qkv4, 1, (0, 1, 2)),
                _attn_call(qkv16, 2, (0, 1, 2))]
        xf = _merge_call(xf, ys, nat, attn, w_attn_out[layer], w_o[layer])
    return xf.astype(x.dtype)
```

```python
import functools
import math

import jax
import jax.numpy as jnp
from jax import lax
from jax.experimental import pallas as pl
from jax.experimental.pallas import tpu as pltpu

F32 = jnp.float32
BF16 = jnp.bfloat16

D_MODEL = 1024
SSM_GROUP = 16
SSM_GROUPS = D_MODEL // SSM_GROUP
SSM_STATE = 64
CHUNK = 16
CHUNK_W = CHUNK * SSM_GROUP
HEAD_DIM = 64
HEADS = 8
ATTN_W = HEADS * HEAD_DIM
DILATIONS = (1, 4, 16)
WINDOW_KEYS = 128
ROPE_DIM = HEAD_DIM // 4
ROPE_THETA = 500000.0
EPS = 1e-6
NEG_INF = -1e30
LOG2E = math.log2(math.e)
COL = 512
U_BLK, ZS_BLK, Q_BLK, K_BLK, V_BLK, ZA_BLK, RS_BLK, RA_BLK = 0, 2, 4, 7, 10, 13, 14, 16
N_BLK = 18
IN_WIDTH = N_BLK * COL
NAT_ZS, NAT_Q, NAT_K, NAT_V, NAT_ZA, NAT_RS, NAT_RA = 0, 2, 3, 4, 5, 6, 8
NAT_W = 10 * COL
QKV_W = 3 * COL
LANES = 128
MXU_DIM = 256
LSE_LANES = LANES // HEADS
LANE_BLKS = LANES // SSM_GROUP
SSM_CB = 4
SSM_PITCH = 8 * 3
VMEM_LIMIT = 56 * 1024 * 1024


def _full_spec(shape):
    nd = len(shape)
    return pl.BlockSpec(shape, lambda *_: (0,) * nd, pipeline_mode=pl.Buffered(1))


def _ssm_prep_kernel(lam_ref, btr_ref, bti_ref, cr_ref, ci_ref, st_ref, mt_ref, rtt_ref, coef_ref):
    ns2 = 2 * SSM_STATE
    lam = lam_ref[0]
    lr, li, dt = lam[0:1, :], lam[1:2, :], jnp.exp(lam[2:3, :])
    mag = jnp.exp(lr * dt)
    ar = mag * jnp.cos(li * dt)
    ai = mag * jnp.sin(li * dt)
    den = lr * lr + li * li
    nr = ar - 1.0
    fr = (nr * lr + ai * li) / den
    fi = (ai * lr - nr * li) / den
    pr, pi = [jnp.ones_like(ar)], [jnp.zeros_like(ar)]
    for _ in range(CHUNK):
        pr, pi = pr + [pr[-1] * ar - pi[-1] * ai], pi + [pr[-1] * ai + pi[-1] * ar]
    by_step = lambda vals: jnp.concatenate([jnp.broadcast_to(v, (SSM_GROUP, ns2)) for v in vals], axis=0)
    per_step = lambda a: jnp.concatenate([a] * CHUNK, axis=0)
    im_part = lax.broadcasted_iota(jnp.int32, (CHUNK_W, ns2), 1) >= SSM_STATE

    btr, bti = per_step(btr_ref[0]), per_step(bti_ref[0])
    bbr = fr * btr - fi * bti
    bbi = fr * bti + fi * btr
    apr = by_step([pr[CHUNK - 1 - s] for s in range(CHUNK)])
    api = by_step([pi[CHUNK - 1 - s] for s in range(CHUNK)])
    wre = apr * bbr - api * bbi
    wim = apr * bbi + api * bbr
    st = jnp.where(im_part, wim, wre)
    st_sw = jnp.where(im_part, wre, wim)
    st_ref[0] = jnp.concatenate([st, st_sw], axis=1).astype(BF16)

    cr, ci = per_step(cr_ref[0]), per_step(ci_ref[0])
    qr = by_step([pr[t + 1] for t in range(CHUNK)])
    qi = by_step([pi[t + 1] for t in range(CHUNK)])
    rtt_ref[0] = jnp.where(im_part, -(cr * qi + ci * qr), cr * qr - ci * qi).astype(BF16)

    cct = jnp.where(im_part, -ci, cr)
    krw = lax.dot_general(st, cct, (((1,), (1,)), ((), ())), preferred_element_type=F32,
                          precision=lax.Precision.HIGHEST)
    lane_blk = lax.broadcasted_iota(jnp.int32, (CHUNK_W, CHUNK_W), 1) // SSM_GROUP
    mt = jnp.zeros((CHUNK_W, CHUNK_W), F32)
    for t in range(CHUNK):
        sh = SSM_GROUP * (CHUNK - 1 - t)
        if sh == 0:
            shifted = krw
        else:
            shifted = jnp.concatenate([krw[sh:, :], jnp.zeros((sh, CHUNK_W), F32)], axis=0)
        mt = jnp.where(lane_blk == t, shifted, mt)
    mt_ref[0] = mt.astype(BF16)

    im_row = lax.broadcasted_iota(jnp.int32, (1, ns2), 1) >= SSM_STATE
    bco = jnp.where(im_row, pi[CHUNK], -pi[CHUNK])
    coef_ref[0] = jnp.concatenate([pr[CHUNK], bco, jnp.zeros((6, ns2), F32)], axis=0)


def _ssm_prep_call(lam_re, lam_im, log_dt, b_re, b_im, c_re, c_im):
    g, n, p = SSM_GROUPS, SSM_STATE, SSM_GROUP
    f = lambda a: a.astype(F32)
    twice = lambda a: jnp.tile(f(a), (1,) * (a.ndim - 1) + (2,))
    ldt_b = jnp.broadcast_to(f(log_dt)[:, None], (g, n))
    lam = jnp.concatenate([twice(lam_re)[:, None], twice(lam_im)[:, None], twice(ldt_b)[:, None],
                           jnp.zeros((g, 5, 2 * n), F32)], axis=1)
    bt = lambda b: twice(b.transpose(0, 2, 1))
    gspec = lambda shape: pl.BlockSpec((1,) + shape, lambda i: (i, 0, 0))
    return pl.pallas_call(
        _ssm_prep_kernel,
        grid=(g,),
        in_specs=[gspec((8, 2 * n))] + [gspec((p, 2 * n))] * 4,
        out_specs=[gspec((CHUNK_W, 4 * n)), gspec((CHUNK_W, CHUNK_W)), gspec((CHUNK_W, 2 * n)),
                   gspec((8, 2 * n))],
        out_shape=[jax.ShapeDtypeStruct((g, CHUNK_W, 4 * n), BF16),
                   jax.ShapeDtypeStruct((g, CHUNK_W, CHUNK_W), BF16),
                   jax.ShapeDtypeStruct((g, CHUNK_W, 2 * n), BF16),
                   jax.ShapeDtypeStruct((g, 8, 2 * n), F32)],
        compiler_params=pltpu.CompilerParams(dimension_semantics=("arbitrary",)),
        name="ssm_prep",
    )(lam, bt(b_re), bt(b_im), twice(c_re), twice(c_im))


def _rope_tables():
    lane = jnp.arange(LANES) % HEAD_DIM
    half = ROPE_DIM // 2
    inv = ROPE_THETA ** (-jnp.arange(0, ROPE_DIM, 2, dtype=F32) / ROPE_DIM)
    expand = ((lane[None, :] < ROPE_DIM) & (lane[None, :] % half == jnp.arange(half)[:, None])).astype(F32)
    s_lo = jnp.where(lane < half, -1.0, 0.0)
    s_hi = jnp.where((lane >= half) & (lane < ROPE_DIM), 1.0, 0.0)
    unrot = jnp.where(lane >= ROPE_DIM, 1.0, 0.0)
    rows = jnp.concatenate([s_lo[None], s_hi[None], unrot[None], jnp.zeros((5, LANES), F32)], axis=0)
    return inv[:, None].astype(F32), expand, rows.astype(F32)


def _sigmoid(v):
    return 1.0 / (1.0 + jnp.exp(-v))


_NAT_DST = {2: NAT_ZS, 3: NAT_ZS + 1, Q_BLK: NAT_Q, K_BLK: NAT_K, V_BLK: NAT_V,
            ZA_BLK: NAT_ZA, 14: NAT_RS, 15: NAT_RS + 1, 16: NAT_RA, 17: NAT_RA + 1}


def _proj_kernel(x_ref, pos_ref, nw_ref, w_ref, qkw_ref, ones_ref, freq_ref, expand_ref, rope_ref,
                 nat_ref, d4_ref, d16_ref, perm_ref, h_ref):
    tm = x_ref.shape[0]
    x = x_ref[...]
    ms = jnp.mean(x * x, axis=-1, keepdims=True)
    h_ref[...] = (x * lax.rsqrt(ms + EPS) * nw_ref[...]).astype(BF16)
    ang = freq_ref[...] * pos_ref[...].astype(F32)

    def spread(v):
        hi = v.astype(BF16)
        lo = (v - hi.astype(F32)).astype(BF16)
        dims = (((0,), (0,)), ((), ()))
        e = expand_ref[...]
        return (lax.dot_general(hi, e, dims, preferred_element_type=F32)
                + lax.dot_general(lo, e, dims, preferred_element_type=F32))

    cosv = spread(jnp.cos(ang)) + rope_ref[2:3, :]
    sinv = spread(jnp.sin(ang))
    s_lo = sinv * rope_ref[0:1, :]
    s_hi = sinv * rope_ref[1:2, :]
    for j in range(ZS_BLK, N_BLK):
        acc = jnp.dot(h_ref[...], w_ref[:, j * COL:(j + 1) * COL], preferred_element_type=F32)
        if ZS_BLK <= j < Q_BLK or j == ZA_BLK:
            res = acc * _sigmoid(acc)
        elif j >= RS_BLK:
            res = _sigmoid(acc)
        elif Q_BLK <= j < V_BLK:
            sq = (acc * acc).astype(BF16)
            hw = ones_ref.shape[0]
            ss = jnp.concatenate([jnp.dot(sq[:, c * hw:(c + 1) * hw], ones_ref[...], preferred_element_type=F32)
                                  for c in range(COL // hw)], axis=1)
            y = acc * lax.rsqrt(ss * (1.0 / HEAD_DIM) + EPS) * qkw_ref[j - Q_BLK:j - Q_BLK + 1, :]
            parts = []
            for c in range(COL // LANES):
                yc = y[:, c * LANES:(c + 1) * LANES]
                parts.append(yc * cosv + pltpu.roll(yc, LANES - ROPE_DIM // 2, 1) * s_lo
                             + pltpu.roll(yc, ROPE_DIM // 2, 1) * s_hi)
            res = jnp.concatenate(parts, axis=1)
        else:
            res = acc
        if j in _NAT_DST:
            dst = _NAT_DST[j]
            nat_ref[:, dst * COL:(dst + 1) * COL] = res.astype(BF16)
        else:
            kind, group = divmod(j - Q_BLK, len(DILATIONS))
            d = DILATIONS[group]
            out_ref = d4_ref if group == 1 else d16_ref
            for c in range(COL // LANES):
                perm_ref[c] = res[:, c * LANES:(c + 1) * LANES]
            for r in range(d):
                rows = [perm_ref[c, pl.ds(r, tm // d, stride=d), :] for c in range(COL // LANES)]
                out_ref[r, :, kind * COL:(kind + 1) * COL] = jnp.concatenate(rows, axis=1).astype(BF16)


def _proj_call(x, positions, norm_w, w_in, q_norm_w, k_norm_w, tm=512):
    bsz, length, _ = x.shape
    scale = LOG2E / math.sqrt(HEAD_DIM)
    qkw = jnp.concatenate([jnp.tile(q_norm_w.astype(F32) * scale, (1, HEADS)),
                           jnp.tile(k_norm_w.astype(F32), (1, HEADS)),
                           jnp.zeros((2, ATTN_W), F32)], axis=0)
    hid = jnp.arange(MXU_DIM) // HEAD_DIM
    ones = (hid[:, None] == hid[None, :]).astype(BF16)
    d4, d16 = DILATIONS[1], DILATIONS[2]
    freq, expand, rope_rows = _rope_tables()
    return pl.pallas_call(
        _proj_kernel,
        grid=(bsz, length // tm),
        in_specs=[pl.BlockSpec((None, tm, D_MODEL), lambda b, i: (b, i, 0)),
                  pl.BlockSpec((None, 1, tm), lambda b, i: (b, 0, i)),
                  _full_spec((1, D_MODEL)),
                  _full_spec((D_MODEL, IN_WIDTH)),
                  _full_spec((8, ATTN_W)),
                  _full_spec((MXU_DIM, MXU_DIM)),
                  _full_spec((ROPE_DIM // 2, 1)),
                  _full_spec((ROPE_DIM // 2, LANES)),
                  _full_spec((8, LANES))],
        out_specs=[pl.BlockSpec((None, tm, NAT_W), lambda b, i: (b, i, 0)),
                   pl.BlockSpec((None, d4, tm // d4, QKV_W), lambda b, i: (b, 0, i, 0)),
                   pl.BlockSpec((None, d16, tm // d16, QKV_W), lambda b, i: (b, 0, i, 0))],
        out_shape=[jax.ShapeDtypeStruct((bsz, length, NAT_W), BF16),
                   jax.ShapeDtypeStruct((bsz, d4, length // d4, QKV_W), BF16),
                   jax.ShapeDtypeStruct((bsz, d16, length // d16, QKV_W), BF16)],
        scratch_shapes=[pltpu.VMEM((COL // LANES, tm, LANES), F32), pltpu.VMEM((tm, D_MODEL), BF16)],
        compiler_params=pltpu.CompilerParams(dimension_semantics=("arbitrary", "arbitrary"),
                                             vmem_limit_bytes=VMEM_LIMIT),
        name="proj",
    )(x.astype(F32), positions.astype(jnp.int32)[:, None, :], norm_w.astype(F32)[None, :],
      w_in.astype(BF16), qkw, ones, freq, expand, rope_rows)


def _block_transpose(xs):
    xs = list(xs)
    blk = lax.broadcasted_iota(jnp.int32, xs[0].shape, 1) // SSM_GROUP
    dist = 1
    while dist < LANE_BLKS:
        upper = (blk & dist) != 0
        shift = dist * SSM_GROUP
        for i in range(LANE_BLKS):
            if i & dist:
                continue
            a, b = xs[i], xs[i + dist]
            xs[i] = jnp.where(upper, pltpu.roll(b, shift, 1), a)
            xs[i + dist] = jnp.where(upper, b, pltpu.roll(a, LANES - shift, 1))
        dist *= 2
    return xs


def _ssm_in_kernel(x_ref, nw_ref, w_ref, unat_ref, u2_ref, rows_ref):
    bsz, tl = x_ref.shape[0], x_ref.shape[1]
    for c in range(tl // CHUNK):
        x = x_ref[:, c * CHUNK:(c + 1) * CHUNK, :].reshape(bsz * CHUNK, D_MODEL)
        ms = jnp.mean(x * x, axis=-1, keepdims=True)
        h = (x * lax.rsqrt(ms + EPS) * nw_ref[...]).astype(BF16)
        u = jnp.dot(h, w_ref[...], preferred_element_type=F32)
        unat_ref[:, c * CHUNK:(c + 1) * CHUNK, :] = u.reshape(bsz, CHUNK, D_MODEL).astype(BF16)
        for b in range(bsz):
            for s8 in range(D_MODEL // LANES):
                rows_ref[c, s8, b * SSM_PITCH:b * SSM_PITCH + CHUNK, :] = (
                    u[b * CHUNK:(b + 1) * CHUNK, s8 * LANES:(s8 + 1) * LANES])
        for s8 in range(D_MODEL // LANES):
            for half in range(CHUNK // LANE_BLKS):
                xs = []
                for k in range(LANE_BLKS):
                    v = rows_ref[c, s8, pl.ds(half * LANE_BLKS + k, bsz, stride=SSM_PITCH), :].astype(BF16)
                    xs.append(pltpu.bitcast(v, jnp.uint32))
                ys = _block_transpose(xs)
                for g in range(LANE_BLKS):
                    u2_ref[s8 * LANE_BLKS + g, c, :, half * LANES:(half + 1) * LANES] = pltpu.bitcast(ys[g], BF16)


def _ssm_in_call(x, norm_w, w_u):
    bsz, length, _ = x.shape
    tl = SSM_CB * CHUNK
    nc = length // CHUNK
    return pl.pallas_call(
        _ssm_in_kernel,
        grid=(length // tl,),
        in_specs=[pl.BlockSpec((bsz, tl, D_MODEL), lambda i: (0, i, 0)),
                  _full_spec((1, D_MODEL)), _full_spec((D_MODEL, D_MODEL))],
        out_specs=[pl.BlockSpec((bsz, tl, D_MODEL), lambda i: (0, i, 0)),
                   pl.BlockSpec((SSM_GROUPS, SSM_CB, bsz, CHUNK_W), lambda i: (0, i, 0, 0))],
        out_shape=[jax.ShapeDtypeStruct((bsz, length, D_MODEL), BF16),
                   jax.ShapeDtypeStruct((SSM_GROUPS, nc, bsz, CHUNK_W), BF16)],
        scratch_shapes=[pltpu.VMEM((SSM_CB, D_MODEL // LANES, bsz * SSM_PITCH, LANES), F32)],
        compiler_params=pltpu.CompilerParams(dimension_semantics=("arbitrary",),
                                             vmem_limit_bytes=VMEM_LIMIT),
        name="ssm_in",
    )(x, norm_w.astype(F32)[None, :], w_u.astype(BF16))


def _ssm_out_kernel(y2_ref, u_ref, zs_ref, dskip_ref, wglu_ref, wso_ref, o_ref, rows_ref):
    bsz, tl = u_ref.shape[0], u_ref.shape[1]
    for c in range(tl // CHUNK):
        for s8 in range(D_MODEL // LANES):
            for half in range(CHUNK // LANE_BLKS):
                xs = [pltpu.bitcast(y2_ref[s8 * LANE_BLKS + g, c, :, half * LANES:(half + 1) * LANES], jnp.uint32)
                      for g in range(LANE_BLKS)]
                ys = _block_transpose(xs)
                for k in range(LANE_BLKS):
                    rows_ref[c, s8, pl.ds(half * LANE_BLKS + k, bsz, stride=SSM_PITCH), :] = (
                        pltpu.bitcast(ys[k], BF16).astype(F32))
        y = jnp.concatenate(
            [jnp.concatenate([rows_ref[c, s8, b * SSM_PITCH:b * SSM_PITCH + CHUNK, :]
                              for s8 in range(D_MODEL // LANES)], axis=1)
             for b in range(bsz)], axis=0)
        tok = slice(c * CHUNK, (c + 1) * CHUNK)
        u = u_ref[:, tok, :].reshape(bsz * CHUNK, D_MODEL).astype(F32)
        zs = zs_ref[:, tok, :].reshape(bsz * CHUNK, D_MODEL).astype(F32)
        y = y + dskip_ref[...] * u
        g = jax.nn.gelu(y, approximate=True)
        gate = _sigmoid(jnp.dot(g.astype(BF16), wglu_ref[...], preferred_element_type=F32))
        ys_in = (g * gate * zs).astype(BF16)
        ys = jnp.dot(ys_in, wso_ref[...], preferred_element_type=F32)
        o_ref[:, tok, :] = ys.reshape(bsz, CHUNK, D_MODEL).astype(BF16)


def _ssm_out_call(y2, unat, nat, d_skip, w_glu, w_ssm_out):
    bsz, length, _ = unat.shape
    tl = SSM_CB * CHUNK
    tok = lambda blk: pl.BlockSpec((bsz, tl, D_MODEL), lambda i: (0, i, blk))
    return pl.pallas_call(
        _ssm_out_kernel,
        grid=(length // tl,),
        in_specs=[pl.BlockSpec((SSM_GROUPS, SSM_CB, bsz, CHUNK_W), lambda i: (0, i, 0, 0)),
                  tok(0), tok(NAT_ZS // 2),
                  _full_spec((1, D_MODEL)), _full_spec((D_MODEL, D_MODEL)), _full_spec((D_MODEL, D_MODEL))],
        out_specs=tok(0),
        out_shape=jax.ShapeDtypeStruct((bsz, length, D_MODEL), BF16),
        scratch_shapes=[pltpu.VMEM((SSM_CB, D_MODEL // LANES, bsz * SSM_PITCH, LANES), F32)],
        compiler_params=pltpu.CompilerParams(dimension_semantics=("arbitrary",),
                                             vmem_limit_bytes=VMEM_LIMIT),
        name="ssm_out",
    )(y2, unat, nat, d_skip.astype(F32)[None, :], w_glu.astype(BF16), w_ssm_out.astype(BF16))


def _ssm_kernel(u_ref, st_ref, mt_ref, rtt_ref, coef_ref, y_ref, hloc_ref, hprev_ref):
    nc, bsz = u_ref.shape[1], u_ref.shape[2]
    ns2 = 2 * SSM_STATE
    u = u_ref[0].reshape(nc * bsz, CHUNK_W)
    half = nc * bsz // 2
    hloc_ref[:half, :] = jnp.dot(u[:half], st_ref[0], preferred_element_type=F32)
    hloc_ref[half:, :] = jnp.dot(u[half:], st_ref[0], preferred_element_type=F32)
    a = jnp.broadcast_to(coef_ref[0, 0:1, :], (bsz, ns2))
    b = jnp.broadcast_to(coef_ref[0, 1:2, :], (bsz, ns2))

    def step(c, carry):
        hx, hy = carry
        r = pl.multiple_of(c * bsz, bsz)
        hprev_ref[pl.ds(r, bsz), :] = hx.astype(BF16)
        px = hloc_ref[pl.ds(r, bsz), 0:ns2]
        py = hloc_ref[pl.ds(r, bsz), ns2:2 * ns2]
        return a * hx + b * hy + px, a * hy - b * hx + py

    zero = jnp.zeros((bsz, ns2), F32)
    lax.fori_loop(0, nc, step, (zero, zero), unroll=8)
    y = jnp.dot(u, mt_ref[0], preferred_element_type=F32)
    y = y + lax.dot_general(hprev_ref[...], rtt_ref[0], (((1,), (1,)), ((), ())), preferred_element_type=F32)
    y_ref[0] = y.reshape(nc, bsz, CHUNK_W).astype(BF16)


def _ssm_call(u2, st, mt, rt, coef):
    g, nc, bsz, _ = u2.shape
    gspec = lambda shape: pl.BlockSpec((1,) + shape, lambda i: (i,) + (0,) * len(shape))
    return pl.pallas_call(
        _ssm_kernel,
        grid=(g,),
        in_specs=[gspec((nc, bsz, CHUNK_W)), gspec((CHUNK_W, 4 * SSM_STATE)), gspec((CHUNK_W, CHUNK_W)),
                  gspec((CHUNK_W, 2 * SSM_STATE)), gspec((8, 2 * SSM_STATE))],
        out_specs=gspec((nc, bsz, CHUNK_W)),
        out_shape=jax.ShapeDtypeStruct(u2.shape, BF16),
        scratch_shapes=[pltpu.VMEM((nc * bsz, 4 * SSM_STATE), F32),
                        pltpu.VMEM((nc * bsz, 2 * SSM_STATE), BF16)],
        compiler_params=pltpu.CompilerParams(dimension_semantics=("arbitrary",),
                                             vmem_limit_bytes=VMEM_LIMIT),
        name="ssm_scan",
    )(u2, st, mt, rt, coef)


def _attn_kernel(q_ref, kc_ref, kp_ref, vc_ref, vp_ref, o_ref, lse_ref):
    nres, tq = q_ref.shape[0], q_ref.shape[1]
    qb = 128
    slab = pl.program_id(2)
    qi = lax.broadcasted_iota(jnp.int32, (qb, 2 * qb), 0)
    kk = lax.broadcasted_iota(jnp.int32, (qb, 2 * qb), 1)
    band = jnp.where((kk >= qi) & (kk <= qi + WINDOW_KEYS), 0.0, NEG_INF).astype(F32)
    band0 = band + jnp.where(jnp.logical_and(slab == 0, kk < qb), NEG_INF, 0.0).astype(F32)
    band = jnp.concatenate([band, band], axis=0)
    band0 = jnp.concatenate([band0, band0], axis=0)
    lo_q = lax.broadcasted_iota(jnp.int32, (qb, LANES), 1) < HEAD_DIM
    ones_kv = jnp.ones((2 * qb, LANES), BF16)

    head_slot = lax.broadcasted_iota(jnp.int32, (qb, LANES), 1) // LSE_LANES

    for r, j in [(r, j) for r in range(nres) for j in range(tq // qb)]:
        madd = band0 if j == 0 else band
        rows = slice(j * qb, (j + 1) * qb)
        m_all = jnp.zeros((qb, LANES), F32)
        den_all = jnp.ones((qb, LANES), F32)
        for hp in range(ATTN_W // LANES):
            cs = slice(hp * LANES, (hp + 1) * LANES)
            qp = q_ref[r, rows, cs]
            if j == 0:
                kp = jnp.concatenate([kp_ref[r, :, cs], kc_ref[r, 0:qb, cs]], axis=0)
                vp = jnp.concatenate([vp_ref[r, :, cs], vc_ref[r, 0:qb, cs]], axis=0)
            else:
                kp = kc_ref[r, (j - 1) * qb:(j + 1) * qb, cs]
                vp = vc_ref[r, (j - 1) * qb:(j + 1) * qb, cs]
            zero = jnp.zeros_like(qp)
            q2 = jnp.concatenate([jnp.where(lo_q, qp, zero), jnp.where(lo_q, zero, qp)], axis=0)
            s = lax.dot_general(q2, kp, (((1,), (1,)), ((), ())), preferred_element_type=F32)
            s = s + madd
            m = jnp.max(s, axis=1, keepdims=True)
            p = jnp.exp2(s - m)
            pv = jnp.dot(p.astype(BF16), jnp.concatenate([vp, ones_kv], axis=1), preferred_element_type=F32)
            num = jnp.where(lo_q, pv[:qb, :LANES], pv[qb:, :LANES])
            den = jnp.where(lo_q, pv[:qb, LANES:], pv[qb:, LANES:])
            o_ref[r, rows, cs] = (num / den).astype(BF16)
            for side in range(2):
                slot = head_slot == 2 * hp + side
                half = slice(side * qb, (side + 1) * qb)
                m_all = jnp.where(slot, m[half], m_all)
                den_all = jnp.where(slot, pv[half, LANES:], den_all)
        lse_ref[r, rows, :] = m_all + jnp.log2(den_all)


def _attn_call(qkv, group, blocks, rows_per_step=1024):
    bsz, d, ld, _ = qkv.shape
    tq = min(rows_per_step, ld)
    nres = min(d, rows_per_step // tq)
    nslab = ld // tq
    per = tq // 128
    cur = lambda blk: pl.BlockSpec((None, nres, tq, COL), lambda b, r, i: (b, r, i, blk))
    prev = lambda blk: pl.BlockSpec(
        (None, nres, 128, COL), lambda b, r, i: (b, r, jnp.maximum(i * per - 1, 0), blk))
    ospec = lambda w: pl.BlockSpec((None, nres, tq, w), lambda b, r, i: (b, r, i, 0))
    qb, kb, vb = blocks
    return pl.pallas_call(
        _attn_kernel,
        grid=(bsz, d // nres, nslab),
        in_specs=[cur(qb), cur(kb), prev(kb), cur(vb), prev(vb)],
        out_specs=[ospec(ATTN_W), ospec(LANES)],
        out_shape=[jax.ShapeDtypeStruct((bsz, d, ld, ATTN_W), BF16),
                   jax.ShapeDtypeStruct((bsz, d, ld, LANES), F32)],
        compiler_params=pltpu.CompilerParams(dimension_semantics=("arbitrary",) * 3,
                                             vmem_limit_bytes=VMEM_LIMIT),
        name=f"attn_d{DILATIONS[group]}",
    )(qkv, qkv, qkv, qkv, qkv)


def _merge_kernel(x_ref, ys_ref, za_ref, rs_ref, ra_ref,
                  a0_ref, a1_ref, a2_ref, l0_ref, l1_ref, l2_ref,
                  spread_ref, wao_ref, wo_ref, o_ref, il_ref):
    tm = x_ref.shape[0]

    def natural_order(ref):
        d, nslab = ref.shape[0], ref.shape[2] // LANES
        for r in range(d):
            blk = ref[r].astype(F32)
            for c in range(nslab):
                il_ref[c, pl.ds(r, tm // d, stride=d), :] = blk[:, c * LANES:(c + 1) * LANES]
        return jnp.concatenate([il_ref[c] for c in range(nslab)], axis=1)

    def per_head_to_lanes(w):
        hi = w.astype(BF16)
        lo = (w - hi.astype(F32)).astype(BF16)
        return (jnp.dot(hi, spread_ref[...], preferred_element_type=F32)
                + jnp.dot(lo, spread_ref[...], preferred_element_type=F32))

    l0, l1, l2 = l0_ref[...], natural_order(l1_ref), natural_order(l2_ref)
    lm = jnp.maximum(jnp.maximum(l0, l1), l2)
    e0, e1, e2 = jnp.exp2(l0 - lm), jnp.exp2(l1 - lm), jnp.exp2(l2 - lm)
    inv = 1.0 / (e0 + e1 + e2)
    att = (per_head_to_lanes(e0 * inv) * a0_ref[...].astype(F32)
           + per_head_to_lanes(e1 * inv) * natural_order(a1_ref)
           + per_head_to_lanes(e2 * inv) * natural_order(a2_ref))
    ya_in = (att * za_ref[...].astype(F32)).astype(BF16)
    ya = jnp.dot(ya_in, wao_ref[...], preferred_element_type=F32)

    m = rs_ref[...].astype(F32) * ys_ref[...].astype(F32) + ra_ref[...].astype(F32) * ya
    o_ref[...] = x_ref[...] + jnp.dot(m.astype(BF16), wo_ref[...], preferred_element_type=F32)


def _merge_call(x, ys, nat, attn, w_attn_out, w_o, tm=512):
    bsz, length, _ = x.shape
    wide = lambda blk: pl.BlockSpec((None, tm, D_MODEL), lambda b, i: (b, i, blk))
    half = lambda blk: pl.BlockSpec((None, tm, ATTN_W), lambda b, i: (b, i, blk))
    res = lambda d, w: pl.BlockSpec((None, d, tm // d, w), lambda b, i: (b, 0, i, 0))
    (a0, l0), (a1, l1), (a2, l2) = attn
    d4, d16 = DILATIONS[1], DILATIONS[2]
    a0, l0 = a0.reshape(bsz, length, ATTN_W), l0.reshape(bsz, length, LANES)
    spread = (jnp.arange(LANES)[:, None] == (jnp.arange(ATTN_W)[None, :] // HEAD_DIM) * LSE_LANES).astype(BF16)
    return pl.pallas_call(
        _merge_kernel,
        grid=(bsz, length // tm),
        in_specs=[wide(0), wide(0), half(NAT_ZA), wide(NAT_RS // 2), wide(NAT_RA // 2),
                  half(0), res(d4, ATTN_W), res(d16, ATTN_W),
                  pl.BlockSpec((None, tm, LANES), lambda b, i: (b, i, 0)), res(d4, LANES), res(d16, LANES),
                  _full_spec((LANES, ATTN_W)), _full_spec((ATTN_W, D_MODEL)), _full_spec((D_MODEL, D_MODEL))],
        out_specs=wide(0),
        out_shape=jax.ShapeDtypeStruct((bsz, length, D_MODEL), F32),
        scratch_shapes=[pltpu.VMEM((ATTN_W // LANES, tm, LANES), F32)],
        compiler_params=pltpu.CompilerParams(dimension_semantics=("arbitrary", "arbitrary"),
                                             vmem_limit_bytes=VMEM_LIMIT),
        name="merge",
    )(x, ys, nat, nat, nat, a0, a1, a2, l0, l1, l2, spread, w_attn_out.astype(BF16), w_o.astype(BF16))


def kernel(x, positions, norm_w, w_in, lam_re, lam_im, log_dt, b_re, b_im, c_re, c_im, d_skip, w_glu,
           q_norm_w, k_norm_w, w_ssm_out, w_attn_out, w_o):
    xf = x.astype(F32)
    for layer in range(norm_w.shape[0]):
        st, mt, rt, coef = _ssm_prep_call(lam_re[layer], lam_im[layer], log_dt[layer], b_re[layer],
                                          b_im[layer], c_re[layer], c_im[layer])
        nat, qkv4, qkv16 = _proj_call(xf, positions, norm_w[layer], w_in[layer], q_norm_w[layer],
                                      k_norm_w[layer])
        unat, u2 = _ssm_in_call(xf, norm_w[layer], w_in[layer][:, :D_MODEL])
        y2 = _ssm_call(u2, st, mt, rt, coef)
        ys = _ssm_out_call(y2, unat, nat, d_skip[layer], w_glu[layer], w_ssm_out[layer])
        attn = [_attn_call(nat[:, None], 0, (NAT_Q, NAT_K, NAT_V)),
                _attn_call(qkv4, 1, (0, 1, 2)),
                _attn_call(qkv16, 2, (0, 1, 2))]
        xf = _merge_call(xf, ys, nat, attn, w_attn_out[layer], w_o[layer])
    return xf.astype(x.dtype)
```

```python
import functools
import math

import jax
import jax.numpy as jnp
from jax import lax
from jax.experimental import pallas as pl
from jax.experimental.pallas import tpu as pltpu

F32 = jnp.float32
BF16 = jnp.bfloat16

D_MODEL = 1024
SSM_GROUP = 16
SSM_GROUPS = D_MODEL // SSM_GROUP
SSM_STATE = 64
CHUNK = 16
CHUNK_W = CHUNK * SSM_GROUP
HEAD_DIM = 64
HEADS = 8
ATTN_W = HEADS * HEAD_DIM
DILATIONS = (1, 4, 16)
WINDOW_KEYS = 128
ROPE_DIM = HEAD_DIM // 4
ROPE_THETA = 500000.0
EPS = 1e-6
NEG_INF = -1e30
LOG2E = math.log2(math.e)
COL = 512
U_BLK, ZS_BLK, Q_BLK, K_BLK, V_BLK, ZA_BLK, RS_BLK, RA_BLK = 0, 2, 4, 7, 10, 13, 14, 16
N_BLK = 18
IN_WIDTH = N_BLK * COL
NAT_ZS, NAT_Q, NAT_K, NAT_V, NAT_ZA, NAT_RS, NAT_RA = 0, 2, 3, 4, 5, 6, 8
NAT_W = 10 * COL
QKV_W = 3 * COL
LANES = 128
MXU_DIM = 256
LSE_LANES = LANES // HEADS
LANE_BLKS = LANES // SSM_GROUP
SSM_CB = 4
SSM_PITCH = 8 * 3
VMEM_LIMIT = 56 * 1024 * 1024


def _full_spec(shape):
    nd = len(shape)
    return pl.BlockSpec(shape, lambda *_: (0,) * nd, pipeline_mode=pl.Buffered(1))


def _ssm_prep_kernel(lam_ref, btr_ref, bti_ref, cr_ref, ci_ref, st_ref, mt_ref, rtt_ref, coef_ref):
    ns2 = 2 * SSM_STATE
    lam = lam_ref[0]
    lr, li, dt = lam[0:1, :], lam[1:2, :], jnp.exp(lam[2:3, :])
    mag = jnp.exp(lr * dt)
    ar = mag * jnp.cos(li * dt)
    ai = mag * jnp.sin(li * dt)
    den = lr * lr + li * li
    nr = ar - 1.0
    fr = (nr * lr + ai * li) / den
    fi = (ai * lr - nr * li) / den
    pr, pi = [jnp.ones_like(ar)], [jnp.zeros_like(ar)]
    for _ in range(CHUNK):
        pr, pi = pr + [pr[-1] * ar - pi[-1] * ai], pi + [pr[-1] * ai + pi[-1] * ar]
    by_step = lambda vals: jnp.concatenate([jnp.broadcast_to(v, (SSM_GROUP, ns2)) for v in vals], axis=0)
    per_step = lambda a: jnp.concatenate([a] * CHUNK, axis=0)
    im_part = lax.broadcasted_iota(jnp.int32, (CHUNK_W, ns2), 1) >= SSM_STATE

    btr, bti = per_step(btr_ref[0]), per_step(bti_ref[0])
    bbr = fr * btr - fi * bti
    bbi = fr * bti + fi * btr
    apr = by_step([pr[CHUNK - 1 - s] for s in range(CHUNK)])
    api = by_step([pi[CHUNK - 1 - s] for s in range(CHUNK)])
    wre = apr * bbr - api * bbi
    wim = apr * bbi + api * bbr
    st = jnp.where(im_part, wim, wre)
    st_sw = jnp.where(im_part, wre, wim)
    st_ref[0] = jnp.concatenate([st, st_sw], axis=1).astype(BF16)

    cr, ci = per_step(cr_ref[0]), per_step(ci_ref[0])
    qr = by_step([pr[t + 1] for t in range(CHUNK)])
    qi = by_step([pi[t + 1] for t in range(CHUNK)])
    rtt_ref[0] = jnp.where(im_part, -(cr * qi + ci * qr), cr * qr - ci * qi).astype(BF16)

    cct = jnp.where(im_part, -ci, cr)
    krw = lax.dot_general(st, cct, (((1,), (1,)), ((), ())), preferred_element_type=F32,
                          precision=lax.Precision.HIGHEST)
    lane_blk = lax.broadcasted_iota(jnp.int32, (CHUNK_W, CHUNK_W), 1) // SSM_GROUP
    mt = jnp.zeros((CHUNK_W, CHUNK_W), F32)
    for t in range(CHUNK):
        sh = SSM_GROUP * (CHUNK - 1 - t)
        if sh == 0:
            shifted = krw
        else:
            shifted = jnp.concatenate([krw[sh:, :], jnp.zeros((sh, CHUNK_W), F32)], axis=0)
        mt = jnp.where(lane_blk == t, shifted, mt)
    mt_ref[0] = mt.astype(BF16)

    im_row = lax.broadcasted_iota(jnp.int32, (1, ns2), 1) >= SSM_STATE
    bco = jnp.where(im_row, pi[CHUNK], -pi[CHUNK])
    coef_ref[0] = jnp.concatenate([pr[CHUNK], bco, jnp.zeros((6, ns2), F32)], axis=0)


def _ssm_prep_call(lam_re, lam_im, log_dt, b_re, b_im, c_re, c_im):
    g, n, p = SSM_GROUPS, SSM_STATE, SSM_GROUP
    f = lambda a: a.astype(F32)
    twice = lambda a: jnp.tile(f(a), (1,) * (a.ndim - 1) + (2,))
    ldt_b = jnp.broadcast_to(f(log_dt)[:, None], (g, n))
    lam = jnp.concatenate([twice(lam_re)[:, None], twice(lam_im)[:, None], twice(ldt_b)[:, None],
                           jnp.zeros((g, 5, 2 * n), F32)], axis=1)
    bt = lambda b: twice(b.transpose(0, 2, 1))
    gspec = lambda shape: pl.BlockSpec((1,) + shape, lambda i: (i, 0, 0))
    return pl.pallas_call(
        _ssm_prep_kernel,
        grid=(g,),
        in_specs=[gspec((8, 2 * n))] + [gspec((p, 2 * n))] * 4,
        out_specs=[gspec((CHUNK_W, 4 * n)), gspec((CHUNK_W, CHUNK_W)), gspec((CHUNK_W, 2 * n)),
                   gspec((8, 2 * n))],
        out_shape=[jax.ShapeDtypeStruct((g, CHUNK_W, 4 * n), BF16),
                   jax.ShapeDtypeStruct((g, CHUNK_W, CHUNK_W), BF16),
                   jax.ShapeDtypeStruct((g, CHUNK_W, 2 * n), BF16),
                   jax.ShapeDtypeStruct((g, 8, 2 * n), F32)],
        compiler_params=pltpu.CompilerParams(dimension_semantics=("arbitrary",)),
        name="ssm_prep",
    )(lam, bt(b_re), bt(b_im), twice(c_re), twice(c_im))


def _rope_tables():
    lane = jnp.arange(LANES) % HEAD_DIM
    half = ROPE_DIM // 2
    inv = ROPE_THETA ** (-jnp.arange(0, ROPE_DIM, 2, dtype=F32) / ROPE_DIM)
    expand = ((lane[None, :] < ROPE_DIM) & (lane[None, :] % half == jnp.arange(half)[:, None])).astype(F32)
    s_lo = jnp.where(lane < half, -1.0, 0.0)
    s_hi = jnp.where((lane >= half) & (lane < ROPE_DIM), 1.0, 0.0)
    unrot = jnp.where(lane >= ROPE_DIM, 1.0, 0.0)
    rows = jnp.concatenate([s_lo[None], s_hi[None], unrot[None], jnp.zeros((5, LANES), F32)], axis=0)
    expand = jnp.concatenate([expand, expand], axis=0).astype(BF16)
    return inv[:, None].astype(F32), expand, rows.astype(F32)


def _sigmoid(v):
    return 1.0 / (1.0 + jnp.exp(-v))


_NAT_DST = {2: NAT_ZS, 3: NAT_ZS + 1, Q_BLK: NAT_Q, K_BLK: NAT_K, V_BLK: NAT_V,
            ZA_BLK: NAT_ZA, 14: NAT_RS, 15: NAT_RS + 1, 16: NAT_RA, 17: NAT_RA + 1}


_PROJ_ORDER = (4, 2, 7, 3, 5, 13, 8, 14, 6, 15, 9, 16, 11, 17, 12, 10)
_N_PERM = 6


def _proj_kernel(x_ref, pos_ref, nw_ref, w_ref, qkw_ref, ones_ref, freq_ref, expand_ref, rope_ref,
                 nat_ref, d4_ref, d16_ref, perm_ref, h_ref):
    tm = x_ref.shape[0]
    x = x_ref[...]
    ms = jnp.mean(x * x, axis=-1, keepdims=True)
    h_ref[...] = (x * lax.rsqrt(ms + EPS) * nw_ref[...]).astype(BF16)
    ang = freq_ref[...] * pos_ref[...].astype(F32)

    def spread(v):
        hi = v.astype(BF16)
        lo = (v - hi.astype(F32)).astype(BF16)
        return lax.dot_general(jnp.concatenate([hi, lo], axis=0), expand_ref[...], (((0,), (0,)), ((), ())),
                               preferred_element_type=F32)

    cosv = spread(jnp.cos(ang)) + rope_ref[2:3, :]
    sinv = spread(jnp.sin(ang))
    s_lo = sinv * rope_ref[0:1, :]
    s_hi = sinv * rope_ref[1:2, :]
    slot = 0
    for j in _PROJ_ORDER:
        acc = jnp.dot(h_ref[...], w_ref[:, j * COL:(j + 1) * COL], preferred_element_type=F32)
        if ZS_BLK <= j < Q_BLK or j == ZA_BLK:
            res = acc * _sigmoid(acc)
        elif j >= RS_BLK:
            res = _sigmoid(acc)
        elif Q_BLK <= j < V_BLK:
            sq = (acc * acc).astype(BF16)
            hw = ones_ref.shape[0]
            ss = jnp.concatenate([jnp.dot(sq[:, c * hw:(c + 1) * hw], ones_ref[...], preferred_element_type=F32)
                                  for c in range(COL // hw)], axis=1)
            y = acc * lax.rsqrt(ss * (1.0 / HEAD_DIM) + EPS) * qkw_ref[j - Q_BLK:j - Q_BLK + 1, :]
            parts = []
            for c in range(COL // LANES):
                yc = y[:, c * LANES:(c + 1) * LANES]
                parts.append(yc * cosv + pltpu.roll(yc, LANES - ROPE_DIM // 2, 1) * s_lo
                             + pltpu.roll(yc, ROPE_DIM // 2, 1) * s_hi)
            res = jnp.concatenate(parts, axis=1)
        else:
            res = acc
        if j in _NAT_DST:
            dst = _NAT_DST[j]
            nat_ref[:, dst * COL:(dst + 1) * COL] = res.astype(BF16)
        else:
            kind, group = divmod(j - Q_BLK, len(DILATIONS))
            d = DILATIONS[group]
            out_ref = d4_ref if group == 1 else d16_ref
            for c in range(COL // LANES):
                perm_ref[slot, c] = res[:, c * LANES:(c + 1) * LANES]
            for r in range(d):
                rows = [perm_ref[slot, c, pl.ds(r, tm // d, stride=d), :] for c in range(COL // LANES)]
                out_ref[r, :, kind * COL:(kind + 1) * COL] = jnp.concatenate(rows, axis=1).astype(BF16)
            slot += 1


def _proj_call(x, positions, norm_w, w_in, q_norm_w, k_norm_w, tm=512):
    bsz, length, _ = x.shape
    scale = LOG2E / math.sqrt(HEAD_DIM)
    qkw = jnp.concatenate([jnp.tile(q_norm_w.astype(F32) * scale, (1, HEADS)),
                           jnp.tile(k_norm_w.astype(F32), (1, HEADS)),
                           jnp.zeros((2, ATTN_W), F32)], axis=0)
    hid = jnp.arange(MXU_DIM) // HEAD_DIM
    ones = (hid[:, None] == hid[None, :]).astype(BF16)
    d4, d16 = DILATIONS[1], DILATIONS[2]
    freq, expand, rope_rows = _rope_tables()
    return pl.pallas_call(
        _proj_kernel,
        grid=(bsz, length // tm),
        in_specs=[pl.BlockSpec((None, tm, D_MODEL), lambda b, i: (b, i, 0)),
                  pl.BlockSpec((None, 1, tm), lambda b, i: (b, 0, i)),
                  _full_spec((1, D_MODEL)),
                  _full_spec((D_MODEL, IN_WIDTH)),
                  _full_spec((8, ATTN_W)),
                  _full_spec((MXU_DIM, MXU_DIM)),
                  _full_spec((ROPE_DIM // 2, 1)),
                  _full_spec((ROPE_DIM, LANES)),
                  _full_spec((8, LANES))],
        out_specs=[pl.BlockSpec((None, tm, NAT_W), lambda b, i: (b, i, 0)),
                   pl.BlockSpec((None, d4, tm // d4, QKV_W), lambda b, i: (b, 0, i, 0)),
                   pl.BlockSpec((None, d16, tm // d16, QKV_W), lambda b, i: (b, 0, i, 0))],
        out_shape=[jax.ShapeDtypeStruct((bsz, length, NAT_W), BF16),
                   jax.ShapeDtypeStruct((bsz, d4, length // d4, QKV_W), BF16),
                   jax.ShapeDtypeStruct((bsz, d16, length // d16, QKV_W), BF16)],
        scratch_shapes=[pltpu.VMEM((_N_PERM, COL // LANES, tm, LANES), F32), pltpu.VMEM((tm, D_MODEL), BF16)],
        compiler_params=pltpu.CompilerParams(dimension_semantics=("arbitrary", "arbitrary"),
                                             vmem_limit_bytes=VMEM_LIMIT),
        name="proj",
    )(x.astype(F32), positions.astype(jnp.int32)[:, None, :], norm_w.astype(F32)[None, :],
      w_in.astype(BF16), qkw, ones, freq, expand, rope_rows)


def _block_transpose(xs):
    xs = list(xs)
    blk = lax.broadcasted_iota(jnp.int32, xs[0].shape, 1) // SSM_GROUP
    dist = 1
    while dist < LANE_BLKS:
        upper = (blk & dist) != 0
        shift = dist * SSM_GROUP
        for i in range(LANE_BLKS):
            if i & dist:
                continue
            a, b = xs[i], xs[i + dist]
            xs[i] = jnp.where(upper, pltpu.roll(b, shift, 1), a)
            xs[i + dist] = jnp.where(upper, b, pltpu.roll(a, LANES - shift, 1))
        dist *= 2
    return xs


def _ssm_in_kernel(x_ref, nw_ref, w_ref, unat_ref, u2_ref, rows_ref):
    bsz, tl = x_ref.shape[0], x_ref.shape[1]
    for c in range(tl // CHUNK):
        x = x_ref[:, c * CHUNK:(c + 1) * CHUNK, :].reshape(bsz * CHUNK, D_MODEL)
        ms = jnp.mean(x * x, axis=-1, keepdims=True)
        h = (x * lax.rsqrt(ms + EPS) * nw_ref[...]).astype(BF16)
        u = jnp.dot(h, w_ref[...], preferred_element_type=F32)
        unat_ref[:, c * CHUNK:(c + 1) * CHUNK, :] = u.reshape(bsz, CHUNK, D_MODEL).astype(BF16)
        for b in range(bsz):
            for s8 in range(D_MODEL // LANES):
                rows_ref[c, s8, b * SSM_PITCH:b * SSM_PITCH + CHUNK, :] = (
                    u[b * CHUNK:(b + 1) * CHUNK, s8 * LANES:(s8 + 1) * LANES])
        for s8 in range(D_MODEL // LANES):
            for half in range(CHUNK // LANE_BLKS):
                xs = []
                for k in range(LANE_BLKS):
                    v = rows_ref[c, s8, pl.ds(half * LANE_BLKS + k, bsz, stride=SSM_PITCH), :].astype(BF16)
                    xs.append(pltpu.bitcast(v, jnp.uint32))
                ys = _block_transpose(xs)
                for g in range(LANE_BLKS):
                    u2_ref[s8 * LANE_BLKS + g, c, :, half * LANES:(half + 1) * LANES] = pltpu.bitcast(ys[g], BF16)


def _ssm_in_call(x, norm_w, w_u):
    bsz, length, _ = x.shape
    tl = SSM_CB * CHUNK
    nc = length // CHUNK
    return pl.pallas_call(
        _ssm_in_kernel,
        grid=(length // tl,),
        in_specs=[pl.BlockSpec((bsz, tl, D_MODEL), lambda i: (0, i, 0)),
                  _full_spec((1, D_MODEL)), _full_spec((D_MODEL, D_MODEL))],
        out_specs=[pl.BlockSpec((bsz, tl, D_MODEL), lambda i: (0, i, 0)),
                   pl.BlockSpec((SSM_GROUPS, SSM_CB, bsz, CHUNK_W), lambda i: (0, i, 0, 0))],
        out_shape=[jax.ShapeDtypeStruct((bsz, length, D_MODEL), BF16),
                   jax.ShapeDtypeStruct((SSM_GROUPS, nc, bsz, CHUNK_W), BF16)],
        scratch_shapes=[pltpu.VMEM((SSM_CB, D_MODEL // LANES, bsz * SSM_PITCH, LANES), F32)],
        compiler_params=pltpu.CompilerParams(dimension_semantics=("arbitrary",),
                                             vmem_limit_bytes=VMEM_LIMIT),
        name="ssm_in",
    )(x, norm_w.astype(F32)[None, :], w_u.astype(BF16))


def _ssm_out_kernel(y2_ref, u_ref, zs_ref, dskip_ref, wglu_ref, wso_ref, o_ref, rows_ref):
    bsz, tl = u_ref.shape[0], u_ref.shape[1]
    for c in range(tl // CHUNK):
        for s8 in range(D_MODEL // LANES):
            for half in range(CHUNK // LANE_BLKS):
                xs = [pltpu.bitcast(y2_ref[s8 * LANE_BLKS + g, c, :, half * LANES:(half + 1) * LANES], jnp.uint32)
                      for g in range(LANE_BLKS)]
                ys = _block_transpose(xs)
                for k in range(LANE_BLKS):
                    rows_ref[c, s8, pl.ds(half * LANE_BLKS + k, bsz, stride=SSM_PITCH), :] = (
                        pltpu.bitcast(ys[k], BF16).astype(F32))
        y = jnp.concatenate(
            [jnp.concatenate([rows_ref[c, s8, b * SSM_PITCH:b * SSM_PITCH + CHUNK, :]
                              for s8 in range(D_MODEL // LANES)], axis=1)
             for b in range(bsz)], axis=0)
        tok = slice(c * CHUNK, (c + 1) * CHUNK)
        u = u_ref[:, tok, :].reshape(bsz * CHUNK, D_MODEL).astype(F32)
        zs = zs_ref[:, tok, :].reshape(bsz * CHUNK, D_MODEL).astype(F32)
        y = y + dskip_ref[...] * u
        g = jax.nn.gelu(y, approximate=True)
        gate = _sigmoid(jnp.dot(g.astype(BF16), wglu_ref[...], preferred_element_type=F32))
        ys_in = (g * gate * zs).astype(BF16)
        ys = jnp.dot(ys_in, wso_ref[...], preferred_element_type=F32)
        o_ref[:, tok, :] = ys.reshape(bsz, CHUNK, D_MODEL).astype(BF16)


def _ssm_out_call(y2, unat, nat, d_skip, w_glu, w_ssm_out):
    bsz, length, _ = unat.shape
    tl = SSM_CB * CHUNK
    tok = lambda blk: pl.BlockSpec((bsz, tl, D_MODEL), lambda i: (0, i, blk))
    return pl.pallas_call(
        _ssm_out_kernel,
        grid=(length // tl,),
        in_specs=[pl.BlockSpec((SSM_GROUPS, SSM_CB, bsz, CHUNK_W), lambda i: (0, i, 0, 0)),
                  tok(0), tok(NAT_ZS // 2),
                  _full_spec((1, D_MODEL)), _full_spec((D_MODEL, D_MODEL)), _full_spec((D_MODEL, D_MODEL))],
        out_specs=tok(0),
        out_shape=jax.ShapeDtypeStruct((bsz, length, D_MODEL), BF16),
        scratch_shapes=[pltpu.VMEM((SSM_CB, D_MODEL // LANES, bsz * SSM_PITCH, LANES), F32)],
        compiler_params=pltpu.CompilerParams(dimension_semantics=("arbitrary",),
                                             vmem_limit_bytes=VMEM_LIMIT),
        name="ssm_out",
    )(y2, unat, nat, d_skip.astype(F32)[None, :], w_glu.astype(BF16), w_ssm_out.astype(BF16))


def _ssm_kernel(u_ref, st_ref, mt_ref, rtt_ref, coef_ref, y_ref, hloc_ref, hprev_ref):
    nc, bsz = u_ref.shape[1], u_ref.shape[2]
    ns2 = 2 * SSM_STATE
    u = u_ref[0].reshape(nc * bsz, CHUNK_W)
    half = nc * bsz // 2
    hloc_ref[:half, :] = jnp.dot(u[:half], st_ref[0], preferred_element_type=F32)
    hloc_ref[half:, :] = jnp.dot(u[half:], st_ref[0], preferred_element_type=F32)
    a = jnp.broadcast_to(coef_ref[0, 0:1, :], (bsz, ns2))
    b = jnp.broadcast_to(coef_ref[0, 1:2, :], (bsz, ns2))

    def step(c, carry):
        hx, hy = carry
        r = pl.multiple_of(c * bsz, bsz)
        hprev_ref[pl.ds(r, bsz), :] = hx.astype(BF16)
        px = hloc_ref[pl.ds(r, bsz), 0:ns2]
        py = hloc_ref[pl.ds(r, bsz), ns2:2 * ns2]
        return a * hx + b * hy + px, a * hy - b * hx + py

    zero = jnp.zeros((bsz, ns2), F32)
    lax.fori_loop(0, nc, step, (zero, zero), unroll=8)
    y = jnp.dot(u, mt_ref[0], preferred_element_type=F32)
    y = y + lax.dot_general(hprev_ref[...], rtt_ref[0], (((1,), (1,)), ((), ())), preferred_element_type=F32)
    y_ref[0] = y.reshape(nc, bsz, CHUNK_W).astype(BF16)


def _ssm_call(u2, st, mt, rt, coef):
    g, nc, bsz, _ = u2.shape
    gspec = lambda shape: pl.BlockSpec((1,) + shape, lambda i: (i,) + (0,) * len(shape))
    return pl.pallas_call(
        _ssm_kernel,
        grid=(g,),
        in_specs=[gspec((nc, bsz, CHUNK_W)), gspec((CHUNK_W, 4 * SSM_STATE)), gspec((CHUNK_W, CHUNK_W)),
                  gspec((CHUNK_W, 2 * SSM_STATE)), gspec((8, 2 * SSM_STATE))],
        out_specs=gspec((nc, bsz, CHUNK_W)),
        out_shape=jax.ShapeDtypeStruct(u2.shape, BF16),
        scratch_shapes=[pltpu.VMEM((nc * bsz, 4 * SSM_STATE), F32),
                        pltpu.VMEM((nc * bsz, 2 * SSM_STATE), BF16)],
        compiler_params=pltpu.CompilerParams(dimension_semantics=("arbitrary",),
                                             vmem_limit_bytes=VMEM_LIMIT),
        name="ssm_scan",
    )(u2, st, mt, rt, coef)


def _attn_kernel(q_ref, kc_ref, kp_ref, vc_ref, vp_ref, o_ref, lse_ref):
    nres, tq = q_ref.shape[0], q_ref.shape[1]
    qb = 128
    slab = pl.program_id(2)
    qi = lax.broadcasted_iota(jnp.int32, (qb, 2 * qb), 0)
    kk = lax.broadcasted_iota(jnp.int32, (qb, 2 * qb), 1)
    band = jnp.where((kk >= qi) & (kk <= qi + WINDOW_KEYS), 0.0, NEG_INF).astype(F32)
    band0 = band + jnp.where(jnp.logical_and(slab == 0, kk < qb), NEG_INF, 0.0).astype(F32)
    band = jnp.concatenate([band, band], axis=0)
    band0 = jnp.concatenate([band0, band0], axis=0)
    lo_q = lax.broadcasted_iota(jnp.int32, (qb, LANES), 1) < HEAD_DIM
    ones_kv = jnp.ones((2 * qb, LANES), BF16)

    head_slot = lax.broadcasted_iota(jnp.int32, (qb, LANES), 1) // LSE_LANES

    for r, j in [(r, j) for r in range(nres) for j in range(tq // qb)]:
        madd = band0 if j == 0 else band
        rows = slice(j * qb, (j + 1) * qb)
        m_all = jnp.zeros((qb, LANES), F32)
        den_all = jnp.ones((qb, LANES), F32)
        for hp in range(ATTN_W // LANES):
            cs = slice(hp * LANES, (hp + 1) * LANES)
            qp = q_ref[r, rows, cs]
            if j == 0:
                kp = jnp.concatenate([kp_ref[r, :, cs], kc_ref[r, 0:qb, cs]], axis=0)
                vp = jnp.concatenate([vp_ref[r, :, cs], vc_ref[r, 0:qb, cs]], axis=0)
            else:
                kp = kc_ref[r, (j - 1) * qb:(j + 1) * qb, cs]
                vp = vc_ref[r, (j - 1) * qb:(j + 1) * qb, cs]
            zero = jnp.zeros_like(qp)
            q2 = jnp.concatenate([jnp.where(lo_q, qp, zero), jnp.where(lo_q, zero, qp)], axis=0)
            s = lax.dot_general(q2, kp, (((1,), (1,)), ((), ())), preferred_element_type=F32)
            s = s + madd
            m = jnp.max(s, axis=1, keepdims=True)
            p = jnp.exp2(s - m)
            pv = jnp.dot(p.astype(BF16), jnp.concatenate([vp, ones_kv], axis=1), preferred_element_type=F32)
            num = jnp.where(lo_q, pv[:qb, :LANES], pv[qb:, :LANES])
            den = jnp.where(lo_q, pv[:qb, LANES:], pv[qb:, LANES:])
            o_ref[r, rows, cs] = (num / den).astype(BF16)
            for side in range(2):
                slot = head_slot == 2 * hp + side
                half = slice(side * qb, (side + 1) * qb)
                m_all = jnp.where(slot, m[half], m_all)
                den_all = jnp.where(slot, pv[half, LANES:], den_all)
        lse_ref[r, rows, :] = m_all + jnp.log2(den_all)


def _attn_call(qkv, group, blocks, rows_per_step=1024):
    bsz, d, ld, _ = qkv.shape
    tq = min(rows_per_step, ld)
    nres = min(d, rows_per_step // tq)
    nslab = ld // tq
    per = tq // 128
    cur = lambda blk: pl.BlockSpec((None, nres, tq, COL), lambda b, r, i: (b, r, i, blk))
    prev = lambda blk: pl.BlockSpec(
        (None, nres, 128, COL), lambda b, r, i: (b, r, jnp.maximum(i * per - 1, 0), blk))
    ospec = lambda w: pl.BlockSpec((None, nres, tq, w), lambda b, r, i: (b, r, i, 0))
    qb, kb, vb = blocks
    return pl.pallas_call(
        _attn_kernel,
        grid=(bsz, d // nres, nslab),
        in_specs=[cur(qb), cur(kb), prev(kb), cur(vb), prev(vb)],
        out_specs=[ospec(ATTN_W), ospec(LANES)],
        out_shape=[jax.ShapeDtypeStruct((bsz, d, ld, ATTN_W), BF16),
                   jax.ShapeDtypeStruct((bsz, d, ld, LANES), F32)],
        compiler_params=pltpu.CompilerParams(dimension_semantics=("arbitrary",) * 3,
                                             vmem_limit_bytes=VMEM_LIMIT),
        name=f"attn_d{DILATIONS[group]}",
    )(qkv, qkv, qkv, qkv, qkv)


def _merge_kernel(x_ref, ys_ref, za_ref, rs_ref, ra_ref,
                  a0_ref, a1_ref, a2_ref, l0_ref, l1_ref, l2_ref,
                  spread_ref, wao_ref, wo_ref, o_ref, il_ref):
    tm = x_ref.shape[0]

    def to_token_order(ref, base):
        d, nslab = ref.shape[0], ref.shape[2] // LANES
        for r in range(d):
            blk = ref[r].astype(F32)
            for c in range(nslab):
                il_ref[base + c, pl.ds(r, tm // d, stride=d), :] = blk[:, c * LANES:(c + 1) * LANES]
        return lambda rows: jnp.concatenate([il_ref[base + c, rows, :] for c in range(nslab)], axis=1)

    def per_head_to_lanes(w):
        hi = w.astype(BF16)
        lo = (w - hi.astype(F32)).astype(BF16)
        return jnp.dot(jnp.concatenate([hi, lo], axis=1), spread_ref[...], preferred_element_type=F32)

    wide_slabs = ATTN_W // LANES
    l1_at, l2_at = to_token_order(l1_ref, 0), to_token_order(l2_ref, 1)
    a1_at, a2_at = to_token_order(a1_ref, 2), to_token_order(a2_ref, 2 + wide_slabs)
    rows = slice(None)
    l0, l1, l2 = l0_ref[...], l1_at(rows), l2_at(rows)
    lm = jnp.maximum(jnp.maximum(l0, l1), l2)
    e0, e1, e2 = jnp.exp2(l0 - lm), jnp.exp2(l1 - lm), jnp.exp2(l2 - lm)
    inv = 1.0 / (e0 + e1 + e2)
    att = (per_head_to_lanes(e0 * inv) * a0_ref[...].astype(F32)
           + per_head_to_lanes(e1 * inv) * a1_at(rows)
           + per_head_to_lanes(e2 * inv) * a2_at(rows))
    ya_in = (att * za_ref[...].astype(F32)).astype(BF16)
    ya = jnp.dot(ya_in, wao_ref[...], preferred_element_type=F32)
    m = rs_ref[...].astype(F32) * ys_ref[...].astype(F32) + ra_ref[...].astype(F32) * ya
    o_ref[...] = x_ref[...] + jnp.dot(m.astype(BF16), wo_ref[...], preferred_element_type=F32)


def _merge_call(x, ys, nat, attn, w_attn_out, w_o, tm=512):
    bsz, length, _ = x.shape
    wide = lambda blk: pl.BlockSpec((None, tm, D_MODEL), lambda b, i: (b, i, blk))
    half = lambda blk: pl.BlockSpec((None, tm, ATTN_W), lambda b, i: (b, i, blk))
    res = lambda d, w: pl.BlockSpec((None, d, tm // d, w), lambda b, i: (b, 0, i, 0))
    (a0, l0), (a1, l1), (a2, l2) = attn
    d4, d16 = DILATIONS[1], DILATIONS[2]
    a0, l0 = a0.reshape(bsz, length, ATTN_W), l0.reshape(bsz, length, LANES)
    spread = (jnp.arange(LANES)[:, None] == (jnp.arange(ATTN_W)[None, :] // HEAD_DIM) * LSE_LANES).astype(BF16)
    spread = jnp.concatenate([spread, spread], axis=0)
    return pl.pallas_call(
        _merge_kernel,
        grid=(bsz, length // tm),
        in_specs=[wide(0), wide(0), half(NAT_ZA), wide(NAT_RS // 2), wide(NAT_RA // 2),
                  half(0), res(d4, ATTN_W), res(d16, ATTN_W),
                  pl.BlockSpec((None, tm, LANES), lambda b, i: (b, i, 0)), res(d4, LANES), res(d16, LANES),
                  _full_spec((2 * LANES, ATTN_W)), _full_spec((ATTN_W, D_MODEL)), _full_spec((D_MODEL, D_MODEL))],
        out_specs=wide(0),
        out_shape=jax.ShapeDtypeStruct((bsz, length, D_MODEL), F32),
        scratch_shapes=[pltpu.VMEM((2 + 2 * (ATTN_W // LANES), tm, LANES), F32)],
        compiler_params=pltpu.CompilerParams(dimension_semantics=("arbitrary", "arbitrary"),
                                             vmem_limit_bytes=VMEM_LIMIT),
        name="merge",
    )(x, ys, nat, nat, nat, a0, a1, a2, l0, l1, l2, spread, w_attn_out.astype(BF16), w_o.astype(BF16))


def kernel(x, positions, norm_w, w_in, lam_re, lam_im, log_dt, b_re, b_im, c_re, c_im, d_skip, w_glu,
           q_norm_w, k_norm_w, w_ssm_out, w_attn_out, w_o):
    xf = x.astype(F32)
    for layer in range(norm_w.shape[0]):
        st, mt, rt, coef = _ssm_prep_call(lam_re[layer], lam_im[layer], log_dt[layer], b_re[layer],
                                          b_im[layer], c_re[layer], c_im[layer])
        nat, qkv4, qkv16 = _proj_call(xf, positions, norm_w[layer], w_in[layer], q_norm_w[layer],
                                      k_norm_w[layer])
        unat, u2 = _ssm_in_call(xf, norm_w[layer], w_in[layer][:, :D_MODEL])
        y2 = _ssm_call(u2, st, mt, rt, coef)
        ys = _ssm_out_call(y2, unat, nat, d_skip[layer], w_glu[layer], w_ssm_out[layer])
        attn = [_attn_call(nat[:, None], 0, (NAT_Q, NAT_K, NAT_V)),
                _attn_call(qkv4, 1, (0, 1, 2)),
                _attn_call(qkv16, 2, (0, 1, 2))]
        xf = _merge_call(xf, ys, nat, attn, w_attn_out[layer], w_o[layer])
    return xf.astype(x.dtype)
```

```python
import functools
import math

import jax
import jax.numpy as jnp
from jax import lax
from jax.experimental import pallas as pl
from jax.experimental.pallas import tpu as pltpu

F32 = jnp.float32
BF16 = jnp.bfloat16

D_MODEL = 1024
SSM_GROUP = 16
SSM_GROUPS = D_MODEL // SSM_GROUP
SSM_STATE = 64
CHUNK = 16
CHUNK_W = CHUNK * SSM_GROUP
HEAD_DIM = 64
HEADS = 8
ATTN_W = HEADS * HEAD_DIM
DILATIONS = (1, 4, 16)
WINDOW_KEYS = 128
ROPE_DIM = HEAD_DIM // 4
ROPE_THETA = 500000.0
EPS = 1e-6
NEG_INF = -1e30
LOG2E = math.log2(math.e)
COL = 512
U_BLK, ZS_BLK, Q_BLK, K_BLK, V_BLK, ZA_BLK, RS_BLK, RA_BLK = 0, 2, 4, 7, 10, 13, 14, 16
N_BLK = 18
IN_WIDTH = N_BLK * COL
NAT_ZS, NAT_Q, NAT_K, NAT_V, NAT_ZA, NAT_RS, NAT_RA = 0, 2, 3, 4, 5, 6, 8
NAT_W = 10 * COL
QKV_W = 3 * COL
LANES = 128
MXU_DIM = 256
LSE_LANES = LANES // HEADS
LANE_BLKS = LANES // SSM_GROUP
SCAN_GROUPS = 2
SSM_CB = 4
SSM_PITCH = 8 * 3
VMEM_LIMIT = 56 * 1024 * 1024


def _full_spec(shape):
    nd = len(shape)
    return pl.BlockSpec(shape, lambda *_: (0,) * nd, pipeline_mode=pl.Buffered(1))


def _ssm_prep_kernel(lam_ref, btr_ref, bti_ref, cr_ref, ci_ref, st_ref, mt_ref, rtt_ref, coef_ref):
    ns2 = 2 * SSM_STATE
    lam = lam_ref[0]
    lr, li, dt = lam[0:1, :], lam[1:2, :], jnp.exp(lam[2:3, :])
    mag = jnp.exp(lr * dt)
    ar = mag * jnp.cos(li * dt)
    ai = mag * jnp.sin(li * dt)
    den = lr * lr + li * li
    nr = ar - 1.0
    fr = (nr * lr + ai * li) / den
    fi = (ai * lr - nr * li) / den
    pr, pi = [jnp.ones_like(ar)], [jnp.zeros_like(ar)]
    for _ in range(CHUNK):
        pr, pi = pr + [pr[-1] * ar - pi[-1] * ai], pi + [pr[-1] * ai + pi[-1] * ar]
    by_step = lambda vals: jnp.concatenate([jnp.broadcast_to(v, (SSM_GROUP, ns2)) for v in vals], axis=0)
    per_step = lambda a: jnp.concatenate([a] * CHUNK, axis=0)
    im_part = lax.broadcasted_iota(jnp.int32, (CHUNK_W, ns2), 1) >= SSM_STATE

    btr, bti = per_step(btr_ref[0]), per_step(bti_ref[0])
    bbr = fr * btr - fi * bti
    bbi = fr * bti + fi * btr
    apr = by_step([pr[CHUNK - 1 - s] for s in range(CHUNK)])
    api = by_step([pi[CHUNK - 1 - s] for s in range(CHUNK)])
    wre = apr * bbr - api * bbi
    wim = apr * bbi + api * bbr
    st = jnp.where(im_part, wim, wre)
    st_sw = jnp.where(im_part, wre, wim)
    st_ref[0] = jnp.concatenate([st, st_sw], axis=1).astype(BF16)

    cr, ci = per_step(cr_ref[0]), per_step(ci_ref[0])
    qr = by_step([pr[t + 1] for t in range(CHUNK)])
    qi = by_step([pi[t + 1] for t in range(CHUNK)])
    rtt_ref[0] = jnp.where(im_part, -(cr * qi + ci * qr), cr * qr - ci * qi).astype(BF16)

    cct = jnp.where(im_part, -ci, cr)
    krw = lax.dot_general(st, cct, (((1,), (1,)), ((), ())), preferred_element_type=F32,
                          precision=lax.Precision.HIGHEST)
    lane_blk = lax.broadcasted_iota(jnp.int32, (CHUNK_W, CHUNK_W), 1) // SSM_GROUP
    mt = jnp.zeros((CHUNK_W, CHUNK_W), F32)
    for t in range(CHUNK):
        sh = SSM_GROUP * (CHUNK - 1 - t)
        if sh == 0:
            shifted = krw
        else:
            shifted = jnp.concatenate([krw[sh:, :], jnp.zeros((sh, CHUNK_W), F32)], axis=0)
        mt = jnp.where(lane_blk == t, shifted, mt)
    mt_ref[0] = mt.astype(BF16)

    im_row = lax.broadcasted_iota(jnp.int32, (1, ns2), 1) >= SSM_STATE
    bco = jnp.where(im_row, pi[CHUNK], -pi[CHUNK])
    coef_ref[0] = jnp.concatenate([pr[CHUNK], bco, jnp.zeros((6, ns2), F32)], axis=0)


def _ssm_prep_call(lam_re, lam_im, log_dt, b_re, b_im, c_re, c_im):
    g, n, p = SSM_GROUPS, SSM_STATE, SSM_GROUP
    f = lambda a: a.astype(F32)
    twice = lambda a: jnp.tile(f(a), (1,) * (a.ndim - 1) + (2,))
    ldt_b = jnp.broadcast_to(f(log_dt)[:, None], (g, n))
    lam = jnp.concatenate([twice(lam_re)[:, None], twice(lam_im)[:, None], twice(ldt_b)[:, None],
                           jnp.zeros((g, 5, 2 * n), F32)], axis=1)
    bt = lambda b: twice(b.transpose(0, 2, 1))
    gspec = lambda shape: pl.BlockSpec((1,) + shape, lambda i: (i, 0, 0))
    return pl.pallas_call(
        _ssm_prep_kernel,
        grid=(g,),
        in_specs=[gspec((8, 2 * n))] + [gspec((p, 2 * n))] * 4,
        out_specs=[gspec((CHUNK_W, 4 * n)), gspec((CHUNK_W, CHUNK_W)), gspec((CHUNK_W, 2 * n)),
                   gspec((8, 2 * n))],
        out_shape=[jax.ShapeDtypeStruct((g, CHUNK_W, 4 * n), BF16),
                   jax.ShapeDtypeStruct((g, CHUNK_W, CHUNK_W), BF16),
                   jax.ShapeDtypeStruct((g, CHUNK_W, 2 * n), BF16),
                   jax.ShapeDtypeStruct((g, 8, 2 * n), F32)],
        compiler_params=pltpu.CompilerParams(dimension_semantics=("arbitrary",)),
        name="ssm_prep",
    )(lam, bt(b_re), bt(b_im), twice(c_re), twice(c_im))


def _rope_tables():
    lane = jnp.arange(LANES) % HEAD_DIM
    half = ROPE_DIM // 2
    inv = ROPE_THETA ** (-jnp.arange(0, ROPE_DIM, 2, dtype=F32) / ROPE_DIM)
    expand = ((lane[None, :] < ROPE_DIM) & (lane[None, :] % half == jnp.arange(half)[:, None])).astype(F32)
    s_lo = jnp.where(lane < half, -1.0, 0.0)
    s_hi = jnp.where((lane >= half) & (lane < ROPE_DIM), 1.0, 0.0)
    unrot = jnp.where(lane >= ROPE_DIM, 1.0, 0.0)
    rows = jnp.concatenate([s_lo[None], s_hi[None], unrot[None], jnp.zeros((5, LANES), F32)], axis=0)
    expand = jnp.concatenate([expand, expand], axis=0).astype(BF16)
    return inv[:, None].astype(F32), expand, rows.astype(F32)


def _sigmoid(v):
    return 1.0 / (1.0 + jnp.exp(-v))


_NAT_DST = {2: NAT_ZS, 3: NAT_ZS + 1, Q_BLK: NAT_Q, K_BLK: NAT_K, V_BLK: NAT_V,
            ZA_BLK: NAT_ZA, 14: NAT_RS, 15: NAT_RS + 1, 16: NAT_RA, 17: NAT_RA + 1}


_PROJ_ORDER = (4, 2, 7, 3, 5, 13, 8, 14, 6, 15, 9, 16, 11, 17, 12, 10)
_N_PERM = 6


def _proj_kernel(x_ref, pos_ref, nw_ref, w_ref, qkw_ref, ones_ref, freq_ref, expand_ref, rope_ref,
                 nat_ref, d4_ref, d16_ref, perm_ref, h_ref):
    tm = x_ref.shape[0]
    x = x_ref[...]
    ms = jnp.mean(x * x, axis=-1, keepdims=True)
    h_ref[...] = (x * lax.rsqrt(ms + EPS) * nw_ref[...]).astype(BF16)
    ang = freq_ref[...] * pos_ref[...].astype(F32)

    def spread(v):
        hi = v.astype(BF16)
        lo = (v - hi.astype(F32)).astype(BF16)
        return lax.dot_general(jnp.concatenate([hi, lo], axis=0), expand_ref[...], (((0,), (0,)), ((), ())),
                               preferred_element_type=F32)

    cosv = spread(jnp.cos(ang)) + rope_ref[2:3, :]
    sinv = spread(jnp.sin(ang))
    s_lo = sinv * rope_ref[0:1, :]
    s_hi = sinv * rope_ref[1:2, :]
    slot = 0
    for j in _PROJ_ORDER:
        acc = jnp.dot(h_ref[...], w_ref[:, j * COL:(j + 1) * COL], preferred_element_type=F32)
        if ZS_BLK <= j < Q_BLK or j == ZA_BLK:
            res = acc * _sigmoid(acc)
        elif j >= RS_BLK:
            res = _sigmoid(acc)
        elif Q_BLK <= j < V_BLK:
            sq = (acc * acc).astype(BF16)
            hw = ones_ref.shape[0]
            ss = jnp.concatenate([jnp.dot(sq[:, c * hw:(c + 1) * hw], ones_ref[...], preferred_element_type=F32)
                                  for c in range(COL // hw)], axis=1)
            y = acc * lax.rsqrt(ss * (1.0 / HEAD_DIM) + EPS) * qkw_ref[j - Q_BLK:j - Q_BLK + 1, :]
            parts = []
            for c in range(COL // LANES):
                yc = y[:, c * LANES:(c + 1) * LANES]
                parts.append(yc * cosv + pltpu.roll(yc, LANES - ROPE_DIM // 2, 1) * s_lo
                             + pltpu.roll(yc, ROPE_DIM // 2, 1) * s_hi)
            res = jnp.concatenate(parts, axis=1)
        else:
            res = acc
        if j in _NAT_DST:
            dst = _NAT_DST[j]
            nat_ref[:, dst * COL:(dst + 1) * COL] = res.astype(BF16)
        else:
            kind, group = divmod(j - Q_BLK, len(DILATIONS))
            d = DILATIONS[group]
            out_ref = d4_ref if group == 1 else d16_ref
            for c in range(COL // LANES):
                perm_ref[slot, c] = res[:, c * LANES:(c + 1) * LANES]
            for r in range(d):
                rows = [perm_ref[slot, c, pl.ds(r, tm // d, stride=d), :] for c in range(COL // LANES)]
                out_ref[r, :, kind * COL:(kind + 1) * COL] = jnp.concatenate(rows, axis=1).astype(BF16)
            slot += 1


def _proj_call(x, positions, norm_w, w_in, q_norm_w, k_norm_w, tm=512):
    bsz, length, _ = x.shape
    scale = LOG2E / math.sqrt(HEAD_DIM)
    qkw = jnp.concatenate([jnp.tile(q_norm_w.astype(F32) * scale, (1, HEADS)),
                           jnp.tile(k_norm_w.astype(F32), (1, HEADS)),
                           jnp.zeros((2, ATTN_W), F32)], axis=0)
    hid = jnp.arange(MXU_DIM) // HEAD_DIM
    ones = (hid[:, None] == hid[None, :]).astype(BF16)
    d4, d16 = DILATIONS[1], DILATIONS[2]
    freq, expand, rope_rows = _rope_tables()
    return pl.pallas_call(
        _proj_kernel,
        grid=(bsz, length // tm),
        in_specs=[pl.BlockSpec((None, tm, D_MODEL), lambda b, i: (b, i, 0)),
                  pl.BlockSpec((None, 1, tm), lambda b, i: (b, 0, i)),
                  _full_spec((1, D_MODEL)),
                  _full_spec((D_MODEL, IN_WIDTH)),
                  _full_spec((8, ATTN_W)),
                  _full_spec((MXU_DIM, MXU_DIM)),
                  _full_spec((ROPE_DIM // 2, 1)),
                  _full_spec((ROPE_DIM, LANES)),
                  _full_spec((8, LANES))],
        out_specs=[pl.BlockSpec((None, tm, NAT_W), lambda b, i: (b, i, 0)),
                   pl.BlockSpec((None, d4, tm // d4, QKV_W), lambda b, i: (b, 0, i, 0)),
                   pl.BlockSpec((None, d16, tm // d16, QKV_W), lambda b, i: (b, 0, i, 0))],
        out_shape=[jax.ShapeDtypeStruct((bsz, length, NAT_W), BF16),
                   jax.ShapeDtypeStruct((bsz, d4, length // d4, QKV_W), BF16),
                   jax.ShapeDtypeStruct((bsz, d16, length // d16, QKV_W), BF16)],
        scratch_shapes=[pltpu.VMEM((_N_PERM, COL // LANES, tm, LANES), F32), pltpu.VMEM((tm, D_MODEL), BF16)],
        compiler_params=pltpu.CompilerParams(dimension_semantics=("arbitrary", "arbitrary"),
                                             vmem_limit_bytes=VMEM_LIMIT),
        name="proj",
    )(x.astype(F32), positions.astype(jnp.int32)[:, None, :], norm_w.astype(F32)[None, :],
      w_in.astype(BF16), qkw, ones, freq, expand, rope_rows)


def _block_transpose(xs):
    xs = list(xs)
    blk = lax.broadcasted_iota(jnp.int32, xs[0].shape, 1) // SSM_GROUP
    dist = 1
    while dist < LANE_BLKS:
        upper = (blk & dist) != 0
        shift = dist * SSM_GROUP
        for i in range(LANE_BLKS):
            if i & dist:
                continue
            a, b = xs[i], xs[i + dist]
            xs[i] = jnp.where(upper, pltpu.roll(b, shift, 1), a)
            xs[i + dist] = jnp.where(upper, b, pltpu.roll(a, LANES - shift, 1))
        dist *= 2
    return xs


def _ssm_in_kernel(x_ref, nw_ref, w_ref, unat_ref, u2_ref, rows_ref):
    bsz, tl = x_ref.shape[0], x_ref.shape[1]
    for c in range(tl // CHUNK):
        x = x_ref[:, c * CHUNK:(c + 1) * CHUNK, :].reshape(bsz * CHUNK, D_MODEL)
        ms = jnp.mean(x * x, axis=-1, keepdims=True)
        h = (x * lax.rsqrt(ms + EPS) * nw_ref[...]).astype(BF16)
        u = jnp.dot(h, w_ref[...], preferred_element_type=F32)
        unat_ref[:, c * CHUNK:(c + 1) * CHUNK, :] = u.reshape(bsz, CHUNK, D_MODEL).astype(BF16)
        for b in range(bsz):
            for s8 in range(D_MODEL // LANES):
                rows_ref[c, s8, b * SSM_PITCH:b * SSM_PITCH + CHUNK, :] = (
                    u[b * CHUNK:(b + 1) * CHUNK, s8 * LANES:(s8 + 1) * LANES])
        for s8 in range(D_MODEL // LANES):
            for half in range(CHUNK // LANE_BLKS):
                xs = []
                for k in range(LANE_BLKS):
                    v = rows_ref[c, s8, pl.ds(half * LANE_BLKS + k, bsz, stride=SSM_PITCH), :].astype(BF16)
                    xs.append(pltpu.bitcast(v, jnp.uint32))
                ys = _block_transpose(xs)
                for g in range(LANE_BLKS):
                    u2_ref[s8 * LANE_BLKS + g, c, :, half * LANES:(half + 1) * LANES] = pltpu.bitcast(ys[g], BF16)


def _ssm_in_call(x, norm_w, w_u):
    bsz, length, _ = x.shape
    tl = SSM_CB * CHUNK
    nc = length // CHUNK
    return pl.pallas_call(
        _ssm_in_kernel,
        grid=(length // tl,),
        in_specs=[pl.BlockSpec((bsz, tl, D_MODEL), lambda i: (0, i, 0)),
                  _full_spec((1, D_MODEL)), _full_spec((D_MODEL, D_MODEL))],
        out_specs=[pl.BlockSpec((bsz, tl, D_MODEL), lambda i: (0, i, 0)),
                   pl.BlockSpec((SSM_GROUPS, SSM_CB, bsz, CHUNK_W), lambda i: (0, i, 0, 0))],
        out_shape=[jax.ShapeDtypeStruct((bsz, length, D_MODEL), BF16),
                   jax.ShapeDtypeStruct((SSM_GROUPS, nc, bsz, CHUNK_W), BF16)],
        scratch_shapes=[pltpu.VMEM((SSM_CB, D_MODEL // LANES, bsz * SSM_PITCH, LANES), F32)],
        compiler_params=pltpu.CompilerParams(dimension_semantics=("arbitrary",),
                                             vmem_limit_bytes=VMEM_LIMIT),
        name="ssm_in",
    )(x, norm_w.astype(F32)[None, :], w_u.astype(BF16))


def _ssm_out_kernel(y2_ref, u_ref, zs_ref, rs_ref, dskip_ref, wglu_ref, wso_ref, o_ref, rows_ref):
    bsz, tl = u_ref.shape[0], u_ref.shape[1]
    for c in range(tl // CHUNK):
        for s8 in range(D_MODEL // LANES):
            for half in range(CHUNK // LANE_BLKS):
                xs = [pltpu.bitcast(y2_ref[s8 * LANE_BLKS + g, c, :, half * LANES:(half + 1) * LANES], jnp.uint32)
                      for g in range(LANE_BLKS)]
                ys = _block_transpose(xs)
                for k in range(LANE_BLKS):
                    rows_ref[c, s8, pl.ds(half * LANE_BLKS + k, bsz, stride=SSM_PITCH), :] = (
                        pltpu.bitcast(ys[k], BF16).astype(F32))
        y = jnp.concatenate(
            [jnp.concatenate([rows_ref[c, s8, b * SSM_PITCH:b * SSM_PITCH + CHUNK, :]
                              for s8 in range(D_MODEL // LANES)], axis=1)
             for b in range(bsz)], axis=0)
        tok = slice(c * CHUNK, (c + 1) * CHUNK)
        u = u_ref[:, tok, :].reshape(bsz * CHUNK, D_MODEL).astype(F32)
        zs = zs_ref[:, tok, :].reshape(bsz * CHUNK, D_MODEL).astype(F32)
        y = y + dskip_ref[...] * u
        g = jax.nn.gelu(y, approximate=True)
        gate = _sigmoid(jnp.dot(g.astype(BF16), wglu_ref[...], preferred_element_type=F32))
        ys_in = (g * gate * zs).astype(BF16)
        ys = jnp.dot(ys_in, wso_ref[...], preferred_element_type=F32)
        gated = ys * rs_ref[:, tok, :].reshape(bsz * CHUNK, D_MODEL).astype(F32)
        o_ref[:, tok, :] = gated.reshape(bsz, CHUNK, D_MODEL).astype(BF16)


def _ssm_out_call(y2, unat, nat, d_skip, w_glu, w_ssm_out):
    bsz, length, _ = unat.shape
    tl = SSM_CB * CHUNK
    tok = lambda blk: pl.BlockSpec((bsz, tl, D_MODEL), lambda i: (0, i, blk))
    return pl.pallas_call(
        _ssm_out_kernel,
        grid=(length // tl,),
        in_specs=[pl.BlockSpec((SSM_GROUPS, SSM_CB, bsz, CHUNK_W), lambda i: (0, i, 0, 0)),
                  tok(0), tok(NAT_ZS // 2), tok(NAT_RS // 2),
                  _full_spec((1, D_MODEL)), _full_spec((D_MODEL, D_MODEL)), _full_spec((D_MODEL, D_MODEL))],
        out_specs=tok(0),
        out_shape=jax.ShapeDtypeStruct((bsz, length, D_MODEL), BF16),
        scratch_shapes=[pltpu.VMEM((SSM_CB, D_MODEL // LANES, bsz * SSM_PITCH, LANES), F32)],
        compiler_params=pltpu.CompilerParams(dimension_semantics=("arbitrary",),
                                             vmem_limit_bytes=VMEM_LIMIT),
        name="ssm_out",
    )(y2, unat, nat, nat, d_skip.astype(F32)[None, :], w_glu.astype(BF16), w_ssm_out.astype(BF16))


def _ssm_kernel(u_ref, st_ref, mt_ref, rtt_ref, coef_ref, y_ref, hloc_ref, hprev_ref):
    ng, nc, bsz = u_ref.shape[0], u_ref.shape[1], u_ref.shape[2]
    ns2 = 2 * SSM_STATE
    us, coefs = [], []
    for g in range(ng):
        u = u_ref[g].reshape(nc * bsz, CHUNK_W)
        us.append(u)
        half = nc * bsz // 2
        hloc_ref[g, :half, :] = jnp.dot(u[:half], st_ref[g], preferred_element_type=F32)
        hloc_ref[g, half:, :] = jnp.dot(u[half:], st_ref[g], preferred_element_type=F32)
        coefs.append((jnp.broadcast_to(coef_ref[g, 0:1, :], (bsz, ns2)),
                      jnp.broadcast_to(coef_ref[g, 1:2, :], (bsz, ns2))))

    def step(c, carry):
        r = pl.multiple_of(c * bsz, bsz)
        out = []
        for g in range(ng):
            hx, hy = carry[2 * g], carry[2 * g + 1]
            a, b = coefs[g]
            hprev_ref[g, pl.ds(r, bsz), :] = hx.astype(BF16)
            px = hloc_ref[g, pl.ds(r, bsz), 0:ns2]
            py = hloc_ref[g, pl.ds(r, bsz), ns2:2 * ns2]
            out += [a * hx + b * hy + px, a * hy - b * hx + py]
        return tuple(out)

    zero = jnp.zeros((bsz, ns2), F32)
    lax.fori_loop(0, nc, step, (zero,) * (2 * ng), unroll=8)
    for g in range(ng):
        y = jnp.dot(us[g], mt_ref[g], preferred_element_type=F32)
        y = y + lax.dot_general(hprev_ref[g], rtt_ref[g], (((1,), (1,)), ((), ())), preferred_element_type=F32)
        y_ref[g] = y.reshape(nc, bsz, CHUNK_W).astype(BF16)


def _ssm_call(u2, st, mt, rt, coef):
    g, nc, bsz, _ = u2.shape
    ng = SCAN_GROUPS
    gspec = lambda shape: pl.BlockSpec((ng,) + shape, lambda i: (i,) + (0,) * len(shape))
    return pl.pallas_call(
        _ssm_kernel,
        grid=(g // ng,),
        in_specs=[gspec((nc, bsz, CHUNK_W)), gspec((CHUNK_W, 4 * SSM_STATE)), gspec((CHUNK_W, CHUNK_W)),
                  gspec((CHUNK_W, 2 * SSM_STATE)), gspec((8, 2 * SSM_STATE))],
        out_specs=gspec((nc, bsz, CHUNK_W)),
        out_shape=jax.ShapeDtypeStruct(u2.shape, BF16),
        scratch_shapes=[pltpu.VMEM((ng, nc * bsz, 4 * SSM_STATE), F32),
                        pltpu.VMEM((ng, nc * bsz, 2 * SSM_STATE), BF16)],
        compiler_params=pltpu.CompilerParams(dimension_semantics=("arbitrary",),
                                             vmem_limit_bytes=VMEM_LIMIT),
        name="ssm_scan",
    )(u2, st, mt, rt, coef)


def _attn_kernel(q_ref, kc_ref, kp_ref, vc_ref, vp_ref, o_ref, lse_ref):
    nres, tq = q_ref.shape[0], q_ref.shape[1]
    qb = 128
    slab = pl.program_id(2)
    qi = lax.broadcasted_iota(jnp.int32, (qb, 2 * qb), 0)
    kk = lax.broadcasted_iota(jnp.int32, (qb, 2 * qb), 1)
    band = jnp.where((kk >= qi) & (kk <= qi + WINDOW_KEYS), 0.0, NEG_INF).astype(F32)
    band0 = band + jnp.where(jnp.logical_and(slab == 0, kk < qb), NEG_INF, 0.0).astype(F32)
    band = jnp.concatenate([band, band], axis=0)
    band0 = jnp.concatenate([band0, band0], axis=0)
    lo_q = lax.broadcasted_iota(jnp.int32, (qb, LANES), 1) < HEAD_DIM
    ones_kv = jnp.ones((2 * qb, LANES), BF16)

    head_slot = lax.broadcasted_iota(jnp.int32, (qb, LANES), 1) // LSE_LANES

    for r, j in [(r, j) for r in range(nres) for j in range(tq // qb)]:
        madd = band0 if j == 0 else band
        rows = slice(j * qb, (j + 1) * qb)
        m_all = jnp.zeros((qb, LANES), F32)
        den_all = jnp.ones((qb, LANES), F32)
        for hp in range(ATTN_W // LANES):
            cs = slice(hp * LANES, (hp + 1) * LANES)
            qp = q_ref[r, rows, cs]
            if j == 0:
                kp = jnp.concatenate([kp_ref[r, :, cs], kc_ref[r, 0:qb, cs]], axis=0)
                vp = jnp.concatenate([vp_ref[r, :, cs], vc_ref[r, 0:qb, cs]], axis=0)
            else:
                kp = kc_ref[r, (j - 1) * qb:(j + 1) * qb, cs]
                vp = vc_ref[r, (j - 1) * qb:(j + 1) * qb, cs]
            zero = jnp.zeros_like(qp)
            q2 = jnp.concatenate([jnp.where(lo_q, qp, zero), jnp.where(lo_q, zero, qp)], axis=0)
            s = lax.dot_general(q2, kp, (((1,), (1,)), ((), ())), preferred_element_type=F32)
            s = s + madd
            m = jnp.max(s, axis=1, keepdims=True)
            p = jnp.exp2(s - m)
            pv = jnp.dot(p.astype(BF16), jnp.concatenate([vp, ones_kv], axis=1), preferred_element_type=F32)
            num = jnp.where(lo_q, pv[:qb, :LANES], pv[qb:, :LANES])
            den = jnp.where(lo_q, pv[:qb, LANES:], pv[qb:, LANES:])
            o_ref[r, rows, cs] = (num / den).astype(BF16)
            for side in range(2):
                slot = head_slot == 2 * hp + side
                half = slice(side * qb, (side + 1) * qb)
                m_all = jnp.where(slot, m[half], m_all)
                den_all = jnp.where(slot, pv[half, LANES:], den_all)
        lse_ref[r, rows, :] = m_all + jnp.log2(den_all)


def _attn_call(qkv, group, blocks, rows_per_step=1024):
    bsz, d, ld, _ = qkv.shape
    tq = min(rows_per_step, ld)
    nres = min(d, rows_per_step // tq)
    nslab = ld // tq
    per = tq // 128
    cur = lambda blk: pl.BlockSpec((None, nres, tq, COL), lambda b, r, i: (b, r, i, blk))
    prev = lambda blk: pl.BlockSpec(
        (None, nres, 128, COL), lambda b, r, i: (b, r, jnp.maximum(i * per - 1, 0), blk))
    ospec = lambda w: pl.BlockSpec((None, nres, tq, w), lambda b, r, i: (b, r, i, 0))
    qb, kb, vb = blocks
    return pl.pallas_call(
        _attn_kernel,
        grid=(bsz, d // nres, nslab),
        in_specs=[cur(qb), cur(kb), prev(kb), cur(vb), prev(vb)],
        out_specs=[ospec(ATTN_W), ospec(LANES)],
        out_shape=[jax.ShapeDtypeStruct((bsz, d, ld, ATTN_W), BF16),
                   jax.ShapeDtypeStruct((bsz, d, ld, LANES), F32)],
        compiler_params=pltpu.CompilerParams(dimension_semantics=("arbitrary",) * 3,
                                             vmem_limit_bytes=VMEM_LIMIT),
        name=f"attn_d{DILATIONS[group]}",
    )(qkv, qkv, qkv, qkv, qkv)


def _merge_kernel(x_ref, ys_ref, za_ref, ra_ref,
                  a0_ref, a1_ref, a2_ref, l0_ref, l1_ref, l2_ref,
                  spread_ref, wao_ref, wo_ref, o_ref, il_ref):
    tm = x_ref.shape[0]

    def to_token_order(ref, base):
        d, nslab = ref.shape[0], ref.shape[2] // LANES
        for r in range(d):
            blk = ref[r].astype(F32)
            for c in range(nslab):
                il_ref[base + c, pl.ds(r, tm // d, stride=d), :] = blk[:, c * LANES:(c + 1) * LANES]
        return lambda rows: jnp.concatenate([il_ref[base + c, rows, :] for c in range(nslab)], axis=1)

    def per_head_to_lanes(w):
        hi = w.astype(BF16)
        lo = (w - hi.astype(F32)).astype(BF16)
        return jnp.dot(jnp.concatenate([hi, lo], axis=1), spread_ref[...], preferred_element_type=F32)

    wide_slabs = ATTN_W // LANES
    l1_at, l2_at = to_token_order(l1_ref, 0), to_token_order(l2_ref, 1)
    a1_at, a2_at = to_token_order(a1_ref, 2), to_token_order(a2_ref, 2 + wide_slabs)
    rows = slice(None)
    l0, l1, l2 = l0_ref[...], l1_at(rows), l2_at(rows)
    lm = jnp.maximum(jnp.maximum(l0, l1), l2)
    e0, e1, e2 = jnp.exp2(l0 - lm), jnp.exp2(l1 - lm), jnp.exp2(l2 - lm)
    inv = 1.0 / (e0 + e1 + e2)
    att = (per_head_to_lanes(e0 * inv) * a0_ref[...].astype(F32)
           + per_head_to_lanes(e1 * inv) * a1_at(rows)
           + per_head_to_lanes(e2 * inv) * a2_at(rows))
    ya_in = (att * za_ref[...].astype(F32)).astype(BF16)
    ya = jnp.dot(ya_in, wao_ref[...], preferred_element_type=F32)
    m = ys_ref[...].astype(F32) + ra_ref[...].astype(F32) * ya
    o_ref[...] = x_ref[...] + jnp.dot(m.astype(BF16), wo_ref[...], preferred_element_type=F32)


def _merge_call(x, ys, nat, attn, w_attn_out, w_o, tm=512):
    bsz, length, _ = x.shape
    wide = lambda blk: pl.BlockSpec((None, tm, D_MODEL), lambda b, i: (b, i, blk))
    half = lambda blk: pl.BlockSpec((None, tm, ATTN_W), lambda b, i: (b, i, blk))
    res = lambda d, w: pl.BlockSpec((None, d, tm // d, w), lambda b, i: (b, 0, i, 0))
    (a0, l0), (a1, l1), (a2, l2) = attn
    d4, d16 = DILATIONS[1], DILATIONS[2]
    a0, l0 = a0.reshape(bsz, length, ATTN_W), l0.reshape(bsz, length, LANES)
    spread = (jnp.arange(LANES)[:, None] == (jnp.arange(ATTN_W)[None, :] // HEAD_DIM) * LSE_LANES).astype(BF16)
    spread = jnp.concatenate([spread, spread], axis=0)
    return pl.pallas_call(
        _merge_kernel,
        grid=(bsz, length // tm),
        in_specs=[wide(0), wide(0), half(NAT_ZA), wide(NAT_RA // 2),
                  half(0), res(d4, ATTN_W), res(d16, ATTN_W),
                  pl.BlockSpec((None, tm, LANES), lambda b, i: (b, i, 0)), res(d4, LANES), res(d16, LANES),
                  _full_spec((2 * LANES, ATTN_W)), _full_spec((ATTN_W, D_MODEL)), _full_spec((D_MODEL, D_MODEL))],
        out_specs=wide(0),
        out_shape=jax.ShapeDtypeStruct((bsz, length, D_MODEL), F32),
        scratch_shapes=[pltpu.VMEM((2 + 2 * (ATTN_W // LANES), tm, LANES), F32)],
        compiler_params=pltpu.CompilerParams(dimension_semantics=("arbitrary", "arbitrary"),
                                             vmem_limit_bytes=VMEM_LIMIT),
        name="merge",
    )(x, ys, nat, nat, a0, a1, a2, l0, l1, l2, spread, w_attn_out.astype(BF16), w_o.astype(BF16))


def kernel(x, positions, norm_w, w_in, lam_re, lam_im, log_dt, b_re, b_im, c_re, c_im, d_skip, w_glu,
           q_norm_w, k_norm_w, w_ssm_out, w_attn_out, w_o):
    xf = x.astype(F32)
    for layer in range(norm_w.shape[0]):
        st, mt, rt, coef = _ssm_prep_call(lam_re[layer], lam_im[layer], log_dt[layer], b_re[layer],
                                          b_im[layer], c_re[layer], c_im[layer])
        nat, qkv4, qkv16 = _proj_call(xf, positions, norm_w[layer], w_in[layer], q_norm_w[layer],
                                      k_norm_w[layer])
        unat, u2 = _ssm_in_call(xf, norm_w[layer], w_in[layer][:, :D_MODEL])
        y2 = _ssm_call(u2, st, mt, rt, coef)
        ys = _ssm_out_call(y2, unat, nat, d_skip[layer], w_glu[layer], w_ssm_out[layer])
        attn = [_attn_call(nat[:, None], 0, (NAT_Q, NAT_K, NAT_V)),
                _attn_call(qkv4, 1, (0, 1, 2)),
                _attn_call(qkv16, 2, (0, 1, 2))]
        xf = _merge_call(xf, ys, nat, attn, w_attn_out[layer], w_o[layer])
    return xf.astype(x.dtype)
```

```python
import functools
import math

import jax
import jax.numpy as jnp
from jax import lax
from jax.experimental import pallas as pl
from jax.experimental.pallas import tpu as pltpu

F32 = jnp.float32
BF16 = jnp.bfloat16

D_MODEL = 1024
SSM_GROUP = 16
SSM_GROUPS = D_MODEL // SSM_GROUP
SSM_STATE = 64
CHUNK = 16
CHUNK_W = CHUNK * SSM_GROUP
HEAD_DIM = 64
HEADS = 8
ATTN_W = HEADS * HEAD_DIM
DILATIONS = (1, 4, 16)
WINDOW_KEYS = 128
ROPE_DIM = HEAD_DIM // 4
ROPE_THETA = 500000.0
EPS = 1e-6
NEG_INF = -1e30
LOG2E = math.log2(math.e)
COL = 512
U_BLK, ZS_BLK, Q_BLK, K_BLK, V_BLK, ZA_BLK, RS_BLK, RA_BLK = 0, 2, 4, 7, 10, 13, 14, 16
N_BLK = 18
IN_WIDTH = N_BLK * COL
NAT_ZS, NAT_Q, NAT_K, NAT_V, NAT_ZA, NAT_RS, NAT_RA = 0, 2, 3, 4, 5, 6, 8
NAT_W = 10 * COL
QKV_W = 3 * COL
LANES = 128
MXU_DIM = 256
LSE_LANES = LANES // HEADS
LANE_BLKS = LANES // SSM_GROUP
SCAN_GROUPS = 2
SSM_CB = 4
SSM_PITCH = 8 * 3
VMEM_LIMIT = 56 * 1024 * 1024


def _params(grid_rank):
    return pltpu.CompilerParams(dimension_semantics=("arbitrary",) * grid_rank, vmem_limit_bytes=VMEM_LIMIT)


def _full_spec(shape):
    nd = len(shape)
    return pl.BlockSpec(shape, lambda *_: (0,) * nd, pipeline_mode=pl.Buffered(1))


def _ssm_prep_kernel(lam_ref, btr_ref, bti_ref, cr_ref, ci_ref, st_ref, mt_ref, rtt_ref, coef_ref):
    ns2 = 2 * SSM_STATE
    lam = lam_ref[0]
    lr, li, dt = lam[0:1, :], lam[1:2, :], jnp.exp(lam[2:3, :])
    mag = jnp.exp(lr * dt)
    ar = mag * jnp.cos(li * dt)
    ai = mag * jnp.sin(li * dt)
    den = lr * lr + li * li
    nr = ar - 1.0
    fr = (nr * lr + ai * li) / den
    fi = (ai * lr - nr * li) / den
    pr, pi = [jnp.ones_like(ar)], [jnp.zeros_like(ar)]
    for _ in range(CHUNK):
        pr, pi = pr + [pr[-1] * ar - pi[-1] * ai], pi + [pr[-1] * ai + pi[-1] * ar]
    by_step = lambda vals: jnp.concatenate([jnp.broadcast_to(v, (SSM_GROUP, ns2)) for v in vals], axis=0)
    per_step = lambda a: jnp.concatenate([a] * CHUNK, axis=0)
    im_part = lax.broadcasted_iota(jnp.int32, (CHUNK_W, ns2), 1) >= SSM_STATE

    btr, bti = per_step(btr_ref[0]), per_step(bti_ref[0])
    bbr = fr * btr - fi * bti
    bbi = fr * bti + fi * btr
    apr = by_step([pr[CHUNK - 1 - s] for s in range(CHUNK)])
    api = by_step([pi[CHUNK - 1 - s] for s in range(CHUNK)])
    wre = apr * bbr - api * bbi
    wim = apr * bbi + api * bbr
    st = jnp.where(im_part, wim, wre)
    st_sw = jnp.where(im_part, wre, wim)
    st_ref[0] = jnp.concatenate([st, st_sw], axis=1).astype(BF16)

    cr, ci = per_step(cr_ref[0]), per_step(ci_ref[0])
    qr = by_step([pr[t + 1] for t in range(CHUNK)])
    qi = by_step([pi[t + 1] for t in range(CHUNK)])
    rtt_ref[0] = jnp.where(im_part, -(cr * qi + ci * qr), cr * qr - ci * qi).astype(BF16)

    cct = jnp.where(im_part, -ci, cr)
    krw = lax.dot_general(st, cct, (((1,), (1,)), ((), ())), preferred_element_type=F32,
                          precision=lax.Precision.HIGHEST)
    lane_blk = lax.broadcasted_iota(jnp.int32, (CHUNK_W, CHUNK_W), 1) // SSM_GROUP
    mt = jnp.zeros((CHUNK_W, CHUNK_W), F32)
    for t in range(CHUNK):
        sh = SSM_GROUP * (CHUNK - 1 - t)
        if sh == 0:
            shifted = krw
        else:
            shifted = jnp.concatenate([krw[sh:, :], jnp.zeros((sh, CHUNK_W), F32)], axis=0)
        mt = jnp.where(lane_blk == t, shifted, mt)
    mt_ref[0] = mt.astype(BF16)

    im_row = lax.broadcasted_iota(jnp.int32, (1, ns2), 1) >= SSM_STATE
    bco = jnp.where(im_row, pi[CHUNK], -pi[CHUNK])
    coef_ref[0] = jnp.concatenate([pr[CHUNK], bco, jnp.zeros((6, ns2), F32)], axis=0)


def _ssm_prep_call(lam_re, lam_im, log_dt, b_re, b_im, c_re, c_im):
    g, n, p = SSM_GROUPS, SSM_STATE, SSM_GROUP
    f = lambda a: a.astype(F32)
    twice = lambda a: jnp.tile(f(a), (1,) * (a.ndim - 1) + (2,))
    ldt_b = jnp.broadcast_to(f(log_dt)[:, None], (g, n))
    lam = jnp.concatenate([twice(lam_re)[:, None], twice(lam_im)[:, None], twice(ldt_b)[:, None],
                           jnp.zeros((g, 5, 2 * n), F32)], axis=1)
    bt = lambda b: twice(b.transpose(0, 2, 1))
    gspec = lambda shape: pl.BlockSpec((1,) + shape, lambda i: (i, 0, 0))
    return pl.pallas_call(
        _ssm_prep_kernel,
        grid=(g,),
        in_specs=[gspec((8, 2 * n))] + [gspec((p, 2 * n))] * 4,
        out_specs=[gspec((CHUNK_W, 4 * n)), gspec((CHUNK_W, CHUNK_W)), gspec((CHUNK_W, 2 * n)),
                   gspec((8, 2 * n))],
        out_shape=[jax.ShapeDtypeStruct((g, CHUNK_W, 4 * n), BF16),
                   jax.ShapeDtypeStruct((g, CHUNK_W, CHUNK_W), BF16),
                   jax.ShapeDtypeStruct((g, CHUNK_W, 2 * n), BF16),
                   jax.ShapeDtypeStruct((g, 8, 2 * n), F32)],
        compiler_params=_params(1),
        name="ssm_prep",
    )(lam, bt(b_re), bt(b_im), twice(c_re), twice(c_im))


def _rope_tables():
    lane = jnp.arange(LANES) % HEAD_DIM
    half = ROPE_DIM // 2
    inv = ROPE_THETA ** (-jnp.arange(0, ROPE_DIM, 2, dtype=F32) / ROPE_DIM)
    expand = ((lane[None, :] < ROPE_DIM) & (lane[None, :] % half == jnp.arange(half)[:, None])).astype(F32)
    s_lo = jnp.where(lane < half, -1.0, 0.0)
    s_hi = jnp.where((lane >= half) & (lane < ROPE_DIM), 1.0, 0.0)
    unrot = jnp.where(lane >= ROPE_DIM, 1.0, 0.0)
    rows = jnp.concatenate([s_lo[None], s_hi[None], unrot[None], jnp.zeros((5, LANES), F32)], axis=0)
    expand = jnp.concatenate([expand, expand], axis=0).astype(BF16)
    return inv[:, None].astype(F32), expand, rows.astype(F32)


def _sigmoid(v):
    return 1.0 / (1.0 + jnp.exp(-v))


_NAT_DST = {2: NAT_ZS, 3: NAT_ZS + 1, Q_BLK: NAT_Q, K_BLK: NAT_K, V_BLK: NAT_V,
            ZA_BLK: NAT_ZA, 14: NAT_RS, 15: NAT_RS + 1, 16: NAT_RA, 17: NAT_RA + 1}


_PROJ_ORDER = (4, 2, 7, 3, 5, 13, 8, 14, 6, 15, 9, 16, 11, 17, 12, 10)
_N_PERM = 6


def _proj_kernel(x_ref, pos_ref, nw_ref, w_ref, qkw_ref, ones_ref, freq_ref, expand_ref, rope_ref,
                 nat_ref, d4_ref, d16_ref, perm_ref, h_ref):
    tm = x_ref.shape[0]
    x = x_ref[...]
    ms = jnp.mean(x * x, axis=-1, keepdims=True)
    h_ref[...] = (x * lax.rsqrt(ms + EPS) * nw_ref[...]).astype(BF16)
    ang = freq_ref[...] * pos_ref[...].astype(F32)

    def spread(v):
        hi = v.astype(BF16)
        lo = (v - hi.astype(F32)).astype(BF16)
        return lax.dot_general(jnp.concatenate([hi, lo], axis=0), expand_ref[...], (((0,), (0,)), ((), ())),
                               preferred_element_type=F32)

    cosv = spread(jnp.cos(ang)) + rope_ref[2:3, :]
    sinv = spread(jnp.sin(ang))
    s_lo = sinv * rope_ref[0:1, :]
    s_hi = sinv * rope_ref[1:2, :]
    slot = 0
    for j in _PROJ_ORDER:
        acc = jnp.dot(h_ref[...], w_ref[:, j * COL:(j + 1) * COL], preferred_element_type=F32)
        if ZS_BLK <= j < Q_BLK or j == ZA_BLK:
            res = acc * _sigmoid(acc)
        elif j >= RS_BLK:
            res = _sigmoid(acc)
        elif Q_BLK <= j < V_BLK:
            sq = (acc * acc).astype(BF16)
            hw = ones_ref.shape[0]
            ss = jnp.concatenate([jnp.dot(sq[:, c * hw:(c + 1) * hw], ones_ref[...], preferred_element_type=F32)
                                  for c in range(COL // hw)], axis=1)
            y = acc * lax.rsqrt(ss * (1.0 / HEAD_DIM) + EPS) * qkw_ref[j - Q_BLK:j - Q_BLK + 1, :]
            parts = []
            for c in range(COL // LANES):
                yc = y[:, c * LANES:(c + 1) * LANES]
                parts.append(yc * cosv + pltpu.roll(yc, LANES - ROPE_DIM // 2, 1) * s_lo
                             + pltpu.roll(yc, ROPE_DIM // 2, 1) * s_hi)
            res = jnp.concatenate(parts, axis=1)
        else:
            res = acc
        if j in _NAT_DST:
            dst = _NAT_DST[j]
            nat_ref[:, dst * COL:(dst + 1) * COL] = res.astype(BF16)
        else:
            kind, group = divmod(j - Q_BLK, len(DILATIONS))
            d = DILATIONS[group]
            out_ref = d4_ref if group == 1 else d16_ref
            for c in range(COL // LANES):
                perm_ref[slot, c] = res[:, c * LANES:(c + 1) * LANES]
            for r in range(d):
                rows = [perm_ref[slot, c, pl.ds(r, tm // d, stride=d), :] for c in range(COL // LANES)]
                out_ref[r, :, kind * COL:(kind + 1) * COL] = jnp.concatenate(rows, axis=1).astype(BF16)
            slot += 1


def _proj_call(x, positions, norm_w, w_in, q_norm_w, k_norm_w, tm=512):
    bsz, length, _ = x.shape
    scale = LOG2E / math.sqrt(HEAD_DIM)
    qkw = jnp.concatenate([jnp.tile(q_norm_w.astype(F32) * scale, (1, HEADS)),
                           jnp.tile(k_norm_w.astype(F32), (1, HEADS)),
                           jnp.zeros((2, ATTN_W), F32)], axis=0)
    hid = jnp.arange(MXU_DIM) // HEAD_DIM
    ones = (hid[:, None] == hid[None, :]).astype(BF16)
    d4, d16 = DILATIONS[1], DILATIONS[2]
    freq, expand, rope_rows = _rope_tables()
    return pl.pallas_call(
        _proj_kernel,
        grid=(bsz, length // tm),
        in_specs=[pl.BlockSpec((None, tm, D_MODEL), lambda b, i: (b, i, 0)),
                  pl.BlockSpec((None, 1, tm), lambda b, i: (b, 0, i)),
                  _full_spec((1, D_MODEL)),
                  _full_spec((D_MODEL, IN_WIDTH)),
                  _full_spec((8, ATTN_W)),
                  _full_spec((MXU_DIM, MXU_DIM)),
                  _full_spec((ROPE_DIM // 2, 1)),
                  _full_spec((ROPE_DIM, LANES)),
                  _full_spec((8, LANES))],
        out_specs=[pl.BlockSpec((None, tm, NAT_W), lambda b, i: (b, i, 0)),
                   pl.BlockSpec((None, d4, tm // d4, QKV_W), lambda b, i: (b, 0, i, 0)),
                   pl.BlockSpec((None, d16, tm // d16, QKV_W), lambda b, i: (b, 0, i, 0))],
        out_shape=[jax.ShapeDtypeStruct((bsz, length, NAT_W), BF16),
                   jax.ShapeDtypeStruct((bsz, d4, length // d4, QKV_W), BF16),
                   jax.ShapeDtypeStruct((bsz, d16, length // d16, QKV_W), BF16)],
        scratch_shapes=[pltpu.VMEM((_N_PERM, COL // LANES, tm, LANES), F32), pltpu.VMEM((tm, D_MODEL), BF16)],
        compiler_params=_params(2),
        name="proj",
    )(x.astype(F32), positions.astype(jnp.int32)[:, None, :], norm_w.astype(F32)[None, :],
      w_in.astype(BF16), qkw, ones, freq, expand, rope_rows)


def _block_transpose(xs):
    xs = list(xs)
    blk = lax.broadcasted_iota(jnp.int32, xs[0].shape, 1) // SSM_GROUP
    dist = 1
    while dist < LANE_BLKS:
        upper = (blk & dist) != 0
        shift = dist * SSM_GROUP
        for i in range(LANE_BLKS):
            if i & dist:
                continue
            a, b = xs[i], xs[i + dist]
            xs[i] = jnp.where(upper, pltpu.roll(b, shift, 1), a)
            xs[i + dist] = jnp.where(upper, b, pltpu.roll(a, LANES - shift, 1))
        dist *= 2
    return xs


def _ssm_in_kernel(x_ref, nw_ref, w_ref, unat_ref, u2_ref, rows_ref):
    bsz, tl = x_ref.shape[0], x_ref.shape[1]
    for c in range(tl // CHUNK):
        x = x_ref[:, c * CHUNK:(c + 1) * CHUNK, :].reshape(bsz * CHUNK, D_MODEL)
        ms = jnp.mean(x * x, axis=-1, keepdims=True)
        h = (x * lax.rsqrt(ms + EPS) * nw_ref[...]).astype(BF16)
        u = jnp.dot(h, w_ref[...], preferred_element_type=F32)
        unat_ref[:, c * CHUNK:(c + 1) * CHUNK, :] = u.reshape(bsz, CHUNK, D_MODEL).astype(BF16)
        for b in range(bsz):
            for s8 in range(D_MODEL // LANES):
                rows_ref[c, s8, b * SSM_PITCH:b * SSM_PITCH + CHUNK, :] = (
                    u[b * CHUNK:(b + 1) * CHUNK, s8 * LANES:(s8 + 1) * LANES])
        for s8 in range(D_MODEL // LANES):
            for half in range(CHUNK // LANE_BLKS):
                xs = []
                for k in range(LANE_BLKS):
                    v = rows_ref[c, s8, pl.ds(half * LANE_BLKS + k, bsz, stride=SSM_PITCH), :].astype(BF16)
                    xs.append(pltpu.bitcast(v, jnp.uint32))
                ys = _block_transpose(xs)
                for g in range(LANE_BLKS):
                    u2_ref[s8 * LANE_BLKS + g, c, :, half * LANES:(half + 1) * LANES] = pltpu.bitcast(ys[g], BF16)


def _ssm_in_call(x, norm_w, w_u):
    bsz, length, _ = x.shape
    tl = SSM_CB * CHUNK
    nc = length // CHUNK
    return pl.pallas_call(
        _ssm_in_kernel,
        grid=(length // tl,),
        in_specs=[pl.BlockSpec((bsz, tl, D_MODEL), lambda i: (0, i, 0)),
                  _full_spec((1, D_MODEL)), _full_spec((D_MODEL, D_MODEL))],
        out_specs=[pl.BlockSpec((bsz, tl, D_MODEL), lambda i: (0, i, 0)),
                   pl.BlockSpec((SSM_GROUPS, SSM_CB, bsz, CHUNK_W), lambda i: (0, i, 0, 0))],
        out_shape=[jax.ShapeDtypeStruct((bsz, length, D_MODEL), BF16),
                   jax.ShapeDtypeStruct((SSM_GROUPS, nc, bsz, CHUNK_W), BF16)],
        scratch_shapes=[pltpu.VMEM((SSM_CB, D_MODEL // LANES, bsz * SSM_PITCH, LANES), F32)],
        compiler_params=_params(1),
        name="ssm_in",
    )(x, norm_w.astype(F32)[None, :], w_u.astype(BF16))


def _ssm_out_kernel(y2_ref, u_ref, zs_ref, rs_ref, dskip_ref, wglu_ref, wso_ref, o_ref, rows_ref):
    bsz, tl = u_ref.shape[0], u_ref.shape[1]
    for c in range(tl // CHUNK):
        for s8 in range(D_MODEL // LANES):
            for half in range(CHUNK // LANE_BLKS):
                xs = [pltpu.bitcast(y2_ref[s8 * LANE_BLKS + g, c, :, half * LANES:(half + 1) * LANES], jnp.uint32)
                      for g in range(LANE_BLKS)]
                ys = _block_transpose(xs)
                for k in range(LANE_BLKS):
                    rows_ref[c, s8, pl.ds(half * LANE_BLKS + k, bsz, stride=SSM_PITCH), :] = (
                        pltpu.bitcast(ys[k], BF16).astype(F32))
        y = jnp.concatenate(
            [jnp.concatenate([rows_ref[c, s8, b * SSM_PITCH:b * SSM_PITCH + CHUNK, :]
                              for s8 in range(D_MODEL // LANES)], axis=1)
             for b in range(bsz)], axis=0)
        tok = slice(c * CHUNK, (c + 1) * CHUNK)
        u = u_ref[:, tok, :].reshape(bsz * CHUNK, D_MODEL).astype(F32)
        zs = zs_ref[:, tok, :].reshape(bsz * CHUNK, D_MODEL).astype(F32)
        y = y + dskip_ref[...] * u
        g = jax.nn.gelu(y, approximate=True)
        gate = _sigmoid(jnp.dot(g.astype(BF16), wglu_ref[...], preferred_element_type=F32))
        ys_in = (g * gate * zs).astype(BF16)
        ys = jnp.dot(ys_in, wso_ref[...], preferred_element_type=F32)
        gated = ys * rs_ref[:, tok, :].reshape(bsz * CHUNK, D_MODEL).astype(F32)
        o_ref[:, tok, :] = gated.reshape(bsz, CHUNK, D_MODEL).astype(BF16)


def _ssm_out_call(y2, unat, nat, d_skip, w_glu, w_ssm_out):
    bsz, length, _ = unat.shape
    tl = SSM_CB * CHUNK
    tok = lambda blk: pl.BlockSpec((bsz, tl, D_MODEL), lambda i: (0, i, blk))
    return pl.pallas_call(
        _ssm_out_kernel,
        grid=(length // tl,),
        in_specs=[pl.BlockSpec((SSM_GROUPS, SSM_CB, bsz, CHUNK_W), lambda i: (0, i, 0, 0)),
                  tok(0), tok(NAT_ZS // 2), tok(NAT_RS // 2),
                  _full_spec((1, D_MODEL)), _full_spec((D_MODEL, D_MODEL)), _full_spec((D_MODEL, D_MODEL))],
        out_specs=tok(0),
        out_shape=jax.ShapeDtypeStruct((bsz, length, D_MODEL), BF16),
        scratch_shapes=[pltpu.VMEM((SSM_CB, D_MODEL // LANES, bsz * SSM_PITCH, LANES), F32)],
        compiler_params=_params(1),
        name="ssm_out",
    )(y2, unat, nat, nat, d_skip.astype(F32)[None, :], w_glu.astype(BF16), w_ssm_out.astype(BF16))


def _ssm_kernel(u_ref, st_ref, mt_ref, rtt_ref, coef_ref, y_ref, hloc_ref, hprev_ref):
    ng, nc, bsz = u_ref.shape[0], u_ref.shape[1], u_ref.shape[2]
    ns2 = 2 * SSM_STATE
    us, coefs = [], []
    for g in range(ng):
        u = u_ref[g].reshape(nc * bsz, CHUNK_W)
        us.append(u)
        half = nc * bsz // 2
        hloc_ref[g, :half, :] = jnp.dot(u[:half], st_ref[g], preferred_element_type=F32)
        hloc_ref[g, half:, :] = jnp.dot(u[half:], st_ref[g], preferred_element_type=F32)
        coefs.append((jnp.broadcast_to(coef_ref[g, 0:1, :], (bsz, ns2)),
                      jnp.broadcast_to(coef_ref[g, 1:2, :], (bsz, ns2))))

    def step(c, carry):
        r = pl.multiple_of(c * bsz, bsz)
        out = []
        for g in range(ng):
            hx, hy = carry[2 * g], carry[2 * g + 1]
            a, b = coefs[g]
            hprev_ref[g, pl.ds(r, bsz), :] = hx.astype(BF16)
            px = hloc_ref[g, pl.ds(r, bsz), 0:ns2]
            py = hloc_ref[g, pl.ds(r, bsz), ns2:2 * ns2]
            out += [a * hx + b * hy + px, a * hy - b * hx + py]
        return tuple(out)

    zero = jnp.zeros((bsz, ns2), F32)
    lax.fori_loop(0, nc, step, (zero,) * (2 * ng), unroll=8)
    for g in range(ng):
        y = jnp.dot(us[g], mt_ref[g], preferred_element_type=F32)
        y = y + lax.dot_general(hprev_ref[g], rtt_ref[g], (((1,), (1,)), ((), ())), preferred_element_type=F32)
        y_ref[g] = y.reshape(nc, bsz, CHUNK_W).astype(BF16)


def _ssm_call(u2, st, mt, rt, coef):
    g, nc, bsz, _ = u2.shape
    ng = SCAN_GROUPS
    gspec = lambda shape: pl.BlockSpec((ng,) + shape, lambda i: (i,) + (0,) * len(shape))
    return pl.pallas_call(
        _ssm_kernel,
        grid=(g // ng,),
        in_specs=[gspec((nc, bsz, CHUNK_W)), gspec((CHUNK_W, 4 * SSM_STATE)), gspec((CHUNK_W, CHUNK_W)),
                  gspec((CHUNK_W, 2 * SSM_STATE)), gspec((8, 2 * SSM_STATE))],
        out_specs=gspec((nc, bsz, CHUNK_W)),
        out_shape=jax.ShapeDtypeStruct(u2.shape, BF16),
        scratch_shapes=[pltpu.VMEM((ng, nc * bsz, 4 * SSM_STATE), F32),
                        pltpu.VMEM((ng, nc * bsz, 2 * SSM_STATE), BF16)],
        compiler_params=_params(1),
        name="ssm_scan",
    )(u2, st, mt, rt, coef)


def _attn_kernel(q_ref, kc_ref, kp_ref, vc_ref, vp_ref, o_ref, lse_ref):
    nres, tq = q_ref.shape[0], q_ref.shape[1]
    qb = 128
    slab = pl.program_id(2)
    qi = lax.broadcasted_iota(jnp.int32, (qb, 2 * qb), 0)
    kk = lax.broadcasted_iota(jnp.int32, (qb, 2 * qb), 1)
    band = jnp.where((kk >= qi) & (kk <= qi + WINDOW_KEYS), 0.0, NEG_INF).astype(F32)
    band0 = band + jnp.where(jnp.logical_and(slab == 0, kk < qb), NEG_INF, 0.0).astype(F32)
    band = jnp.concatenate([band, band], axis=0)
    band0 = jnp.concatenate([band0, band0], axis=0)
    lo_q = lax.broadcasted_iota(jnp.int32, (qb, LANES), 1) < HEAD_DIM
    ones_kv = jnp.ones((2 * qb, LANES), BF16)

    head_slot = lax.broadcasted_iota(jnp.int32, (qb, LANES), 1) // LSE_LANES

    for r, j in [(r, j) for r in range(nres) for j in range(tq // qb)]:
        madd = band0 if j == 0 else band
        rows = slice(j * qb, (j + 1) * qb)
        m_all = jnp.zeros((qb, LANES), F32)
        den_all = jnp.ones((qb, LANES), F32)
        for hp in range(ATTN_W // LANES):
            cs = slice(hp * LANES, (hp + 1) * LANES)
            qp = q_ref[r, rows, cs]
            if j == 0:
                kp = jnp.concatenate([kp_ref[r, :, cs], kc_ref[r, 0:qb, cs]], axis=0)
                vp = jnp.concatenate([vp_ref[r, :, cs], vc_ref[r, 0:qb, cs]], axis=0)
            else:
                kp = kc_ref[r, (j - 1) * qb:(j + 1) * qb, cs]
                vp = vc_ref[r, (j - 1) * qb:(j + 1) * qb, cs]
            zero = jnp.zeros_like(qp)
            q2 = jnp.concatenate([jnp.where(lo_q, qp, zero), jnp.where(lo_q, zero, qp)], axis=0)
            s = lax.dot_general(q2, kp, (((1,), (1,)), ((), ())), preferred_element_type=F32)
            s = s + madd
            m = jnp.max(s, axis=1, keepdims=True)
            p = jnp.exp2(s - m)
            pv = jnp.dot(p.astype(BF16), jnp.concatenate([vp, ones_kv], axis=1), preferred_element_type=F32)
            num = jnp.where(lo_q, pv[:qb, :LANES], pv[qb:, :LANES])
            den = jnp.where(lo_q, pv[:qb, LANES:], pv[qb:, LANES:])
            o_ref[r, rows, cs] = (num / den).astype(BF16)
            for side in range(2):
                slot = head_slot == 2 * hp + side
                half = slice(side * qb, (side + 1) * qb)
                m_all = jnp.where(slot, m[half], m_all)
                den_all = jnp.where(slot, pv[half, LANES:], den_all)
        lse_ref[r, rows, :] = m_all + jnp.log2(den_all)


def _attn_call(qkv, group, blocks, rows_per_step=1024):
    bsz, d, ld, _ = qkv.shape
    tq = min(rows_per_step, ld)
    nres = min(d, rows_per_step // tq)
    nslab = ld // tq
    per = tq // 128
    cur = lambda blk: pl.BlockSpec((None, nres, tq, COL), lambda b, r, i: (b, r, i, blk))
    prev = lambda blk: pl.BlockSpec(
        (None, nres, 128, COL), lambda b, r, i: (b, r, jnp.maximum(i * per - 1, 0), blk))
    ospec = lambda w: pl.BlockSpec((None, nres, tq, w), lambda b, r, i: (b, r, i, 0))
    qb, kb, vb = blocks
    return pl.pallas_call(
        _attn_kernel,
        grid=(bsz, d // nres, nslab),
        in_specs=[cur(qb), cur(kb), prev(kb), cur(vb), prev(vb)],
        out_specs=[ospec(ATTN_W), ospec(LANES)],
        out_shape=[jax.ShapeDtypeStruct((bsz, d, ld, ATTN_W), BF16),
                   jax.ShapeDtypeStruct((bsz, d, ld, LANES), F32)],
        compiler_params=_params(3),
        name=f"attn_d{DILATIONS[group]}",
    )(qkv, qkv, qkv, qkv, qkv)


def _merge_kernel(x_ref, ys_ref, za_ref, ra_ref,
                  a0_ref, a1_ref, a2_ref, l0_ref, l1_ref, l2_ref,
                  spread_ref, wao_ref, wo_ref, o_ref, il_ref):
    tm = x_ref.shape[0]

    def to_token_order(ref, base):
        d, nslab = ref.shape[0], ref.shape[2] // LANES
        for r in range(d):
            blk = ref[r].astype(F32)
            for c in range(nslab):
                il_ref[base + c, pl.ds(r, tm // d, stride=d), :] = blk[:, c * LANES:(c + 1) * LANES]
        return lambda rows: jnp.concatenate([il_ref[base + c, rows, :] for c in range(nslab)], axis=1)

    def per_head_to_lanes(w):
        hi = w.astype(BF16)
        lo = (w - hi.astype(F32)).astype(BF16)
        return jnp.dot(jnp.concatenate([hi, lo], axis=1), spread_ref[...], preferred_element_type=F32)

    wide_slabs = ATTN_W // LANES
    l1_at, l2_at = to_token_order(l1_ref, 0), to_token_order(l2_ref, 1)
    a1_at, a2_at = to_token_order(a1_ref, 2), to_token_order(a2_ref, 2 + wide_slabs)
    rows = slice(None)
    l0, l1, l2 = l0_ref[...], l1_at(rows), l2_at(rows)
    lm = jnp.maximum(jnp.maximum(l0, l1), l2)
    e0, e1, e2 = jnp.exp2(l0 - lm), jnp.exp2(l1 - lm), jnp.exp2(l2 - lm)
    inv = 1.0 / (e0 + e1 + e2)
    att = (per_head_to_lanes(e0 * inv) * a0_ref[...].astype(F32)
           + per_head_to_lanes(e1 * inv) * a1_at(rows)
           + per_head_to_lanes(e2 * inv) * a2_at(rows))
    ya_in = (att * za_ref[...].astype(F32)).astype(BF16)
    ya = jnp.dot(ya_in, wao_ref[...], preferred_element_type=F32)
    m = ys_ref[...].astype(F32) + ra_ref[...].astype(F32) * ya
    o_ref[...] = x_ref[...] + jnp.dot(m.astype(BF16), wo_ref[...], preferred_element_type=F32)


def _merge_call(x, ys, nat, attn, w_attn_out, w_o, tm=1024):
    bsz, length, _ = x.shape
    wide = lambda blk: pl.BlockSpec((None, tm, D_MODEL), lambda b, i: (b, i, blk))
    half = lambda blk: pl.BlockSpec((None, tm, ATTN_W), lambda b, i: (b, i, blk))
    res = lambda d, w: pl.BlockSpec((None, d, tm // d, w), lambda b, i: (b, 0, i, 0))
    (a0, l0), (a1, l1), (a2, l2) = attn
    d4, d16 = DILATIONS[1], DILATIONS[2]
    a0, l0 = a0.reshape(bsz, length, ATTN_W), l0.reshape(bsz, length, LANES)
    spread = (jnp.arange(LANES)[:, None] == (jnp.arange(ATTN_W)[None, :] // HEAD_DIM) * LSE_LANES).astype(BF16)
    spread = jnp.concatenate([spread, spread], axis=0)
    return pl.pallas_call(
        _merge_kernel,
        grid=(bsz, length // tm),
        in_specs=[wide(0), wide(0), half(NAT_ZA), wide(NAT_RA // 2),
                  half(0), res(d4, ATTN_W), res(d16, ATTN_W),
                  pl.BlockSpec((None, tm, LANES), lambda b, i: (b, i, 0)), res(d4, LANES), res(d16, LANES),
                  _full_spec((2 * LANES, ATTN_W)), _full_spec((ATTN_W, D_MODEL)), _full_spec((D_MODEL, D_MODEL))],
        out_specs=wide(0),
        out_shape=jax.ShapeDtypeStruct((bsz, length, D_MODEL), F32),
        scratch_shapes=[pltpu.VMEM((2 + 2 * (ATTN_W // LANES), tm, LANES), F32)],
        compiler_params=_params(2),
        name="merge",
    )(x, ys, nat, nat, a0, a1, a2, l0, l1, l2, spread, w_attn_out.astype(BF16), w_o.astype(BF16))


def kernel(x, positions, norm_w, w_in, lam_re, lam_im, log_dt, b_re, b_im, c_re, c_im, d_skip, w_glu,
           q_norm_w, k_norm_w, w_ssm_out, w_attn_out, w_o):
    xf = x.astype(F32)
    for layer in range(norm_w.shape[0]):
        st, mt, rt, coef = _ssm_prep_call(lam_re[layer], lam_im[layer], log_dt[layer], b_re[layer],
                                          b_im[layer], c_re[layer], c_im[layer])
        nat, qkv4, qkv16 = _proj_call(xf, positions, norm_w[layer], w_in[layer], q_norm_w[layer],
                                      k_norm_w[layer])
        unat, u2 = _ssm_in_call(xf, norm_w[layer], w_in[layer][:, :D_MODEL])
        y2 = _ssm_call(u2, st, mt, rt, coef)
        ys = _ssm_out_call(y2, unat, nat, d_skip[layer], w_glu[layer], w_ssm_out[layer])
        attn = [_attn_call(nat[:, None], 0, (NAT_Q, NAT_K, NAT_V)),
                _attn_call(qkv4, 1, (0, 1, 2)),
                _attn_call(qkv16, 2, (0, 1, 2))]
        xf = _merge_call(xf, ys, nat, attn, w_attn_out[layer], w_o[layer])
    return xf.astype(x.dtype)
```

```python
import math

import jax
import jax.numpy as jnp
from jax import lax
from jax.experimental import pallas as pl
from jax.experimental.pallas import tpu as pltpu

F32 = jnp.float32
BF16 = jnp.bfloat16

D_MODEL = 1024
SSM_GROUP = 16
SSM_GROUPS = D_MODEL // SSM_GROUP
SSM_STATE = 64
CHUNK = 16
CHUNK_W = CHUNK * SSM_GROUP
HEAD_DIM = 64
HEADS = 8
ATTN_W = HEADS * HEAD_DIM
DILATIONS = (1, 4, 16)
WINDOW_KEYS = 128
ROPE_DIM = HEAD_DIM // 4
ROPE_THETA = 500000.0
EPS = 1e-6
NEG_INF = -1e30
LOG2E = math.log2(math.e)
COL = 512
ZS_BLK, Q_BLK, K_BLK, V_BLK, ZA_BLK, RS_BLK, RA_BLK = 2, 4, 7, 10, 13, 14, 16
N_BLK = 18
IN_WIDTH = N_BLK * COL
NAT_ZS, NAT_Q, NAT_K, NAT_V, NAT_ZA, NAT_RS, NAT_RA = 0, 2, 3, 4, 5, 6, 8
NAT_W = 10 * COL
QKV_W = 3 * COL
LANES = 128
MXU_DIM = 256
LSE_LANES = LANES // HEADS
LANE_BLKS = LANES // SSM_GROUP
SCAN_GROUPS = 2
SSM_CB = 4
SSM_PITCH = 8 * 3
VMEM_LIMIT = 56 * 1024 * 1024


def _params(grid_rank):
    return pltpu.CompilerParams(dimension_semantics=("arbitrary",) * grid_rank, vmem_limit_bytes=VMEM_LIMIT)


def _full_spec(shape):
    nd = len(shape)
    return pl.BlockSpec(shape, lambda *_: (0,) * nd, pipeline_mode=pl.Buffered(1))


def _ssm_prep_kernel(lam_ref, btr_ref, bti_ref, cr_ref, ci_ref, st_ref, mt_ref, rtt_ref, coef_ref):
    ns2 = 2 * SSM_STATE
    lam = lam_ref[0]
    lr, li, dt = lam[0:1, :], lam[1:2, :], jnp.exp(lam[2:3, :])
    mag = jnp.exp(lr * dt)
    ar = mag * jnp.cos(li * dt)
    ai = mag * jnp.sin(li * dt)
    den = lr * lr + li * li
    nr = ar - 1.0
    fr = (nr * lr + ai * li) / den
    fi = (ai * lr - nr * li) / den
    pr, pi = [jnp.ones_like(ar)], [jnp.zeros_like(ar)]
    for _ in range(CHUNK):
        pr, pi = pr + [pr[-1] * ar - pi[-1] * ai], pi + [pr[-1] * ai + pi[-1] * ar]
    by_step = lambda vals: jnp.concatenate([jnp.broadcast_to(v, (SSM_GROUP, ns2)) for v in vals], axis=0)
    per_step = lambda a: jnp.concatenate([a] * CHUNK, axis=0)
    im_part = lax.broadcasted_iota(jnp.int32, (CHUNK_W, ns2), 1) >= SSM_STATE

    btr, bti = per_step(btr_ref[0]), per_step(bti_ref[0])
    bbr = fr * btr - fi * bti
    bbi = fr * bti + fi * btr
    apr = by_step([pr[CHUNK - 1 - s] for s in range(CHUNK)])
    api = by_step([pi[CHUNK - 1 - s] for s in range(CHUNK)])
    wre = apr * bbr - api * bbi
    wim = apr * bbi + api * bbr
    st = jnp.where(im_part, wim, wre)
    st_sw = jnp.where(im_part, wre, wim)
    st_ref[0] = jnp.concatenate([st, st_sw], axis=1).astype(BF16)

    cr, ci = per_step(cr_ref[0]), per_step(ci_ref[0])
    qr = by_step([pr[t + 1] for t in range(CHUNK)])
    qi = by_step([pi[t + 1] for t in range(CHUNK)])
    rtt_ref[0] = jnp.where(im_part, -(cr * qi + ci * qr), cr * qr - ci * qi).astype(BF16)

    cct = jnp.where(im_part, -ci, cr)
    krw = lax.dot_general(st, cct, (((1,), (1,)), ((), ())), preferred_element_type=F32,
                          precision=lax.Precision.HIGHEST)
    lane_blk = lax.broadcasted_iota(jnp.int32, (CHUNK_W, CHUNK_W), 1) // SSM_GROUP
    mt = jnp.zeros((CHUNK_W, CHUNK_W), F32)
    for t in range(CHUNK):
        sh = SSM_GROUP * (CHUNK - 1 - t)
        if sh == 0:
            shifted = krw
        else:
            shifted = jnp.concatenate([krw[sh:, :], jnp.zeros((sh, CHUNK_W), F32)], axis=0)
        mt = jnp.where(lane_blk == t, shifted, mt)
    mt_ref[0] = mt.astype(BF16)

    im_row = lax.broadcasted_iota(jnp.int32, (1, ns2), 1) >= SSM_STATE
    bco = jnp.where(im_row, pi[CHUNK], -pi[CHUNK])
    coef_ref[0] = jnp.concatenate([pr[CHUNK], bco, jnp.zeros((6, ns2), F32)], axis=0)


def _ssm_prep_call(lam_re, lam_im, log_dt, b_re, b_im, c_re, c_im):
    g, n, p = SSM_GROUPS, SSM_STATE, SSM_GROUP
    f = lambda a: a.astype(F32)
    twice = lambda a: jnp.tile(f(a), (1,) * (a.ndim - 1) + (2,))
    ldt_b = jnp.broadcast_to(f(log_dt)[:, None], (g, n))
    lam = jnp.concatenate([twice(lam_re)[:, None], twice(lam_im)[:, None], twice(ldt_b)[:, None],
                           jnp.zeros((g, 5, 2 * n), F32)], axis=1)
    bt = lambda b: twice(b.transpose(0, 2, 1))
    gspec = lambda shape: pl.BlockSpec((1,) + shape, lambda i: (i, 0, 0))
    return pl.pallas_call(
        _ssm_prep_kernel,
        grid=(g,),
        in_specs=[gspec((8, 2 * n))] + [gspec((p, 2 * n))] * 4,
        out_specs=[gspec((CHUNK_W, 4 * n)), gspec((CHUNK_W, CHUNK_W)), gspec((CHUNK_W, 2 * n)),
                   gspec((8, 2 * n))],
        out_shape=[jax.ShapeDtypeStruct((g, CHUNK_W, 4 * n), BF16),
                   jax.ShapeDtypeStruct((g, CHUNK_W, CHUNK_W), BF16),
                   jax.ShapeDtypeStruct((g, CHUNK_W, 2 * n), BF16),
                   jax.ShapeDtypeStruct((g, 8, 2 * n), F32)],
        compiler_params=_params(1),
        name="ssm_prep",
    )(lam, bt(b_re), bt(b_im), twice(c_re), twice(c_im))


def _rope_tables():
    lane = jnp.arange(LANES) % HEAD_DIM
    half = ROPE_DIM // 2
    inv = ROPE_THETA ** (-jnp.arange(0, ROPE_DIM, 2, dtype=F32) / ROPE_DIM)
    expand = ((lane[None, :] < ROPE_DIM) & (lane[None, :] % half == jnp.arange(half)[:, None])).astype(F32)
    s_lo = jnp.where(lane < half, -1.0, 0.0)
    s_hi = jnp.where((lane >= half) & (lane < ROPE_DIM), 1.0, 0.0)
    unrot = jnp.where(lane >= ROPE_DIM, 1.0, 0.0)
    rows = jnp.concatenate([s_lo[None], s_hi[None], unrot[None], jnp.zeros((5, LANES), F32)], axis=0)
    expand = jnp.concatenate([expand, expand], axis=0).astype(BF16)
    return inv[:, None].astype(F32), expand, rows.astype(F32)


def _sigmoid(v):
    return 1.0 / (1.0 + jnp.exp(-v))


_NAT_DST = {2: NAT_ZS, 3: NAT_ZS + 1, Q_BLK: NAT_Q, K_BLK: NAT_K, V_BLK: NAT_V,
            ZA_BLK: NAT_ZA, 14: NAT_RS, 15: NAT_RS + 1, 16: NAT_RA, 17: NAT_RA + 1}


_PROJ_ORDER = (4, 2, 7, 3, 5, 13, 8, 14, 6, 15, 9, 16, 11, 17, 12, 10)
_N_PERM = 6


def _proj_kernel(x_ref, pos_ref, nw_ref, w_ref, qkw_ref, ones_ref, freq_ref, expand_ref, rope_ref,
                 nat_ref, d4_ref, d16_ref, perm_ref, stage_ref, h_ref):
    tm = x_ref.shape[0]
    x = x_ref[...]
    ms = jnp.mean(x * x, axis=-1, keepdims=True)
    h_ref[...] = (x * lax.rsqrt(ms + EPS) * nw_ref[...]).astype(BF16)
    ang = freq_ref[...] * pos_ref[...].astype(F32)

    def spread(v):
        hi = v.astype(BF16)
        lo = (v - hi.astype(F32)).astype(BF16)
        return lax.dot_general(jnp.concatenate([hi, lo], axis=0), expand_ref[...], (((0,), (0,)), ((), ())),
                               preferred_element_type=F32)

    cosv = spread(jnp.cos(ang)) + rope_ref[2:3, :]
    sinv = spread(jnp.sin(ang))
    s_lo = sinv * rope_ref[0:1, :]
    s_hi = sinv * rope_ref[1:2, :]
    slot = 0
    for j in _PROJ_ORDER:
        acc = jnp.dot(h_ref[...], w_ref[:, j * COL:(j + 1) * COL], preferred_element_type=F32)
        if ZS_BLK <= j < Q_BLK or j == ZA_BLK:
            res = acc * _sigmoid(acc)
        elif j >= RS_BLK:
            res = _sigmoid(acc)
        elif Q_BLK <= j < V_BLK:
            sq = (acc * acc).astype(BF16)
            hw = ones_ref.shape[0]
            ss = jnp.concatenate([jnp.dot(sq[:, c * hw:(c + 1) * hw], ones_ref[...], preferred_element_type=F32)
                                  for c in range(COL // hw)], axis=1)
            y = acc * lax.rsqrt(ss * (1.0 / HEAD_DIM) + EPS) * qkw_ref[j - Q_BLK:j - Q_BLK + 1, :]
            parts = []
            for c in range(COL // LANES):
                yc = y[:, c * LANES:(c + 1) * LANES]
                parts.append(yc * cosv + pltpu.roll(yc, LANES - ROPE_DIM // 2, 1) * s_lo
                             + pltpu.roll(yc, ROPE_DIM // 2, 1) * s_hi)
            res = jnp.concatenate(parts, axis=1)
        else:
            res = acc
        if j in _NAT_DST:
            dst = _NAT_DST[j]
            nat_ref[:, dst * COL:(dst + 1) * COL] = res.astype(BF16)
        else:
            kind, group = divmod(j - Q_BLK, len(DILATIONS))
            d = DILATIONS[group]
            out_ref = d4_ref if group == 1 else d16_ref
            nslab = COL // LANES
            for c in range(nslab):
                perm_ref[slot, c] = res[:, c * LANES:(c + 1) * LANES]
            if d == DILATIONS[2]:
                d1 = DILATIONS[1]
                for c in range(nslab):
                    for r1 in range(d1):
                        stage_ref[kind, c, r1 * (tm // d1):(r1 + 1) * (tm // d1), :] = (
                            perm_ref[slot, c, pl.ds(r1, tm // d1, stride=d1), :])
                for r1, r2 in [(r1, r2) for r1 in range(d1) for r2 in range(d // d1)]:
                    rows = [stage_ref[kind, c, pl.ds(r1 * (tm // d1) + r2, tm // d, stride=d1), :]
                            for c in range(nslab)]
                    out_ref[r1 + d1 * r2, :, kind * COL:(kind + 1) * COL] = (
                        jnp.concatenate(rows, axis=1).astype(BF16))
            else:
                for r in range(d):
                    rows = [perm_ref[slot, c, pl.ds(r, tm // d, stride=d), :] for c in range(nslab)]
                    out_ref[r, :, kind * COL:(kind + 1) * COL] = jnp.concatenate(rows, axis=1).astype(BF16)
            slot += 1


def _proj_call(x, positions, norm_w, w_in, q_norm_w, k_norm_w, tm=512):
    bsz, length, _ = x.shape
    scale = LOG2E / math.sqrt(HEAD_DIM)
    qkw = jnp.concatenate([jnp.tile(q_norm_w.astype(F32) * scale, (1, HEADS)),
                           jnp.tile(k_norm_w.astype(F32), (1, HEADS)),
                           jnp.zeros((2, ATTN_W), F32)], axis=0)
    hid = jnp.arange(MXU_DIM) // HEAD_DIM
    ones = (hid[:, None] == hid[None, :]).astype(BF16)
    d4, d16 = DILATIONS[1], DILATIONS[2]
    freq, expand, rope_rows = _rope_tables()
    return pl.pallas_call(
        _proj_kernel,
        grid=(bsz, length // tm),
        in_specs=[pl.BlockSpec((None, tm, D_MODEL), lambda b, i: (b, i, 0)),
                  pl.BlockSpec((None, 1, tm), lambda b, i: (b, 0, i)),
                  _full_spec((1, D_MODEL)),
                  _full_spec((D_MODEL, IN_WIDTH)),
                  _full_spec((8, ATTN_W)),
                  _full_spec((MXU_DIM, MXU_DIM)),
                  _full_spec((ROPE_DIM // 2, 1)),
                  _full_spec((ROPE_DIM, LANES)),
                  _full_spec((8, LANES))],
        out_specs=[pl.BlockSpec((None, tm, NAT_W), lambda b, i: (b, i, 0)),
                   pl.BlockSpec((None, d4, tm // d4, QKV_W), lambda b, i: (b, 0, i, 0)),
                   pl.BlockSpec((None, d16, tm // d16, QKV_W), lambda b, i: (b, 0, i, 0))],
        out_shape=[jax.ShapeDtypeStruct((bsz, length, NAT_W), BF16),
                   jax.ShapeDtypeStruct((bsz, d4, length // d4, QKV_W), BF16),
                   jax.ShapeDtypeStruct((bsz, d16, length // d16, QKV_W), BF16)],
        scratch_shapes=[pltpu.VMEM((_N_PERM, COL // LANES, tm, LANES), F32),
                        pltpu.VMEM((3, COL // LANES, tm, LANES), F32),
                        pltpu.VMEM((tm, D_MODEL), BF16)],
        compiler_params=_params(2),
        name="proj",
    )(x.astype(F32), positions.astype(jnp.int32)[:, None, :], norm_w.astype(F32)[None, :],
      w_in.astype(BF16), qkw, ones, freq, expand, rope_rows)


def _block_transpose(xs):
    xs = list(xs)
    blk = lax.broadcasted_iota(jnp.int32, xs[0].shape, 1) // SSM_GROUP
    dist = 1
    while dist < LANE_BLKS:
        upper = (blk & dist) != 0
        shift = dist * SSM_GROUP
        for i in range(LANE_BLKS):
            if i & dist:
                continue
            a, b = xs[i], xs[i + dist]
            xs[i] = jnp.where(upper, pltpu.roll(b, shift, 1), a)
            xs[i + dist] = jnp.where(upper, b, pltpu.roll(a, LANES - shift, 1))
        dist *= 2
    return xs


def _ssm_in_kernel(x_ref, nw_ref, w_in_ref, unat_ref, u2_ref, rows_ref, w_ref):
    bsz, tl = x_ref.shape[0], x_ref.shape[1]

    @pl.when(pl.program_id(0) == 0)
    def _():
        w_ref[...] = w_in_ref[...]

    for c in range(tl // CHUNK):
        x = x_ref[:, c * CHUNK:(c + 1) * CHUNK, :].reshape(bsz * CHUNK, D_MODEL)
        ms = jnp.mean(x * x, axis=-1, keepdims=True)
        h = (x * lax.rsqrt(ms + EPS) * nw_ref[...]).astype(BF16)
        u = jnp.dot(h, w_ref[...], preferred_element_type=F32)
        unat_ref[:, c * CHUNK:(c + 1) * CHUNK, :] = u.reshape(bsz, CHUNK, D_MODEL).astype(BF16)
        for b in range(bsz):
            for s8 in range(D_MODEL // LANES):
                rows_ref[c, s8, b * SSM_PITCH:b * SSM_PITCH + CHUNK, :] = (
                    u[b * CHUNK:(b + 1) * CHUNK, s8 * LANES:(s8 + 1) * LANES])
        for s8 in range(D_MODEL // LANES):
            for half in range(CHUNK // LANE_BLKS):
                xs = []
                for k in range(LANE_BLKS):
                    v = rows_ref[c, s8, pl.ds(half * LANE_BLKS + k, bsz, stride=SSM_PITCH), :].astype(BF16)
                    xs.append(pltpu.bitcast(v, jnp.uint32))
                ys = _block_transpose(xs)
                for g in range(LANE_BLKS):
                    u2_ref[s8 * LANE_BLKS + g, c, :, half * LANES:(half + 1) * LANES] = pltpu.bitcast(ys[g], BF16)


def _ssm_in_call(x, norm_w, w_u):
    bsz, length, _ = x.shape
    tl = SSM_CB * CHUNK
    nc = length // CHUNK
    return pl.pallas_call(
        _ssm_in_kernel,
        grid=(length // tl,),
        in_specs=[pl.BlockSpec((bsz, tl, D_MODEL), lambda i: (0, i, 0)),
                  _full_spec((1, D_MODEL)), _full_spec((D_MODEL, D_MODEL))],
        out_specs=[pl.BlockSpec((bsz, tl, D_MODEL), lambda i: (0, i, 0)),
                   pl.BlockSpec((SSM_GROUPS, SSM_CB, bsz, CHUNK_W), lambda i: (0, i, 0, 0))],
        out_shape=[jax.ShapeDtypeStruct((bsz, length, D_MODEL), BF16),
                   jax.ShapeDtypeStruct((SSM_GROUPS, nc, bsz, CHUNK_W), BF16)],
        scratch_shapes=[pltpu.VMEM((SSM_CB, D_MODEL // LANES, bsz * SSM_PITCH, LANES), F32),
                        pltpu.VMEM((D_MODEL, D_MODEL), BF16)],
        compiler_params=_params(1),
        name="ssm_in",
    )(x, norm_w.astype(F32)[None, :], w_u.astype(BF16))


def _ssm_out_kernel(y2_ref, u_ref, zs_ref, rs_ref, dskip_ref, wglu_in_ref, wso_in_ref, o_ref, rows_ref,
                    wglu_ref, wso_ref):
    bsz, tl = u_ref.shape[0], u_ref.shape[1]

    @pl.when(pl.program_id(0) == 0)
    def _():
        wglu_ref[...] = wglu_in_ref[...]
        wso_ref[...] = wso_in_ref[...]

    for c in range(tl // CHUNK):
        for s8 in range(D_MODEL // LANES):
            for half in range(CHUNK // LANE_BLKS):
                xs = [pltpu.bitcast(y2_ref[s8 * LANE_BLKS + g, c, :, half * LANES:(half + 1) * LANES], jnp.uint32)
                      for g in range(LANE_BLKS)]
                ys = _block_transpose(xs)
                for k in range(LANE_BLKS):
                    rows_ref[c, s8, pl.ds(half * LANE_BLKS + k, bsz, stride=SSM_PITCH), :] = (
                        pltpu.bitcast(ys[k], BF16).astype(F32))
        y = jnp.concatenate(
            [jnp.concatenate([rows_ref[c, s8, b * SSM_PITCH:b * SSM_PITCH + CHUNK, :]
                              for s8 in range(D_MODEL // LANES)], axis=1)
             for b in range(bsz)], axis=0)
        tok = slice(c * CHUNK, (c + 1) * CHUNK)
        u = u_ref[:, tok, :].reshape(bsz * CHUNK, D_MODEL).astype(F32)
        zs = zs_ref[:, tok, :].reshape(bsz * CHUNK, D_MODEL).astype(F32)
        y = y + dskip_ref[...] * u
        g = jax.nn.gelu(y, approximate=True)
        gate = _sigmoid(jnp.dot(g.astype(BF16), wglu_ref[...], preferred_element_type=F32))
        ys_in = (g * gate * zs).astype(BF16)
        ys = jnp.dot(ys_in, wso_ref[...], preferred_element_type=F32)
        gated = ys * rs_ref[:, tok, :].reshape(bsz * CHUNK, D_MODEL).astype(F32)
        o_ref[:, tok, :] = gated.reshape(bsz, CHUNK, D_MODEL).astype(BF16)


def _ssm_out_call(y2, unat, nat, d_skip, w_glu, w_ssm_out):
    bsz, length, _ = unat.shape
    tl = SSM_CB * CHUNK
    tok = lambda blk: pl.BlockSpec((bsz, tl, D_MODEL), lambda i: (0, i, blk))
    return pl.pallas_call(
        _ssm_out_kernel,
        grid=(length // tl,),
        in_specs=[pl.BlockSpec((SSM_GROUPS, SSM_CB, bsz, CHUNK_W), lambda i: (0, i, 0, 0)),
                  tok(0), tok(NAT_ZS // 2), tok(NAT_RS // 2),
                  _full_spec((1, D_MODEL)), _full_spec((D_MODEL, D_MODEL)), _full_spec((D_MODEL, D_MODEL))],
        out_specs=tok(0),
        out_shape=jax.ShapeDtypeStruct((bsz, length, D_MODEL), BF16),
        scratch_shapes=[pltpu.VMEM((SSM_CB, D_MODEL // LANES, bsz * SSM_PITCH, LANES), F32),
                        pltpu.VMEM((D_MODEL, D_MODEL), BF16), pltpu.VMEM((D_MODEL, D_MODEL), BF16)],
        compiler_params=_params(1),
        name="ssm_out",
    )(y2, unat, nat, nat, d_skip.astype(F32)[None, :], w_glu.astype(BF16), w_ssm_out.astype(BF16))


def _ssm_kernel(u_ref, st_ref, mt_ref, rtt_ref, coef_ref, y_ref, hloc_ref, hprev_ref):
    ng, nc, bsz = u_ref.shape[0], u_ref.shape[1], u_ref.shape[2]
    ns2 = 2 * SSM_STATE
    us, coefs = [], []
    for g in range(ng):
        u = u_ref[g].reshape(nc * bsz, CHUNK_W)
        us.append(u)
        half = nc * bsz // 2
        hloc_ref[g, :half, :] = jnp.dot(u[:half], st_ref[g], preferred_element_type=F32)
        hloc_ref[g, half:, :] = jnp.dot(u[half:], st_ref[g], preferred_element_type=F32)
        coefs.append((jnp.broadcast_to(coef_ref[g, 0:1, :], (bsz, ns2)),
                      jnp.broadcast_to(coef_ref[g, 1:2, :], (bsz, ns2))))

    def step(c, carry):
        r = pl.multiple_of(c * bsz, bsz)
        out = []
        for g in range(ng):
            hx, hy = carry[2 * g], carry[2 * g + 1]
            a, b = coefs[g]
            hprev_ref[g, pl.ds(r, bsz), :] = hx.astype(BF16)
            px = hloc_ref[g, pl.ds(r, bsz), 0:ns2]
            py = hloc_ref[g, pl.ds(r, bsz), ns2:2 * ns2]
            out += [a * hx + b * hy + px, a * hy - b * hx + py]
        return tuple(out)

    zero = jnp.zeros((bsz, ns2), F32)
    lax.fori_loop(0, nc, step, (zero,) * (2 * ng), unroll=8)
    for g in range(ng):
        y = jnp.dot(us[g], mt_ref[g], preferred_element_type=F32)
        y = y + lax.dot_general(hprev_ref[g], rtt_ref[g], (((1,), (1,)), ((), ())), preferred_element_type=F32)
        y_ref[g] = y.reshape(nc, bsz, CHUNK_W).astype(BF16)


def _ssm_call(u2, st, mt, rt, coef):
    g, nc, bsz, _ = u2.shape
    ng = SCAN_GROUPS
    gspec = lambda shape: pl.BlockSpec((ng,) + shape, lambda i: (i,) + (0,) * len(shape))
    return pl.pallas_call(
        _ssm_kernel,
        grid=(g // ng,),
        in_specs=[gspec((nc, bsz, CHUNK_W)), gspec((CHUNK_W, 4 * SSM_STATE)), gspec((CHUNK_W, CHUNK_W)),
                  gspec((CHUNK_W, 2 * SSM_STATE)), gspec((8, 2 * SSM_STATE))],
        out_specs=gspec((nc, bsz, CHUNK_W)),
        out_shape=jax.ShapeDtypeStruct(u2.shape, BF16),
        scratch_shapes=[pltpu.VMEM((ng, nc * bsz, 4 * SSM_STATE), F32),
                        pltpu.VMEM((ng, nc * bsz, 2 * SSM_STATE), BF16)],
        compiler_params=_params(1),
        name="ssm_scan",
    )(u2, st, mt, rt, coef)


def _attn_kernel(q_ref, kc_ref, kp_ref, vc_ref, vp_ref, o_ref, lse_ref):
    nres, tq = q_ref.shape[0], q_ref.shape[1]
    qb = 128
    slab = pl.program_id(2)
    qi = lax.broadcasted_iota(jnp.int32, (qb, 2 * qb), 0)
    kk = lax.broadcasted_iota(jnp.int32, (qb, 2 * qb), 1)
    band = jnp.where((kk >= qi) & (kk <= qi + WINDOW_KEYS), 0.0, NEG_INF).astype(F32)
    band0 = band + jnp.where(jnp.logical_and(slab == 0, kk < qb), NEG_INF, 0.0).astype(F32)
    band = jnp.concatenate([band, band], axis=0)
    band0 = jnp.concatenate([band0, band0], axis=0)
    lo_q = lax.broadcasted_iota(jnp.int32, (qb, LANES), 1) < HEAD_DIM
    ones_kv = jnp.ones((2 * qb, LANES), BF16)

    head_slot = lax.broadcasted_iota(jnp.int32, (qb, LANES), 1) // LSE_LANES

    for r, j in [(r, j) for r in range(nres) for j in range(tq // qb)]:
        madd = band0 if j == 0 else band
        rows = slice(j * qb, (j + 1) * qb)
        m_all = jnp.zeros((qb, LANES), F32)
        den_all = jnp.ones((qb, LANES), F32)
        for hp in range(ATTN_W // LANES):
            cs = slice(hp * LANES, (hp + 1) * LANES)
            qp = q_ref[r, rows, cs]
            if j == 0:
                kp = jnp.concatenate([kp_ref[r, :, cs], kc_ref[r, 0:qb, cs]], axis=0)
                vp = jnp.concatenate([vp_ref[r, :, cs], vc_ref[r, 0:qb, cs]], axis=0)
            else:
                kp = kc_ref[r, (j - 1) * qb:(j + 1) * qb, cs]
                vp = vc_ref[r, (j - 1) * qb:(j + 1) * qb, cs]
            zero = jnp.zeros_like(qp)
            q2 = jnp.concatenate([jnp.where(lo_q, qp, zero), jnp.where(lo_q, zero, qp)], axis=0)
            s = lax.dot_general(q2, kp, (((1,), (1,)), ((), ())), preferred_element_type=F32)
            s = s + madd
            m = jnp.max(s, axis=1, keepdims=True)
            p = jnp.exp2(s - m)
            pv = jnp.dot(p.astype(BF16), jnp.concatenate([vp, ones_kv], axis=1), preferred_element_type=F32)
            num = jnp.where(lo_q, pv[:qb, :LANES], pv[qb:, :LANES])
            den = jnp.where(lo_q, pv[:qb, LANES:], pv[qb:, LANES:])
            o_ref[r, rows, cs] = (num / den).astype(BF16)
            for side in range(2):
                slot = head_slot == 2 * hp + side
                half = slice(side * qb, (side + 1) * qb)
                m_all = jnp.where(slot, m[half], m_all)
                den_all = jnp.where(slot, pv[half, LANES:], den_all)
        lse_ref[r, rows, :] = m_all + jnp.log2(den_all)


def _attn_call(qkv, group, blocks, rows_per_step=1024):
    bsz, d, ld, _ = qkv.shape
    tq = min(rows_per_step, ld)
    nres = min(d, rows_per_step // tq)
    nslab = ld // tq
    per = tq // 128
    cur = lambda blk: pl.BlockSpec((None, nres, tq, COL), lambda b, r, i: (b, r, i, blk))
    prev = lambda blk: pl.BlockSpec(
        (None, nres, 128, COL), lambda b, r, i: (b, r, jnp.maximum(i * per - 1, 0), blk))
    ospec = lambda w: pl.BlockSpec((None, nres, tq, w), lambda b, r, i: (b, r, i, 0))
    qb, kb, vb = blocks
    return pl.pallas_call(
        _attn_kernel,
        grid=(bsz, d // nres, nslab),
        in_specs=[cur(qb), cur(kb), prev(kb), cur(vb), prev(vb)],
        out_specs=[ospec(ATTN_W), ospec(LANES)],
        out_shape=[jax.ShapeDtypeStruct((bsz, d, ld, ATTN_W), BF16),
                   jax.ShapeDtypeStruct((bsz, d, ld, LANES), F32)],
        compiler_params=_params(3),
        name=f"attn_d{DILATIONS[group]}",
    )(qkv, qkv, qkv, qkv, qkv)


def _merge_kernel(x_ref, ys_ref, za_ref, ra_ref,
                  a0_ref, a1_ref, a2_ref, l0_ref, l1_ref, l2_ref,
                  spread_in_ref, wao_in_ref, wo_in_ref, o_ref, il_ref, stage_ref, spread_ref, wao_ref, wo_ref):
    tm = x_ref.shape[0]
    d1 = DILATIONS[1]

    @pl.when(jnp.logical_and(pl.program_id(0) == 0, pl.program_id(1) == 0))
    def _():
        spread_ref[...] = spread_in_ref[...]
        wao_ref[...] = wao_in_ref[...]
        wo_ref[...] = wo_in_ref[...]


    def to_token_order(ref, base, stage_base=None):
        d, nslab = ref.shape[0], ref.shape[2] // LANES
        for r in range(d):
            blk = ref[r].astype(F32)
            for c in range(nslab):
                piece = blk[:, c * LANES:(c + 1) * LANES]
                if d == d1:
                    il_ref[base + c, pl.ds(r, tm // d, stride=d), :] = piece
                else:
                    r1, r2 = r % d1, r // d1
                    stage_ref[stage_base + c, pl.ds(r1 * (tm // d1) + r2, tm // d, stride=d1), :] = piece
        if d != d1:
            for c in range(nslab):
                for r1 in range(d1):
                    il_ref[base + c, pl.ds(r1, tm // d1, stride=d1), :] = (
                        stage_ref[stage_base + c, r1 * (tm // d1):(r1 + 1) * (tm // d1), :])
        return lambda rows: jnp.concatenate([il_ref[base + c, rows, :] for c in range(nslab)], axis=1)

    def per_head_to_lanes(w):
        hi = w.astype(BF16)
        lo = (w - hi.astype(F32)).astype(BF16)
        return jnp.dot(jnp.concatenate([hi, lo], axis=1), spread_ref[...], preferred_element_type=F32)

    wide_slabs = ATTN_W // LANES
    l1_at, l2_at = to_token_order(l1_ref, 0), to_token_order(l2_ref, 1, 0)
    a1_at, a2_at = to_token_order(a1_ref, 2), to_token_order(a2_ref, 2 + wide_slabs, 1)
    rows = slice(None)
    l0, l1, l2 = l0_ref[...], l1_at(rows), l2_at(rows)
    lm = jnp.maximum(jnp.maximum(l0, l1), l2)
    e0, e1, e2 = jnp.exp2(l0 - lm), jnp.exp2(l1 - lm), jnp.exp2(l2 - lm)
    inv = 1.0 / (e0 + e1 + e2)
    att = (per_head_to_lanes(e0 * inv) * a0_ref[...].astype(F32)
           + per_head_to_lanes(e1 * inv) * a1_at(rows)
           + per_head_to_lanes(e2 * inv) * a2_at(rows))
    ya_in = (att * za_ref[...].astype(F32)).astype(BF16)
    ya = jnp.dot(ya_in, wao_ref[...], preferred_element_type=F32)
    m = ys_ref[...].astype(F32) + ra_ref[...].astype(F32) * ya
    o_ref[...] = x_ref[...] + jnp.dot(m.astype(BF16), wo_ref[...], preferred_element_type=F32)


def _merge_call(x, ys, nat, attn, w_attn_out, w_o, tm=1024):
    bsz, length, _ = x.shape
    wide = lambda blk: pl.BlockSpec((None, tm, D_MODEL), lambda b, i: (b, i, blk))
    half = lambda blk: pl.BlockSpec((None, tm, ATTN_W), lambda b, i: (b, i, blk))
    res = lambda d, w: pl.BlockSpec((None, d, tm // d, w), lambda b, i: (b, 0, i, 0))
    (a0, l0), (a1, l1), (a2, l2) = attn
    d4, d16 = DILATIONS[1], DILATIONS[2]
    a0, l0 = a0.reshape(bsz, length, ATTN_W), l0.reshape(bsz, length, LANES)
    spread = (jnp.arange(LANES)[:, None] == (jnp.arange(ATTN_W)[None, :] // HEAD_DIM) * LSE_LANES).astype(BF16)
    spread = jnp.concatenate([spread, spread], axis=0)
    return pl.pallas_call(
        _merge_kernel,
        grid=(bsz, length // tm),
        in_specs=[wide(0), wide(0), half(NAT_ZA), wide(NAT_RA // 2),
                  half(0), res(d4, ATTN_W), res(d16, ATTN_W),
                  pl.BlockSpec((None, tm, LANES), lambda b, i: (b, i, 0)), res(d4, LANES), res(d16, LANES),
                  _full_spec((2 * LANES, ATTN_W)), _full_spec((ATTN_W, D_MODEL)), _full_spec((D_MODEL, D_MODEL))],
        out_specs=wide(0),
        out_shape=jax.ShapeDtypeStruct((bsz, length, D_MODEL), F32),
        scratch_shapes=[pltpu.VMEM((2 + 2 * (ATTN_W // LANES), tm, LANES), F32),
                        pltpu.VMEM((1 + ATTN_W // LANES, tm, LANES), F32),
                        pltpu.VMEM((2 * LANES, ATTN_W), BF16), pltpu.VMEM((ATTN_W, D_MODEL), BF16),
                        pltpu.VMEM((D_MODEL, D_MODEL), BF16)],
        compiler_params=_params(2),
        name="merge",
    )(x, ys, nat, nat, a0, a1, a2, l0, l1, l2, spread, w_attn_out.astype(BF16), w_o.astype(BF16))


def kernel(x, positions, norm_w, w_in, lam_re, lam_im, log_dt, b_re, b_im, c_re, c_im, d_skip, w_glu,
           q_norm_w, k_norm_w, w_ssm_out, w_attn_out, w_o):
    bsz, length, width = x.shape
    assert width == D_MODEL and w_in.shape[1:] == (D_MODEL, IN_WIDTH)
    assert bsz % 16 == 0, "the scan layout packs the batch into 16-row bf16 tiles"
    assert length % (128 * DILATIONS[-1]) == 0, "every residue class must hold whole 128-query blocks"
    xf = x.astype(F32)
    for layer in range(norm_w.shape[0]):
        st, mt, rt, coef = _ssm_prep_call(lam_re[layer], lam_im[layer], log_dt[layer], b_re[layer],
                                          b_im[layer], c_re[layer], c_im[layer])
        nat, qkv4, qkv16 = _proj_call(xf, positions, norm_w[layer], w_in[layer], q_norm_w[layer],
                                      k_norm_w[layer])
        unat, u2 = _ssm_in_call(xf, norm_w[layer], w_in[layer][:, :D_MODEL])
        y2 = _ssm_call(u2, st, mt, rt, coef)
        ys = _ssm_out_call(y2, unat, nat, d_skip[layer], w_glu[layer], w_ssm_out[layer])
        attn = [_attn_call(nat[:, None], 0, (NAT_Q, NAT_K, NAT_V)),
                _attn_call(qkv4, 1, (0, 1, 2)),
                _attn_call(qkv16, 2, (0, 1, 2))]
        xf = _merge_call(xf, ys, nat, attn, w_attn_out[layer], w_o[layer])
    return xf.astype(x.dtype)
```

```python
import math

import jax
import jax.numpy as jnp
from jax import lax
from jax.experimental import pallas as pl
from jax.experimental.pallas import tpu as pltpu

F32 = jnp.float32
BF16 = jnp.bfloat16

D_MODEL = 1024
SSM_GROUP = 16
SSM_GROUPS = D_MODEL // SSM_GROUP
SSM_STATE = 64
CHUNK = 16
CHUNK_W = CHUNK * SSM_GROUP
HEAD_DIM = 64
HEADS = 8
ATTN_W = HEADS * HEAD_DIM
DILATIONS = (1, 4, 16)
WINDOW_KEYS = 128
ROPE_DIM = HEAD_DIM // 4
ROPE_THETA = 500000.0
EPS = 1e-6
NEG_INF = -1e30
LOG2E = math.log2(math.e)
GELU_K1 = -2.0 * math.sqrt(2.0 / math.pi) * LOG2E
GELU_K3 = GELU_K1 * 0.044715
COL = 512
ZS_BLK, Q_BLK, K_BLK, V_BLK, ZA_BLK, RS_BLK, RA_BLK = 2, 4, 7, 10, 13, 14, 16
N_BLK = 18
IN_WIDTH = N_BLK * COL
NAT_ZS, NAT_Q, NAT_K, NAT_V, NAT_ZA, NAT_RS, NAT_RA = 0, 2, 3, 4, 5, 6, 8
NAT_W = 10 * COL
QKV_W = 3 * COL
LANES = 128
MXU_DIM = 256
LSE_LANES = LANES // HEADS
LANE_BLKS = LANES // SSM_GROUP
SCAN_GROUPS = 2
SSM_CB = 4
SSM_PITCH = 8 * 3
VMEM_LIMIT = 56 * 1024 * 1024


def _params(grid_rank):
    return pltpu.CompilerParams(dimension_semantics=("arbitrary",) * grid_rank, vmem_limit_bytes=VMEM_LIMIT)


def _full_spec(shape):
    nd = len(shape)
    return pl.BlockSpec(shape, lambda *_: (0,) * nd, pipeline_mode=pl.Buffered(1))


def _ssm_prep_kernel(lam_ref, btr_ref, bti_ref, cr_ref, ci_ref, st_ref, mt_ref, rtt_ref, coef_ref):
    ns2 = 2 * SSM_STATE
    lam = lam_ref[0]
    lr, li, dt = lam[0:1, :], lam[1:2, :], jnp.exp(lam[2:3, :])
    mag = jnp.exp(lr * dt)
    ar = mag * jnp.cos(li * dt)
    ai = mag * jnp.sin(li * dt)
    den = lr * lr + li * li
    nr = ar - 1.0
    fr = (nr * lr + ai * li) / den
    fi = (ai * lr - nr * li) / den
    pr, pi = [jnp.ones_like(ar)], [jnp.zeros_like(ar)]
    for _ in range(CHUNK):
        pr, pi = pr + [pr[-1] * ar - pi[-1] * ai], pi + [pr[-1] * ai + pi[-1] * ar]
    by_step = lambda vals: jnp.concatenate([jnp.broadcast_to(v, (SSM_GROUP, ns2)) for v in vals], axis=0)
    per_step = lambda a: jnp.concatenate([a] * CHUNK, axis=0)
    im_part = lax.broadcasted_iota(jnp.int32, (CHUNK_W, ns2), 1) >= SSM_STATE

    btr, bti = per_step(btr_ref[0]), per_step(bti_ref[0])
    bbr = fr * btr - fi * bti
    bbi = fr * bti + fi * btr
    apr = by_step([pr[CHUNK - 1 - s] for s in range(CHUNK)])
    api = by_step([pi[CHUNK - 1 - s] for s in range(CHUNK)])
    wre = apr * bbr - api * bbi
    wim = apr * bbi + api * bbr
    st = jnp.where(im_part, wim, wre)
    st_sw = jnp.where(im_part, wre, wim)
    st_ref[0] = jnp.concatenate([st, st_sw], axis=1).astype(BF16)

    cr, ci = per_step(cr_ref[0]), per_step(ci_ref[0])
    qr = by_step([pr[t + 1] for t in range(CHUNK)])
    qi = by_step([pi[t + 1] for t in range(CHUNK)])
    rtt_ref[0] = jnp.where(im_part, -(cr * qi + ci * qr), cr * qr - ci * qi).astype(BF16)

    cct = jnp.where(im_part, -ci, cr)
    split = lambda v: (v.astype(BF16), (v - v.astype(BF16).astype(F32)).astype(BF16))
    (st_hi, st_lo), (cc_hi, cc_lo) = split(st), split(cct)
    nt = (((1,), (1,)), ((), ()))
    krw = (lax.dot_general(jnp.concatenate([st_hi, st_lo], axis=1), jnp.concatenate([cc_hi, cc_hi], axis=1), nt,
                           preferred_element_type=F32)
           + lax.dot_general(st_hi, cc_lo, nt, preferred_element_type=F32))
    lane_blk = lax.broadcasted_iota(jnp.int32, (CHUNK_W, CHUNK_W), 1) // SSM_GROUP
    mt = jnp.zeros((CHUNK_W, CHUNK_W), F32)
    for t in range(CHUNK):
        sh = SSM_GROUP * (CHUNK - 1 - t)
        if sh == 0:
            shifted = krw
        else:
            shifted = jnp.concatenate([krw[sh:, :], jnp.zeros((sh, CHUNK_W), F32)], axis=0)
        mt = jnp.where(lane_blk == t, shifted, mt)
    mt_ref[0] = mt.astype(BF16)

    im_row = lax.broadcasted_iota(jnp.int32, (1, ns2), 1) >= SSM_STATE
    bco = jnp.where(im_row, pi[CHUNK], -pi[CHUNK])
    coef_ref[0] = jnp.concatenate([pr[CHUNK], bco, jnp.zeros((6, ns2), F32)], axis=0)


def _ssm_prep_call(lam_re, lam_im, log_dt, b_re, b_im, c_re, c_im):
    g, n, p = SSM_GROUPS, SSM_STATE, SSM_GROUP
    f = lambda a: a.astype(F32)
    twice = lambda a: jnp.tile(f(a), (1,) * (a.ndim - 1) + (2,))
    ldt_b = jnp.broadcast_to(f(log_dt)[:, None], (g, n))
    lam = jnp.concatenate([twice(lam_re)[:, None], twice(lam_im)[:, None], twice(ldt_b)[:, None],
                           jnp.zeros((g, 5, 2 * n), F32)], axis=1)
    bt = lambda b: twice(b.transpose(0, 2, 1))
    gspec = lambda shape: pl.BlockSpec((1,) + shape, lambda i: (i, 0, 0))
    return pl.pallas_call(
        _ssm_prep_kernel,
        grid=(g,),
        in_specs=[gspec((8, 2 * n))] + [gspec((p, 2 * n))] * 4,
        out_specs=[gspec((CHUNK_W, 4 * n)), gspec((CHUNK_W, CHUNK_W)), gspec((CHUNK_W, 2 * n)),
                   gspec((8, 2 * n))],
        out_shape=[jax.ShapeDtypeStruct((g, CHUNK_W, 4 * n), BF16),
                   jax.ShapeDtypeStruct((g, CHUNK_W, CHUNK_W), BF16),
                   jax.ShapeDtypeStruct((g, CHUNK_W, 2 * n), BF16),
                   jax.ShapeDtypeStruct((g, 8, 2 * n), F32)],
        compiler_params=_params(1),
        name="ssm_prep",
    )(lam, bt(b_re), bt(b_im), twice(c_re), twice(c_im))


def _rope_tables():
    lane = jnp.arange(LANES) % HEAD_DIM
    half = ROPE_DIM // 2
    inv = ROPE_THETA ** (-jnp.arange(0, ROPE_DIM, 2, dtype=F32) / ROPE_DIM)
    expand = ((lane[None, :] < ROPE_DIM) & (lane[None, :] % half == jnp.arange(half)[:, None])).astype(F32)
    s_lo = jnp.where(lane < half, -1.0, 0.0)
    s_hi = jnp.where((lane >= half) & (lane < ROPE_DIM), 1.0, 0.0)
    unrot = jnp.where(lane >= ROPE_DIM, 1.0, 0.0)
    rows = jnp.concatenate([s_lo[None], s_hi[None], unrot[None], jnp.zeros((5, LANES), F32)], axis=0)
    expand = jnp.concatenate([expand, expand], axis=0).astype(BF16)
    return inv[:, None].astype(F32), expand, rows.astype(F32)


def _sigmoid(v):
    return 1.0 / (1.0 + jnp.exp(-v))


_NAT_DST = {2: NAT_ZS, 3: NAT_ZS + 1, Q_BLK: NAT_Q, K_BLK: NAT_K, V_BLK: NAT_V,
            ZA_BLK: NAT_ZA, 14: NAT_RS, 15: NAT_RS + 1, 16: NAT_RA, 17: NAT_RA + 1}


_PROJ_ORDER = (2, 4, 3, 7, 13, 5, 14, 8, 15, 6, 16, 9, 17, 11, 12, 10)
_N_PERM = 6


def _proj_kernel(x_ref, pos_ref, nw_ref, w_ref, qkw_ref, ones_ref, freq_ref, expand_ref, rope_ref,
                 nat_ref, d4_ref, d16_ref, perm_ref, stage_ref, h_ref):
    tm = x_ref.shape[0]
    x = x_ref[...]
    ms = jnp.mean(x * x, axis=-1, keepdims=True)
    h_ref[...] = (x * lax.rsqrt(ms + EPS) * nw_ref[...]).astype(BF16)
    ang = freq_ref[...] * pos_ref[...].astype(F32)

    def spread(v):
        hi = v.astype(BF16)
        lo = (v - hi.astype(F32)).astype(BF16)
        return lax.dot_general(jnp.concatenate([hi, lo], axis=0), expand_ref[...], (((0,), (0,)), ((), ())),
                               preferred_element_type=F32)

    cosv = spread(jnp.cos(ang)) + rope_ref[2:3, :]
    sinv = spread(jnp.sin(ang))
    s_lo = sinv * rope_ref[0:1, :]
    s_hi = sinv * rope_ref[1:2, :]
    slot = 0
    for j in _PROJ_ORDER:
        acc = jnp.dot(h_ref[...], w_ref[:, j * COL:(j + 1) * COL], preferred_element_type=F32)
        if ZS_BLK <= j < Q_BLK or j == ZA_BLK:
            res = acc * _sigmoid(acc)
        elif j >= RS_BLK:
            res = _sigmoid(acc)
        elif Q_BLK <= j < V_BLK:
            sq = (acc * acc).astype(BF16)
            hw = ones_ref.shape[0]
            ss = jnp.concatenate([jnp.dot(sq[:, c * hw:(c + 1) * hw], ones_ref[...], preferred_element_type=F32)
                                  for c in range(COL // hw)], axis=1)
            y = acc * lax.rsqrt(ss * (1.0 / HEAD_DIM) + EPS) * qkw_ref[j - Q_BLK:j - Q_BLK + 1, :]
            parts = []
            for c in range(COL // LANES):
                yc = y[:, c * LANES:(c + 1) * LANES]
                parts.append(yc * cosv + pltpu.roll(yc, LANES - ROPE_DIM // 2, 1) * s_lo
                             + pltpu.roll(yc, ROPE_DIM // 2, 1) * s_hi)
            res = jnp.concatenate(parts, axis=1)
        else:
            res = acc
        if j in _NAT_DST:
            dst = _NAT_DST[j]
            nat_ref[:, dst * COL:(dst + 1) * COL] = res.astype(BF16)
        else:
            kind, group = divmod(j - Q_BLK, len(DILATIONS))
            d = DILATIONS[group]
            out_ref = d4_ref if group == 1 else d16_ref
            nslab = COL // LANES
            for c in range(nslab):
                perm_ref[slot, c] = res[:, c * LANES:(c + 1) * LANES]
            if d == DILATIONS[2]:
                d1 = DILATIONS[1]
                for c in range(nslab):
                    for r1 in range(d1):
                        stage_ref[kind, c, r1 * (tm // d1):(r1 + 1) * (tm // d1), :] = (
                            perm_ref[slot, c, pl.ds(r1, tm // d1, stride=d1), :])
                for r1, r2 in [(r1, r2) for r1 in range(d1) for r2 in range(d // d1)]:
                    rows = [stage_ref[kind, c, pl.ds(r1 * (tm // d1) + r2, tm // d, stride=d1), :]
                            for c in range(nslab)]
                    out_ref[r1 + d1 * r2, :, kind * COL:(kind + 1) * COL] = (
                        jnp.concatenate(rows, axis=1).astype(BF16))
            else:
                for r in range(d):
                    rows = [perm_ref[slot, c, pl.ds(r, tm // d, stride=d), :] for c in range(nslab)]
                    out_ref[r, :, kind * COL:(kind + 1) * COL] = jnp.concatenate(rows, axis=1).astype(BF16)
            slot += 1


def _proj_call(x, positions, norm_w, w_in, q_norm_w, k_norm_w, tm=512):
    bsz, length, _ = x.shape
    scale = LOG2E / math.sqrt(HEAD_DIM)
    qkw = jnp.concatenate([jnp.tile(q_norm_w.astype(F32) * scale, (1, HEADS)),
                           jnp.tile(k_norm_w.astype(F32), (1, HEADS)),
                           jnp.zeros((2, ATTN_W), F32)], axis=0)
    hid = jnp.arange(MXU_DIM) // HEAD_DIM
    ones = (hid[:, None] == hid[None, :]).astype(BF16)
    d4, d16 = DILATIONS[1], DILATIONS[2]
    freq, expand, rope_rows = _rope_tables()
    return pl.pallas_call(
        _proj_kernel,
        grid=(bsz, length // tm),
        in_specs=[pl.BlockSpec((None, tm, D_MODEL), lambda b, i: (b, i, 0)),
                  pl.BlockSpec((None, 1, tm), lambda b, i: (b, 0, i)),
                  _full_spec((1, D_MODEL)),
                  _full_spec((D_MODEL, IN_WIDTH)),
                  _full_spec((8, ATTN_W)),
                  _full_spec((MXU_DIM, MXU_DIM)),
                  _full_spec((ROPE_DIM // 2, 1)),
                  _full_spec((ROPE_DIM, LANES)),
                  _full_spec((8, LANES))],
        out_specs=[pl.BlockSpec((None, tm, NAT_W), lambda b, i: (b, i, 0)),
                   pl.BlockSpec((None, d4, tm // d4, QKV_W), lambda b, i: (b, 0, i, 0)),
                   pl.BlockSpec((None, d16, tm // d16, QKV_W), lambda b, i: (b, 0, i, 0))],
        out_shape=[jax.ShapeDtypeStruct((bsz, length, NAT_W), BF16),
                   jax.ShapeDtypeStruct((bsz, d4, length // d4, QKV_W), BF16),
                   jax.ShapeDtypeStruct((bsz, d16, length // d16, QKV_W), BF16)],
        scratch_shapes=[pltpu.VMEM((_N_PERM, COL // LANES, tm, LANES), F32),
                        pltpu.VMEM((3, COL // LANES, tm, LANES), F32),
                        pltpu.VMEM((tm, D_MODEL), BF16)],
        compiler_params=_params(2),
        name="proj",
    )(x.astype(F32), positions.astype(jnp.int32)[:, None, :], norm_w.astype(F32)[None, :],
      w_in.astype(BF16), qkw, ones, freq, expand, rope_rows)


def _block_transpose(xs):
    xs = list(xs)
    blk = lax.broadcasted_iota(jnp.int32, xs[0].shape, 1) // SSM_GROUP
    dist = 1
    while dist < LANE_BLKS:
        upper = (blk & dist) != 0
        shift = dist * SSM_GROUP
        for i in range(LANE_BLKS):
            if i & dist:
                continue
            a, b = xs[i], xs[i + dist]
            xs[i] = jnp.where(upper, pltpu.roll(b, shift, 1), a)
            xs[i + dist] = jnp.where(upper, b, pltpu.roll(a, LANES - shift, 1))
        dist *= 2
    return xs


def _ssm_in_kernel(x_ref, nw_ref, w_ref, unat_ref, u2_ref, rows_ref):
    bsz, tl = x_ref.shape[0], x_ref.shape[1]
    for c in range(tl // CHUNK):
        x = x_ref[:, c * CHUNK:(c + 1) * CHUNK, :].reshape(bsz * CHUNK, D_MODEL)
        ms = jnp.mean(x * x, axis=-1, keepdims=True)
        h = (x * lax.rsqrt(ms + EPS) * nw_ref[...]).astype(BF16)
        u = jnp.dot(h, w_ref[...], preferred_element_type=F32)
        unat_ref[:, c * CHUNK:(c + 1) * CHUNK, :] = u.reshape(bsz, CHUNK, D_MODEL).astype(BF16)
        for b in range(bsz):
            for s8 in range(D_MODEL // LANES):
                rows_ref[c, s8, b * SSM_PITCH:b * SSM_PITCH + CHUNK, :] = (
                    u[b * CHUNK:(b + 1) * CHUNK, s8 * LANES:(s8 + 1) * LANES])
        for s8 in range(D_MODEL // LANES):
            for half in range(CHUNK // LANE_BLKS):
                xs = []
                for k in range(LANE_BLKS):
                    v = rows_ref[c, s8, pl.ds(half * LANE_BLKS + k, bsz, stride=SSM_PITCH), :].astype(BF16)
                    xs.append(pltpu.bitcast(v, jnp.uint32))
                ys = _block_transpose(xs)
                for g in range(LANE_BLKS):
                    u2_ref[s8 * LANE_BLKS + g, c, :, half * LANES:(half + 1) * LANES] = pltpu.bitcast(ys[g], BF16)


def _ssm_in_call(x, norm_w, w_u):
    bsz, length, _ = x.shape
    tl = SSM_CB * CHUNK
    nc = length // CHUNK
    return pl.pallas_call(
        _ssm_in_kernel,
        grid=(length // tl,),
        in_specs=[pl.BlockSpec((bsz, tl, D_MODEL), lambda i: (0, i, 0)),
                  _full_spec((1, D_MODEL)), _full_spec((D_MODEL, D_MODEL))],
        out_specs=[pl.BlockSpec((bsz, tl, D_MODEL), lambda i: (0, i, 0)),
                   pl.BlockSpec((SSM_GROUPS, SSM_CB, bsz, CHUNK_W), lambda i: (0, i, 0, 0))],
        out_shape=[jax.ShapeDtypeStruct((bsz, length, D_MODEL), BF16),
                   jax.ShapeDtypeStruct((SSM_GROUPS, nc, bsz, CHUNK_W), BF16)],
        scratch_shapes=[pltpu.VMEM((SSM_CB, D_MODEL // LANES, bsz * SSM_PITCH, LANES), F32)],
        compiler_params=_params(1),
        name="ssm_in",
    )(x, norm_w.astype(F32)[None, :], w_u.astype(BF16))


def _ssm_out_kernel(y2_ref, u_ref, zs_ref, rs_ref, dskip_ref, wglu_ref, wso_ref, o_ref, rows_ref):
    bsz, tl = u_ref.shape[0], u_ref.shape[1]
    for c in range(tl // CHUNK):
        for s8 in range(D_MODEL // LANES):
            for half in range(CHUNK // LANE_BLKS):
                xs = [pltpu.bitcast(y2_ref[s8 * LANE_BLKS + g, c, :, half * LANES:(half + 1) * LANES], jnp.uint32)
                      for g in range(LANE_BLKS)]
                ys = _block_transpose(xs)
                for k in range(LANE_BLKS):
                    rows_ref[c, s8, pl.ds(half * LANE_BLKS + k, bsz, stride=SSM_PITCH), :] = (
                        pltpu.bitcast(ys[k], BF16).astype(F32))
        y = jnp.concatenate(
            [jnp.concatenate([rows_ref[c, s8, b * SSM_PITCH:b * SSM_PITCH + CHUNK, :]
                              for s8 in range(D_MODEL // LANES)], axis=1)
             for b in range(bsz)], axis=0)
        tok = slice(c * CHUNK, (c + 1) * CHUNK)
        u = u_ref[:, tok, :].reshape(bsz * CHUNK, D_MODEL).astype(F32)
        zs = zs_ref[:, tok, :].reshape(bsz * CHUNK, D_MODEL).astype(F32)
        y = y + dskip_ref[...] * u
        g = y / (1.0 + jnp.exp2(y * (GELU_K1 + GELU_K3 * (y * y))))
        gate = _sigmoid(jnp.dot(g.astype(BF16), wglu_ref[...], preferred_element_type=F32))
        ys_in = (g * gate * zs).astype(BF16)
        ys = jnp.dot(ys_in, wso_ref[...], preferred_element_type=F32)
        gated = ys * rs_ref[:, tok, :].reshape(bsz * CHUNK, D_MODEL).astype(F32)
        o_ref[:, tok, :] = gated.reshape(bsz, CHUNK, D_MODEL).astype(BF16)


def _ssm_out_call(y2, unat, nat, d_skip, w_glu, w_ssm_out):
    bsz, length, _ = unat.shape
    tl = SSM_CB * CHUNK
    tok = lambda blk: pl.BlockSpec((bsz, tl, D_MODEL), lambda i: (0, i, blk))
    return pl.pallas_call(
        _ssm_out_kernel,
        grid=(length // tl,),
        in_specs=[pl.BlockSpec((SSM_GROUPS, SSM_CB, bsz, CHUNK_W), lambda i: (0, i, 0, 0)),
                  tok(0), tok(NAT_ZS // 2), tok(NAT_RS // 2),
                  _full_spec((1, D_MODEL)), _full_spec((D_MODEL, D_MODEL)), _full_spec((D_MODEL, D_MODEL))],
        out_specs=tok(0),
        out_shape=jax.ShapeDtypeStruct((bsz, length, D_MODEL), BF16),
        scratch_shapes=[pltpu.VMEM((SSM_CB, D_MODEL // LANES, bsz * SSM_PITCH, LANES), F32)],
        compiler_params=_params(1),
        name="ssm_out",
    )(y2, unat, nat, nat, d_skip.astype(F32)[None, :], w_glu.astype(BF16), w_ssm_out.astype(BF16))


def _ssm_kernel(u_ref, st_ref, mt_ref, rtt_ref, coef_ref, y_ref, hloc_ref, hprev_ref):
    ng, nc, bsz = u_ref.shape[0], u_ref.shape[1], u_ref.shape[2]
    ns2 = 2 * SSM_STATE
    us, coefs = [], []
    for g in range(ng):
        u = u_ref[g].reshape(nc * bsz, CHUNK_W)
        us.append(u)
        half = nc * bsz // 2
        hloc_ref[g, :half, :] = jnp.dot(u[:half], st_ref[g], preferred_element_type=F32)
        hloc_ref[g, half:, :] = jnp.dot(u[half:], st_ref[g], preferred_element_type=F32)
        coefs.append((jnp.broadcast_to(coef_ref[g, 0:1, :], (bsz, ns2)),
                      jnp.broadcast_to(coef_ref[g, 1:2, :], (bsz, ns2))))

    def step(c, carry):
        r = pl.multiple_of(c * bsz, bsz)
        out = []
        for g in range(ng):
            hx, hy = carry[2 * g], carry[2 * g + 1]
            a, b = coefs[g]
            hprev_ref[g, pl.ds(r, bsz), :] = hx.astype(BF16)
            px = hloc_ref[g, pl.ds(r, bsz), 0:ns2]
            py = hloc_ref[g, pl.ds(r, bsz), ns2:2 * ns2]
            out += [a * hx + b * hy + px, a * hy - b * hx + py]
        return tuple(out)

    zero = jnp.zeros((bsz, ns2), F32)
    lax.fori_loop(0, nc, step, (zero,) * (2 * ng), unroll=8)
    for g in range(ng):
        y = jnp.dot(us[g], mt_ref[g], preferred_element_type=F32)
        y = y + lax.dot_general(hprev_ref[g], rtt_ref[g], (((1,), (1,)), ((), ())), preferred_element_type=F32)
        y_ref[g] = y.reshape(nc, bsz, CHUNK_W).astype(BF16)


def _ssm_call(u2, st, mt, rt, coef):
    g, nc, bsz, _ = u2.shape
    ng = SCAN_GROUPS
    gspec = lambda shape: pl.BlockSpec((ng,) + shape, lambda i: (i,) + (0,) * len(shape))
    return pl.pallas_call(
        _ssm_kernel,
        grid=(g // ng,),
        in_specs=[gspec((nc, bsz, CHUNK_W)), gspec((CHUNK_W, 4 * SSM_STATE)), gspec((CHUNK_W, CHUNK_W)),
                  gspec((CHUNK_W, 2 * SSM_STATE)), gspec((8, 2 * SSM_STATE))],
        out_specs=gspec((nc, bsz, CHUNK_W)),
        out_shape=jax.ShapeDtypeStruct(u2.shape, BF16),
        scratch_shapes=[pltpu.VMEM((ng, nc * bsz, 4 * SSM_STATE), F32),
                        pltpu.VMEM((ng, nc * bsz, 2 * SSM_STATE), BF16)],
        compiler_params=_params(1),
        name="ssm_scan",
    )(u2, st, mt, rt, coef)


def _attn_kernel(q_ref, kc_ref, kp_ref, vc_ref, vp_ref, o_ref, lse_ref):
    nres, tq = q_ref.shape[0], q_ref.shape[1]
    qb = 128
    slab = pl.program_id(2)
    qi = lax.broadcasted_iota(jnp.int32, (qb, 2 * qb), 0)
    kk = lax.broadcasted_iota(jnp.int32, (qb, 2 * qb), 1)
    band = jnp.where((kk >= qi) & (kk <= qi + WINDOW_KEYS), 0.0, NEG_INF).astype(F32)
    band0 = band + jnp.where(jnp.logical_and(slab == 0, kk < qb), NEG_INF, 0.0).astype(F32)
    band = jnp.concatenate([band, band], axis=0)
    band0 = jnp.concatenate([band0, band0], axis=0)
    lo_q = lax.broadcasted_iota(jnp.int32, (qb, LANES), 1) < HEAD_DIM
    ones_kv = jnp.ones((2 * qb, LANES), BF16)

    head_slot = lax.broadcasted_iota(jnp.int32, (qb, LANES), 1) // LSE_LANES

    for r, j in [(r, j) for r in range(nres) for j in range(tq // qb)]:
        madd = band0 if j == 0 else band
        rows = slice(j * qb, (j + 1) * qb)
        m_all = jnp.zeros((qb, LANES), F32)
        den_all = jnp.ones((qb, LANES), F32)
        for hp in range(ATTN_W // LANES):
            cs = slice(hp * LANES, (hp + 1) * LANES)
            qp = q_ref[r, rows, cs]
            if j == 0:
                kp = jnp.concatenate([kp_ref[r, :, cs], kc_ref[r, 0:qb, cs]], axis=0)
                vp = jnp.concatenate([vp_ref[r, :, cs], vc_ref[r, 0:qb, cs]], axis=0)
            else:
                kp = kc_ref[r, (j - 1) * qb:(j + 1) * qb, cs]
                vp = vc_ref[r, (j - 1) * qb:(j + 1) * qb, cs]
            zero = jnp.zeros_like(qp)
            q2 = jnp.concatenate([jnp.where(lo_q, qp, zero), jnp.where(lo_q, zero, qp)], axis=0)
            s = lax.dot_general(q2, kp, (((1,), (1,)), ((), ())), preferred_element_type=F32)
            s = s + madd
            m = jnp.max(s, axis=1, keepdims=True)
            p = jnp.exp2(s - m)
            pv = jnp.dot(p.astype(BF16), jnp.concatenate([vp, ones_kv], axis=1), preferred_element_type=F32)
            num = jnp.where(lo_q, pv[:qb, :LANES], pv[qb:, :LANES])
            den = jnp.where(lo_q, pv[:qb, LANES:], pv[qb:, LANES:])
            o_ref[r, rows, cs] = (num / den).astype(BF16)
            for side in range(2):
                slot = head_slot == 2 * hp + side
                half = slice(side * qb, (side + 1) * qb)
                m_all = jnp.where(slot, m[half], m_all)
                den_all = jnp.where(slot, pv[half, LANES:], den_all)
        lse_ref[r, rows, :] = m_all + jnp.log2(den_all)


def _attn_call(qkv, group, blocks, rows_per_step=2048):
    bsz, d, ld, _ = qkv.shape
    tq = min(rows_per_step, ld)
    nres = min(d, rows_per_step // tq)
    nslab = ld // tq
    per = tq // 128
    cur = lambda blk: pl.BlockSpec((None, nres, tq, COL), lambda b, r, i: (b, r, i, blk))
    prev = lambda blk: pl.BlockSpec(
        (None, nres, 128, COL), lambda b, r, i: (b, r, jnp.maximum(i * per - 1, 0), blk))
    ospec = lambda w: pl.BlockSpec((None, nres, tq, w), lambda b, r, i: (b, r, i, 0))
    qb, kb, vb = blocks
    return pl.pallas_call(
        _attn_kernel,
        grid=(bsz, d // nres, nslab),
        in_specs=[cur(qb), cur(kb), prev(kb), cur(vb), prev(vb)],
        out_specs=[ospec(ATTN_W), ospec(LANES)],
        out_shape=[jax.ShapeDtypeStruct((bsz, d, ld, ATTN_W), BF16),
                   jax.ShapeDtypeStruct((bsz, d, ld, LANES), F32)],
        compiler_params=_params(3),
        name=f"attn_d{DILATIONS[group]}",
    )(qkv, qkv, qkv, qkv, qkv)


def _merge_kernel(x_ref, ys_ref, za_ref, ra_ref,
                  a0_ref, a1_ref, a2_ref, l0_ref, l1_ref, l2_ref,
                  spread_ref, wao_ref, wo_ref, o_ref, il_ref, stage_ref):
    tm = x_ref.shape[0]
    d1 = DILATIONS[1]

    def to_token_order(ref, base, stage_base=None):
        d, nslab = ref.shape[0], ref.shape[2] // LANES
        for r in range(d):
            blk = ref[r].astype(F32)
            for c in range(nslab):
                piece = blk[:, c * LANES:(c + 1) * LANES]
                if d == d1:
                    il_ref[base + c, pl.ds(r, tm // d, stride=d), :] = piece
                else:
                    r1, r2 = r % d1, r // d1
                    stage_ref[stage_base + c, pl.ds(r1 * (tm // d1) + r2, tm // d, stride=d1), :] = piece
        if d != d1:
            for c in range(nslab):
                for r1 in range(d1):
                    il_ref[base + c, pl.ds(r1, tm // d1, stride=d1), :] = (
                        stage_ref[stage_base + c, r1 * (tm // d1):(r1 + 1) * (tm // d1), :])
        return lambda rows: jnp.concatenate([il_ref[base + c, rows, :] for c in range(nslab)], axis=1)

    def per_head_to_lanes(w):
        hi = w.astype(BF16)
        lo = (w - hi.astype(F32)).astype(BF16)
        return jnp.dot(jnp.concatenate([hi, lo], axis=1), spread_ref[...], preferred_element_type=F32)

    wide_slabs = ATTN_W // LANES
    l1_at, l2_at = to_token_order(l1_ref, 0), to_token_order(l2_ref, 1, 0)
    a1_at, a2_at = to_token_order(a1_ref, 2), to_token_order(a2_ref, 2 + wide_slabs, 1)
    rows = slice(None)
    l0, l1, l2 = l0_ref[...], l1_at(rows), l2_at(rows)
    lm = jnp.maximum(jnp.maximum(l0, l1), l2)
    e0, e1, e2 = jnp.exp2(l0 - lm), jnp.exp2(l1 - lm), jnp.exp2(l2 - lm)
    inv = 1.0 / (e0 + e1 + e2)
    att = (per_head_to_lanes(e0 * inv) * a0_ref[...].astype(F32)
           + per_head_to_lanes(e1 * inv) * a1_at(rows)
           + per_head_to_lanes(e2 * inv) * a2_at(rows))
    ya_in = (att * za_ref[...].astype(F32)).astype(BF16)
    ya = jnp.dot(ya_in, wao_ref[...], preferred_element_type=F32)
    m = ys_ref[...].astype(F32) + ra_ref[...].astype(F32) * ya
    o_ref[...] = x_ref[...] + jnp.dot(m.astype(BF16), wo_ref[...], preferred_element_type=F32)


def _merge_call(x, ys, nat, attn, w_attn_out, w_o, tm=1024):
    bsz, length, _ = x.shape
    wide = lambda blk: pl.BlockSpec((None, tm, D_MODEL), lambda b, i: (b, i, blk))
    half = lambda blk: pl.BlockSpec((None, tm, ATTN_W), lambda b, i: (b, i, blk))
    res = lambda d, w: pl.BlockSpec((None, d, tm // d, w), lambda b, i: (b, 0, i, 0))
    (a0, l0), (a1, l1), (a2, l2) = attn
    d4, d16 = DILATIONS[1], DILATIONS[2]
    a0, l0 = a0.reshape(bsz, length, ATTN_W), l0.reshape(bsz, length, LANES)
    spread = (jnp.arange(LANES)[:, None] == (jnp.arange(ATTN_W)[None, :] // HEAD_DIM) * LSE_LANES).astype(BF16)
    spread = jnp.concatenate([spread, spread], axis=0)
    return pl.pallas_call(
        _merge_kernel,
        grid=(bsz, length // tm),
        in_specs=[wide(0), wide(0), half(NAT_ZA), wide(NAT_RA // 2),
                  half(0), res(d4, ATTN_W), res(d16, ATTN_W),
                  pl.BlockSpec((None, tm, LANES), lambda b, i: (b, i, 0)), res(d4, LANES), res(d16, LANES),
                  _full_spec((2 * LANES, ATTN_W)), _full_spec((ATTN_W, D_MODEL)), _full_spec((D_MODEL, D_MODEL))],
        out_specs=wide(0),
        out_shape=jax.ShapeDtypeStruct((bsz, length, D_MODEL), F32),
        scratch_shapes=[pltpu.VMEM((2 + 2 * (ATTN_W // LANES), tm, LANES), F32),
                        pltpu.VMEM((1 + ATTN_W // LANES, tm, LANES), F32)],
        compiler_params=_params(2),
        name="merge",
    )(x, ys, nat, nat, a0, a1, a2, l0, l1, l2, spread, w_attn_out.astype(BF16), w_o.astype(BF16))


def kernel(x, positions, norm_w, w_in, lam_re, lam_im, log_dt, b_re, b_im, c_re, c_im, d_skip, w_glu,
           q_norm_w, k_norm_w, w_ssm_out, w_attn_out, w_o):
    bsz, length, width = x.shape
    assert width == D_MODEL and w_in.shape[1:] == (D_MODEL, IN_WIDTH)
    assert bsz % 16 == 0, "the scan layout packs the batch into 16-row bf16 tiles"
    assert length % (128 * DILATIONS[-1]) == 0, "every residue class must hold whole 128-query blocks"
    xf = x.astype(F32)
    for layer in range(norm_w.shape[0]):
        st, mt, rt, coef = _ssm_prep_call(lam_re[layer], lam_im[layer], log_dt[layer], b_re[layer],
                                          b_im[layer], c_re[layer], c_im[layer])
        nat, qkv4, qkv16 = _proj_call(xf, positions, norm_w[layer], w_in[layer], q_norm_w[layer],
                                      k_norm_w[layer])
        unat, u2 = _ssm_in_call(xf, norm_w[layer], w_in[layer][:, :D_MODEL])
        y2 = _ssm_call(u2, st, mt, rt, coef)
        ys = _ssm_out_call(y2, unat, nat, d_skip[layer], w_glu[layer], w_ssm_out[layer])
        attn = [_attn_call(nat[:, None], 0, (NAT_Q, NAT_K, NAT_V)),
                _attn_call(qkv4, 1, (0, 1, 2)),
                _attn_call(qkv16, 2, (0, 1, 2))]
        xf = _merge_call(xf, ys, nat, attn, w_attn_out[layer], w_o[layer])
    return xf.astype(x.dtype)
```

```python
import math

import jax
import jax.numpy as jnp
from jax import lax
from jax.experimental import pallas as pl
from jax.experimental.pallas import tpu as pltpu

F32 = jnp.float32
BF16 = jnp.bfloat16

D_MODEL = 1024
SSM_GROUP = 16
SSM_GROUPS = D_MODEL // SSM_GROUP
SSM_STATE = 64
CHUNK = 16
CHUNK_W = CHUNK * SSM_GROUP
HEAD_DIM = 64
HEADS = 8
ATTN_W = HEADS * HEAD_DIM
DILATIONS = (1, 4, 16)
WINDOW_KEYS = 128
ROPE_DIM = HEAD_DIM // 4
ROPE_THETA = 500000.0
EPS = 1e-6
NEG_INF = -1e30
LOG2E = math.log2(math.e)
GELU_K1 = -2.0 * math.sqrt(2.0 / math.pi) * LOG2E
GELU_K3 = GELU_K1 * 0.044715
COL = 512
ZS_BLK, Q_BLK, K_BLK, V_BLK, ZA_BLK, RS_BLK, RA_BLK = 2, 4, 7, 10, 13, 14, 16
N_BLK = 18
IN_WIDTH = N_BLK * COL
NAT_ZS, NAT_Q, NAT_K, NAT_V, NAT_ZA, NAT_RS, NAT_RA = 0, 2, 3, 4, 5, 6, 8
NAT_W = 10 * COL
QKV_W = 3 * COL
LANES = 128
MXU_DIM = 256
LSE_LANES = LANES // HEADS
LANE_BLKS = LANES // SSM_GROUP
SCAN_GROUPS = 2
SSM_CB = 8
SSM_SLOTS = 2
SSM_PITCH = 8 * 3
VMEM_LIMIT = 56 * 1024 * 1024


def _params(grid_rank):
    return pltpu.CompilerParams(dimension_semantics=("arbitrary",) * grid_rank, vmem_limit_bytes=VMEM_LIMIT)


def _full_spec(shape):
    nd = len(shape)
    return pl.BlockSpec(shape, lambda *_: (0,) * nd, pipeline_mode=pl.Buffered(1))


def _ssm_prep_kernel(lam_ref, btr_ref, bti_ref, cr_ref, ci_ref, st_ref, mt_ref, rtt_ref, coef_ref):
    ns2 = 2 * SSM_STATE
    lam = lam_ref[0]
    lr, li, dt = lam[0:1, :], lam[1:2, :], jnp.exp(lam[2:3, :])
    mag = jnp.exp(lr * dt)
    ar = mag * jnp.cos(li * dt)
    ai = mag * jnp.sin(li * dt)
    den = lr * lr + li * li
    nr = ar - 1.0
    fr = (nr * lr + ai * li) / den
    fi = (ai * lr - nr * li) / den
    pr, pi = [jnp.ones_like(ar)], [jnp.zeros_like(ar)]
    for _ in range(CHUNK):
        pr, pi = pr + [pr[-1] * ar - pi[-1] * ai], pi + [pr[-1] * ai + pi[-1] * ar]
    by_step = lambda vals: jnp.concatenate([jnp.broadcast_to(v, (SSM_GROUP, ns2)) for v in vals], axis=0)
    per_step = lambda a: jnp.concatenate([a] * CHUNK, axis=0)
    im_part = lax.broadcasted_iota(jnp.int32, (CHUNK_W, ns2), 1) >= SSM_STATE

    btr, bti = per_step(btr_ref[0]), per_step(bti_ref[0])
    bbr = fr * btr - fi * bti
    bbi = fr * bti + fi * btr
    apr = by_step([pr[CHUNK - 1 - s] for s in range(CHUNK)])
    api = by_step([pi[CHUNK - 1 - s] for s in range(CHUNK)])
    wre = apr * bbr - api * bbi
    wim = apr * bbi + api * bbr
    st = jnp.where(im_part, wim, wre)
    st_sw = jnp.where(im_part, wre, wim)
    st_ref[0] = jnp.concatenate([st, st_sw], axis=1).astype(BF16)

    cr, ci = per_step(cr_ref[0]), per_step(ci_ref[0])
    qr = by_step([pr[t + 1] for t in range(CHUNK)])
    qi = by_step([pi[t + 1] for t in range(CHUNK)])
    rtt_ref[0] = jnp.where(im_part, -(cr * qi + ci * qr), cr * qr - ci * qi).astype(BF16)

    cct = jnp.where(im_part, -ci, cr)
    split = lambda v: (v.astype(BF16), (v - v.astype(BF16).astype(F32)).astype(BF16))
    (st_hi, st_lo), (cc_hi, cc_lo) = split(st), split(cct)
    nt = (((1,), (1,)), ((), ()))
    krw = (lax.dot_general(jnp.concatenate([st_hi, st_lo], axis=1), jnp.concatenate([cc_hi, cc_hi], axis=1), nt,
                           preferred_element_type=F32)
           + lax.dot_general(st_hi, cc_lo, nt, preferred_element_type=F32))
    lane_blk = lax.broadcasted_iota(jnp.int32, (CHUNK_W, CHUNK_W), 1) // SSM_GROUP
    mt = jnp.zeros((CHUNK_W, CHUNK_W), F32)
    for t in range(CHUNK):
        sh = SSM_GROUP * (CHUNK - 1 - t)
        if sh == 0:
            shifted = krw
        else:
            shifted = jnp.concatenate([krw[sh:, :], jnp.zeros((sh, CHUNK_W), F32)], axis=0)
        mt = jnp.where(lane_blk == t, shifted, mt)
    mt_ref[0] = mt.astype(BF16)

    im_row = lax.broadcasted_iota(jnp.int32, (1, ns2), 1) >= SSM_STATE
    bco = jnp.where(im_row, pi[CHUNK], -pi[CHUNK])
    coef_ref[0] = jnp.concatenate([pr[CHUNK], bco, jnp.zeros((6, ns2), F32)], axis=0)


def _ssm_prep_call(lam_re, lam_im, log_dt, b_re, b_im, c_re, c_im):
    g, n, p = SSM_GROUPS, SSM_STATE, SSM_GROUP
    f = lambda a: a.astype(F32)
    twice = lambda a: jnp.tile(f(a), (1,) * (a.ndim - 1) + (2,))
    ldt_b = jnp.broadcast_to(f(log_dt)[:, None], (g, n))
    lam = jnp.concatenate([twice(lam_re)[:, None], twice(lam_im)[:, None], twice(ldt_b)[:, None],
                           jnp.zeros((g, 5, 2 * n), F32)], axis=1)
    bt = lambda b: twice(b.transpose(0, 2, 1))
    gspec = lambda shape: pl.BlockSpec((1,) + shape, lambda i: (i, 0, 0))
    return pl.pallas_call(
        _ssm_prep_kernel,
        grid=(g,),
        in_specs=[gspec((8, 2 * n))] + [gspec((p, 2 * n))] * 4,
        out_specs=[gspec((CHUNK_W, 4 * n)), gspec((CHUNK_W, CHUNK_W)), gspec((CHUNK_W, 2 * n)),
                   gspec((8, 2 * n))],
        out_shape=[jax.ShapeDtypeStruct((g, CHUNK_W, 4 * n), BF16),
                   jax.ShapeDtypeStruct((g, CHUNK_W, CHUNK_W), BF16),
                   jax.ShapeDtypeStruct((g, CHUNK_W, 2 * n), BF16),
                   jax.ShapeDtypeStruct((g, 8, 2 * n), F32)],
        compiler_params=_params(1),
        name="ssm_prep",
    )(lam, bt(b_re), bt(b_im), twice(c_re), twice(c_im))


def _rope_tables():
    lane = jnp.arange(LANES) % HEAD_DIM
    half = ROPE_DIM // 2
    inv = ROPE_THETA ** (-jnp.arange(0, ROPE_DIM, 2, dtype=F32) / ROPE_DIM)
    expand = ((lane[None, :] < ROPE_DIM) & (lane[None, :] % half == jnp.arange(half)[:, None])).astype(F32)
    s_lo = jnp.where(lane < half, -1.0, 0.0)
    s_hi = jnp.where((lane >= half) & (lane < ROPE_DIM), 1.0, 0.0)
    unrot = jnp.where(lane >= ROPE_DIM, 1.0, 0.0)
    rows = jnp.concatenate([s_lo[None], s_hi[None], unrot[None], jnp.zeros((5, LANES), F32)], axis=0)
    expand = jnp.concatenate([expand, expand], axis=0).astype(BF16)
    return inv[:, None].astype(F32), expand, rows.astype(F32)


def _sigmoid(v):
    return 1.0 / (1.0 + jnp.exp(-v))


_NAT_DST = {2: NAT_ZS, 3: NAT_ZS + 1, Q_BLK: NAT_Q, K_BLK: NAT_K, V_BLK: NAT_V,
            ZA_BLK: NAT_ZA, 14: NAT_RS, 15: NAT_RS + 1, 16: NAT_RA, 17: NAT_RA + 1}


_PROJ_ORDER = (2, 4, 3, 7, 13, 5, 14, 8, 15, 6, 16, 9, 17, 11, 12, 10)
_N_PERM = 6


def _proj_kernel(x_ref, pos_ref, nw_ref, w_ref, qkw_ref, ones_ref, freq_ref, expand_ref, rope_ref,
                 nat_ref, d4_ref, d16_ref, perm_ref, stage_ref, h_ref):
    tm = x_ref.shape[0]
    x = x_ref[...]
    ms = jnp.mean(x * x, axis=-1, keepdims=True)
    h_ref[...] = (x * lax.rsqrt(ms + EPS) * nw_ref[...]).astype(BF16)
    ang = freq_ref[...] * pos_ref[...].astype(F32)

    def spread(v):
        hi = v.astype(BF16)
        lo = (v - hi.astype(F32)).astype(BF16)
        return lax.dot_general(jnp.concatenate([hi, lo], axis=0), expand_ref[...], (((0,), (0,)), ((), ())),
                               preferred_element_type=F32)

    cosv = spread(jnp.cos(ang)) + rope_ref[2:3, :]
    sinv = spread(jnp.sin(ang))
    s_lo = sinv * rope_ref[0:1, :]
    s_hi = sinv * rope_ref[1:2, :]
    slot = 0
    for j in _PROJ_ORDER:
        acc = jnp.dot(h_ref[...], w_ref[:, j * COL:(j + 1) * COL], preferred_element_type=F32)
        if ZS_BLK <= j < Q_BLK or j == ZA_BLK:
            res = acc * _sigmoid(acc)
        elif j >= RS_BLK:
            res = _sigmoid(acc)
        elif Q_BLK <= j < V_BLK:
            sq = (acc * acc).astype(BF16)
            hw = ones_ref.shape[0]
            ss = jnp.concatenate([jnp.dot(sq[:, c * hw:(c + 1) * hw], ones_ref[...], preferred_element_type=F32)
                                  for c in range(COL // hw)], axis=1)
            y = acc * lax.rsqrt(ss * (1.0 / HEAD_DIM) + EPS) * qkw_ref[j - Q_BLK:j - Q_BLK + 1, :]
            parts = []
            for c in range(COL // LANES):
                yc = y[:, c * LANES:(c + 1) * LANES]
                parts.append(yc * cosv + pltpu.roll(yc, LANES - ROPE_DIM // 2, 1) * s_lo
                             + pltpu.roll(yc, ROPE_DIM // 2, 1) * s_hi)
            res = jnp.concatenate(parts, axis=1)
        else:
            res = acc
        if j in _NAT_DST:
            dst = _NAT_DST[j]
            nat_ref[:, dst * COL:(dst + 1) * COL] = res.astype(BF16)
        else:
            kind, group = divmod(j - Q_BLK, len(DILATIONS))
            d = DILATIONS[group]
            out_ref = d4_ref if group == 1 else d16_ref
            nslab = COL // LANES
            for c in range(nslab):
                perm_ref[slot, c] = res[:, c * LANES:(c + 1) * LANES]
            if d == DILATIONS[2]:
                d1 = DILATIONS[1]
                for c in range(nslab):
                    for r1 in range(d1):
                        stage_ref[kind, c, r1 * (tm // d1):(r1 + 1) * (tm // d1), :] = (
                            perm_ref[slot, c, pl.ds(r1, tm // d1, stride=d1), :])
                for r1, r2 in [(r1, r2) for r1 in range(d1) for r2 in range(d // d1)]:
                    rows = [stage_ref[kind, c, pl.ds(r1 * (tm // d1) + r2, tm // d, stride=d1), :]
                            for c in range(nslab)]
                    out_ref[r1 + d1 * r2, :, kind * COL:(kind + 1) * COL] = (
                        jnp.concatenate(rows, axis=1).astype(BF16))
            else:
                for r in range(d):
                    rows = [perm_ref[slot, c, pl.ds(r, tm // d, stride=d), :] for c in range(nslab)]
                    out_ref[r, :, kind * COL:(kind + 1) * COL] = jnp.concatenate(rows, axis=1).astype(BF16)
            slot += 1


def _proj_call(x, positions, norm_w, w_in, q_norm_w, k_norm_w, tm=512):
    bsz, length, _ = x.shape
    scale = LOG2E / math.sqrt(HEAD_DIM)
    qkw = jnp.concatenate([jnp.tile(q_norm_w.astype(F32) * scale, (1, HEADS)),
                           jnp.tile(k_norm_w.astype(F32), (1, HEADS)),
                           jnp.zeros((2, ATTN_W), F32)], axis=0)
    hid = jnp.arange(MXU_DIM) // HEAD_DIM
    ones = (hid[:, None] == hid[None, :]).astype(BF16)
    d4, d16 = DILATIONS[1], DILATIONS[2]
    freq, expand, rope_rows = _rope_tables()
    return pl.pallas_call(
        _proj_kernel,
        grid=(bsz, length // tm),
        in_specs=[pl.BlockSpec((None, tm, D_MODEL), lambda b, i: (b, i, 0)),
                  pl.BlockSpec((None, 1, tm), lambda b, i: (b, 0, i)),
                  _full_spec((1, D_MODEL)),
                  _full_spec((D_MODEL, IN_WIDTH)),
                  _full_spec((8, ATTN_W)),
                  _full_spec((MXU_DIM, MXU_DIM)),
                  _full_spec((ROPE_DIM // 2, 1)),
                  _full_spec((ROPE_DIM, LANES)),
                  _full_spec((8, LANES))],
        out_specs=[pl.BlockSpec((None, tm, NAT_W), lambda b, i: (b, i, 0)),
                   pl.BlockSpec((None, d4, tm // d4, QKV_W), lambda b, i: (b, 0, i, 0)),
                   pl.BlockSpec((None, d16, tm // d16, QKV_W), lambda b, i: (b, 0, i, 0))],
        out_shape=[jax.ShapeDtypeStruct((bsz, length, NAT_W), BF16),
                   jax.ShapeDtypeStruct((bsz, d4, length // d4, QKV_W), BF16),
                   jax.ShapeDtypeStruct((bsz, d16, length // d16, QKV_W), BF16)],
        scratch_shapes=[pltpu.VMEM((_N_PERM, COL // LANES, tm, LANES), F32),
                        pltpu.VMEM((3, COL // LANES, tm, LANES), F32),
                        pltpu.VMEM((tm, D_MODEL), BF16)],
        compiler_params=_params(2),
        name="proj",
    )(x.astype(F32), positions.astype(jnp.int32)[:, None, :], norm_w.astype(F32)[None, :],
      w_in.astype(BF16), qkw, ones, freq, expand, rope_rows)


def _block_transpose(xs):
    xs = list(xs)
    blk = lax.broadcasted_iota(jnp.int32, xs[0].shape, 1) // SSM_GROUP
    dist = 1
    while dist < LANE_BLKS:
        upper = (blk & dist) != 0
        shift = dist * SSM_GROUP
        for i in range(LANE_BLKS):
            if i & dist:
                continue
            a, b = xs[i], xs[i + dist]
            xs[i] = jnp.where(upper, pltpu.roll(b, shift, 1), a)
            xs[i + dist] = jnp.where(upper, b, pltpu.roll(a, LANES - shift, 1))
        dist *= 2
    return xs


def _ssm_in_kernel(x_ref, nw_ref, w_ref, unat_ref, u2_ref, rows_ref):
    bsz, tl = x_ref.shape[0], x_ref.shape[1]
    for c in range(tl // CHUNK):
        x = x_ref[:, c * CHUNK:(c + 1) * CHUNK, :].reshape(bsz * CHUNK, D_MODEL)
        ms = jnp.mean(x * x, axis=-1, keepdims=True)
        h = (x * lax.rsqrt(ms + EPS) * nw_ref[...]).astype(BF16)
        u = jnp.dot(h, w_ref[...], preferred_element_type=F32)
        unat_ref[:, c * CHUNK:(c + 1) * CHUNK, :] = u.reshape(bsz, CHUNK, D_MODEL).astype(BF16)
        for b in range(bsz):
            for s8 in range(D_MODEL // LANES):
                rows_ref[c % SSM_SLOTS, s8, b * SSM_PITCH:b * SSM_PITCH + CHUNK, :] = (
                    u[b * CHUNK:(b + 1) * CHUNK, s8 * LANES:(s8 + 1) * LANES])
        for s8 in range(D_MODEL // LANES):
            for half in range(CHUNK // LANE_BLKS):
                xs = []
                for k in range(LANE_BLKS):
                    v = rows_ref[c % SSM_SLOTS, s8, pl.ds(half * LANE_BLKS + k, bsz, stride=SSM_PITCH), :].astype(BF16)
                    xs.append(pltpu.bitcast(v, jnp.uint32))
                ys = _block_transpose(xs)
                for g in range(LANE_BLKS):
                    u2_ref[s8 * LANE_BLKS + g, c, :, half * LANES:(half + 1) * LANES] = pltpu.bitcast(ys[g], BF16)


def _ssm_in_call(x, norm_w, w_u):
    bsz, length, _ = x.shape
    tl = SSM_CB * CHUNK
    nc = length // CHUNK
    return pl.pallas_call(
        _ssm_in_kernel,
        grid=(length // tl,),
        in_specs=[pl.BlockSpec((bsz, tl, D_MODEL), lambda i: (0, i, 0)),
                  _full_spec((1, D_MODEL)), _full_spec((D_MODEL, D_MODEL))],
        out_specs=[pl.BlockSpec((bsz, tl, D_MODEL), lambda i: (0, i, 0)),
                   pl.BlockSpec((SSM_GROUPS, SSM_CB, bsz, CHUNK_W), lambda i: (0, i, 0, 0))],
        out_shape=[jax.ShapeDtypeStruct((bsz, length, D_MODEL), BF16),
                   jax.ShapeDtypeStruct((SSM_GROUPS, nc, bsz, CHUNK_W), BF16)],
        scratch_shapes=[pltpu.VMEM((SSM_SLOTS, D_MODEL // LANES, bsz * SSM_PITCH, LANES), F32)],
        compiler_params=_params(1),
        name="ssm_in",
    )(x, norm_w.astype(F32)[None, :], w_u.astype(BF16))


def _ssm_out_kernel(y2_ref, u_ref, zs_ref, rs_ref, dskip_ref, wglu_ref, wso_ref, o_ref, rows_ref):
    bsz, tl = u_ref.shape[0], u_ref.shape[1]
    for c in range(tl // CHUNK):
        for s8 in range(D_MODEL // LANES):
            for half in range(CHUNK // LANE_BLKS):
                xs = [pltpu.bitcast(y2_ref[s8 * LANE_BLKS + g, c, :, half * LANES:(half + 1) * LANES], jnp.uint32)
                      for g in range(LANE_BLKS)]
                ys = _block_transpose(xs)
                for k in range(LANE_BLKS):
                    rows_ref[c % SSM_SLOTS, s8, pl.ds(half * LANE_BLKS + k, bsz, stride=SSM_PITCH), :] = (
                        pltpu.bitcast(ys[k], BF16).astype(F32))
        y = jnp.concatenate(
            [jnp.concatenate([rows_ref[c % SSM_SLOTS, s8, b * SSM_PITCH:b * SSM_PITCH + CHUNK, :]
                              for s8 in range(D_MODEL // LANES)], axis=1)
             for b in range(bsz)], axis=0)
        tok = slice(c * CHUNK, (c + 1) * CHUNK)
        u = u_ref[:, tok, :].reshape(bsz * CHUNK, D_MODEL).astype(F32)
        zs = zs_ref[:, tok, :].reshape(bsz * CHUNK, D_MODEL).astype(F32)
        y = y + dskip_ref[...] * u
        g = y / (1.0 + jnp.exp2(y * (GELU_K1 + GELU_K3 * (y * y))))
        gate = _sigmoid(jnp.dot(g.astype(BF16), wglu_ref[...], preferred_element_type=F32))
        ys_in = (g * gate * zs).astype(BF16)
        ys = jnp.dot(ys_in, wso_ref[...], preferred_element_type=F32)
        gated = ys * rs_ref[:, tok, :].reshape(bsz * CHUNK, D_MODEL).astype(F32)
        o_ref[:, tok, :] = gated.reshape(bsz, CHUNK, D_MODEL).astype(BF16)


def _ssm_out_call(y2, unat, nat, d_skip, w_glu, w_ssm_out):
    bsz, length, _ = unat.shape
    tl = SSM_CB * CHUNK
    tok = lambda blk: pl.BlockSpec((bsz, tl, D_MODEL), lambda i: (0, i, blk))
    return pl.pallas_call(
        _ssm_out_kernel,
        grid=(length // tl,),
        in_specs=[pl.BlockSpec((SSM_GROUPS, SSM_CB, bsz, CHUNK_W), lambda i: (0, i, 0, 0)),
                  tok(0), tok(NAT_ZS // 2), tok(NAT_RS // 2),
                  _full_spec((1, D_MODEL)), _full_spec((D_MODEL, D_MODEL)), _full_spec((D_MODEL, D_MODEL))],
        out_specs=tok(0),
        out_shape=jax.ShapeDtypeStruct((bsz, length, D_MODEL), BF16),
        scratch_shapes=[pltpu.VMEM((SSM_SLOTS, D_MODEL // LANES, bsz * SSM_PITCH, LANES), F32)],
        compiler_params=_params(1),
        name="ssm_out",
    )(y2, unat, nat, nat, d_skip.astype(F32)[None, :], w_glu.astype(BF16), w_ssm_out.astype(BF16))


def _ssm_kernel(u_ref, st_ref, mt_ref, rtt_ref, coef_ref, y_ref, hloc_ref, hprev_ref):
    ng, nc, bsz = u_ref.shape[0], u_ref.shape[1], u_ref.shape[2]
    ns2 = 2 * SSM_STATE
    us, coefs = [], []
    for g in range(ng):
        u = u_ref[g].reshape(nc * bsz, CHUNK_W)
        us.append(u)
        half = nc * bsz // 2
        hloc_ref[g, :half, :] = jnp.dot(u[:half], st_ref[g], preferred_element_type=F32)
        hloc_ref[g, half:, :] = jnp.dot(u[half:], st_ref[g], preferred_element_type=F32)
        coefs.append((jnp.broadcast_to(coef_ref[g, 0:1, :], (bsz, ns2)),
                      jnp.broadcast_to(coef_ref[g, 1:2, :], (bsz, ns2))))

    def step(c, carry):
        r = pl.multiple_of(c * bsz, bsz)
        out = []
        for g in range(ng):
            hx, hy = carry[2 * g], carry[2 * g + 1]
            a, b = coefs[g]
            hprev_ref[g, pl.ds(r, bsz), :] = hx.astype(BF16)
            px = hloc_ref[g, pl.ds(r, bsz), 0:ns2]
            py = hloc_ref[g, pl.ds(r, bsz), ns2:2 * ns2]
            out += [a * hx + b * hy + px, a * hy - b * hx + py]
        return tuple(out)

    zero = jnp.zeros((bsz, ns2), F32)
    lax.fori_loop(0, nc, step, (zero,) * (2 * ng), unroll=8)
    for g in range(ng):
        y = jnp.dot(us[g], mt_ref[g], preferred_element_type=F32)
        y = y + lax.dot_general(hprev_ref[g], rtt_ref[g], (((1,), (1,)), ((), ())), preferred_element_type=F32)
        y_ref[g] = y.reshape(nc, bsz, CHUNK_W).astype(BF16)


def _ssm_call(u2, st, mt, rt, coef):
    g, nc, bsz, _ = u2.shape
    ng = SCAN_GROUPS
    gspec = lambda shape: pl.BlockSpec((ng,) + shape, lambda i: (i,) + (0,) * len(shape))
    return pl.pallas_call(
        _ssm_kernel,
        grid=(g // ng,),
        in_specs=[gspec((nc, bsz, CHUNK_W)), gspec((CHUNK_W, 4 * SSM_STATE)), gspec((CHUNK_W, CHUNK_W)),
                  gspec((CHUNK_W, 2 * SSM_STATE)), gspec((8, 2 * SSM_STATE))],
        out_specs=gspec((nc, bsz, CHUNK_W)),
        out_shape=jax.ShapeDtypeStruct(u2.shape, BF16),
        scratch_shapes=[pltpu.VMEM((ng, nc * bsz, 4 * SSM_STATE), F32),
                        pltpu.VMEM((ng, nc * bsz, 2 * SSM_STATE), BF16)],
        compiler_params=_params(1),
        name="ssm_scan",
    )(u2, st, mt, rt, coef)


def _attn_kernel(q_ref, kc_ref, kp_ref, vc_ref, vp_ref, o_ref, lse_ref):
    nres, tq = q_ref.shape[0], q_ref.shape[1]
    qb = 128
    slab = pl.program_id(2)
    qi = lax.broadcasted_iota(jnp.int32, (qb, 2 * qb), 0)
    kk = lax.broadcasted_iota(jnp.int32, (qb, 2 * qb), 1)
    band = jnp.where((kk >= qi) & (kk <= qi + WINDOW_KEYS), 0.0, NEG_INF).astype(F32)
    band0 = band + jnp.where(jnp.logical_and(slab == 0, kk < qb), NEG_INF, 0.0).astype(F32)
    band = jnp.concatenate([band, band], axis=0)
    band0 = jnp.concatenate([band0, band0], axis=0)
    lo_q = lax.broadcasted_iota(jnp.int32, (qb, LANES), 1) < HEAD_DIM
    ones_kv = jnp.ones((2 * qb, LANES), BF16)

    head_slot = lax.broadcasted_iota(jnp.int32, (qb, LANES), 1) // LSE_LANES

    for r, j in [(r, j) for r in range(nres) for j in range(tq // qb)]:
        madd = band0 if j == 0 else band
        rows = slice(j * qb, (j + 1) * qb)
        m_all = jnp.zeros((qb, LANES), F32)
        den_all = jnp.ones((qb, LANES), F32)
        for hp in range(ATTN_W // LANES):
            cs = slice(hp * LANES, (hp + 1) * LANES)
            qp = q_ref[r, rows, cs]
            if j == 0:
                kp = jnp.concatenate([kp_ref[r, :, cs], kc_ref[r, 0:qb, cs]], axis=0)
                vp = jnp.concatenate([vp_ref[r, :, cs], vc_ref[r, 0:qb, cs]], axis=0)
            else:
                kp = kc_ref[r, (j - 1) * qb:(j + 1) * qb, cs]
                vp = vc_ref[r, (j - 1) * qb:(j + 1) * qb, cs]
            zero = jnp.zeros_like(qp)
            q2 = jnp.concatenate([jnp.where(lo_q, qp, zero), jnp.where(lo_q, zero, qp)], axis=0)
            s = lax.dot_general(q2, kp, (((1,), (1,)), ((), ())), preferred_element_type=F32)
            s = s + madd
            m = jnp.max(s, axis=1, keepdims=True)
            p = jnp.exp2(s - m)
            pv = jnp.dot(p.astype(BF16), jnp.concatenate([vp, ones_kv], axis=1), preferred_element_type=F32)
            num = jnp.where(lo_q, pv[:qb, :LANES], pv[qb:, :LANES])
            den = jnp.where(lo_q, pv[:qb, LANES:], pv[qb:, LANES:])
            o_ref[r, rows, cs] = (num / den).astype(BF16)
            for side in range(2):
                slot = head_slot == 2 * hp + side
                half = slice(side * qb, (side + 1) * qb)
                m_all = jnp.where(slot, m[half], m_all)
                den_all = jnp.where(slot, pv[half, LANES:], den_all)
        lse_ref[r, rows, :] = m_all + jnp.log2(den_all)


def _attn_call(qkv, group, blocks, rows_per_step=2048):
    bsz, d, ld, _ = qkv.shape
    tq = min(rows_per_step, ld)
    nres = min(d, rows_per_step // tq)
    nslab = ld // tq
    per = tq // 128
    cur = lambda blk: pl.BlockSpec((None, nres, tq, COL), lambda b, r, i: (b, r, i, blk))
    prev = lambda blk: pl.BlockSpec(
        (None, nres, 128, COL), lambda b, r, i: (b, r, jnp.maximum(i * per - 1, 0), blk))
    ospec = lambda w: pl.BlockSpec((None, nres, tq, w), lambda b, r, i: (b, r, i, 0))
    qb, kb, vb = blocks
    return pl.pallas_call(
        _attn_kernel,
        grid=(bsz, d // nres, nslab),
        in_specs=[cur(qb), cur(kb), prev(kb), cur(vb), prev(vb)],
        out_specs=[ospec(ATTN_W), ospec(LANES)],
        out_shape=[jax.ShapeDtypeStruct((bsz, d, ld, ATTN_W), BF16),
                   jax.ShapeDtypeStruct((bsz, d, ld, LANES), F32)],
        compiler_params=_params(3),
        name=f"attn_d{DILATIONS[group]}",
    )(qkv, qkv, qkv, qkv, qkv)


def _merge_kernel(x_ref, ys_ref, za_ref, ra_ref,
                  a0_ref, a1_ref, a2_ref, l0_ref, l1_ref, l2_ref,
                  spread_ref, wao_ref, wo_ref, o_ref, il_ref, stage_ref):
    tm = x_ref.shape[0]
    d1 = DILATIONS[1]

    def to_token_order(ref, base, stage_base=None):
        d, nslab = ref.shape[0], ref.shape[2] // LANES
        for r in range(d):
            blk = ref[r].astype(F32)
            for c in range(nslab):
                piece = blk[:, c * LANES:(c + 1) * LANES]
                if d == d1:
                    il_ref[base + c, pl.ds(r, tm // d, stride=d), :] = piece
                else:
                    r1, r2 = r % d1, r // d1
                    stage_ref[stage_base + c, pl.ds(r1 * (tm // d1) + r2, tm // d, stride=d1), :] = piece
        if d != d1:
            for c in range(nslab):
                for r1 in range(d1):
                    il_ref[base + c, pl.ds(r1, tm // d1, stride=d1), :] = (
                        stage_ref[stage_base + c, r1 * (tm // d1):(r1 + 1) * (tm // d1), :])
        return lambda rows: jnp.concatenate([il_ref[base + c, rows, :] for c in range(nslab)], axis=1)

    def per_head_to_lanes(w):
        hi = w.astype(BF16)
        lo = (w - hi.astype(F32)).astype(BF16)
        return jnp.dot(jnp.concatenate([hi, lo], axis=1), spread_ref[...], preferred_element_type=F32)

    wide_slabs = ATTN_W // LANES
    l1_at, l2_at = to_token_order(l1_ref, 0), to_token_order(l2_ref, 1, 0)
    a1_at, a2_at = to_token_order(a1_ref, 2), to_token_order(a2_ref, 2 + wide_slabs, 1)
    rows = slice(None)
    l0, l1, l2 = l0_ref[...], l1_at(rows), l2_at(rows)
    lm = jnp.maximum(jnp.maximum(l0, l1), l2)
    e0, e1, e2 = jnp.exp2(l0 - lm), jnp.exp2(l1 - lm), jnp.exp2(l2 - lm)
    inv = 1.0 / (e0 + e1 + e2)
    att = (per_head_to_lanes(e0 * inv) * a0_ref[...].astype(F32)
           + per_head_to_lanes(e1 * inv) * a1_at(rows)
           + per_head_to_lanes(e2 * inv) * a2_at(rows))
    ya_in = (att * za_ref[...].astype(F32)).astype(BF16)
    ya = jnp.dot(ya_in, wao_ref[...], preferred_element_type=F32)
    m = ys_ref[...].astype(F32) + ra_ref[...].astype(F32) * ya
    o_ref[...] = x_ref[...] + jnp.dot(m.astype(BF16), wo_ref[...], preferred_element_type=F32)


def _merge_call(x, ys, nat, attn, w_attn_out, w_o, tm=1024):
    bsz, length, _ = x.shape
    wide = lambda blk: pl.BlockSpec((None, tm, D_MODEL), lambda b, i: (b, i, blk))
    half = lambda blk: pl.BlockSpec((None, tm, ATTN_W), lambda b, i: (b, i, blk))
    res = lambda d, w: pl.BlockSpec((None, d, tm // d, w), lambda b, i: (b, 0, i, 0))
    (a0, l0), (a1, l1), (a2, l2) = attn
    d4, d16 = DILATIONS[1], DILATIONS[2]
    a0, l0 = a0.reshape(bsz, length, ATTN_W), l0.reshape(bsz, length, LANES)
    spread = (jnp.arange(LANES)[:, None] == (jnp.arange(ATTN_W)[None, :] // HEAD_DIM) * LSE_LANES).astype(BF16)
    spread = jnp.concatenate([spread, spread], axis=0)
    return pl.pallas_call(
        _merge_kernel,
        grid=(bsz, length // tm),
        in_specs=[wide(0), wide(0), half(NAT_ZA), wide(NAT_RA // 2),
                  half(0), res(d4, ATTN_W), res(d16, ATTN_W),
                  pl.BlockSpec((None, tm, LANES), lambda b, i: (b, i, 0)), res(d4, LANES), res(d16, LANES),
                  _full_spec((2 * LANES, ATTN_W)), _full_spec((ATTN_W, D_MODEL)), _full_spec((D_MODEL, D_MODEL))],
        out_specs=wide(0),
        out_shape=jax.ShapeDtypeStruct((bsz, length, D_MODEL), F32),
        scratch_shapes=[pltpu.VMEM((2 + 2 * (ATTN_W // LANES), tm, LANES), F32),
                        pltpu.VMEM((1 + ATTN_W // LANES, tm, LANES), F32)],
        compiler_params=_params(2),
        name="merge",
    )(x, ys, nat, nat, a0, a1, a2, l0, l1, l2, spread, w_attn_out.astype(BF16), w_o.astype(BF16))


def kernel(x, positions, norm_w, w_in, lam_re, lam_im, log_dt, b_re, b_im, c_re, c_im, d_skip, w_glu,
           q_norm_w, k_norm_w, w_ssm_out, w_attn_out, w_o):
    bsz, length, width = x.shape
    assert width == D_MODEL and w_in.shape[1:] == (D_MODEL, IN_WIDTH)
    assert bsz % 16 == 0, "the scan layout packs the batch into 16-row bf16 tiles"
    assert length % (128 * DILATIONS[-1]) == 0, "every residue class must hold whole 128-query blocks"
    xf = x.astype(F32)
    for layer in range(norm_w.shape[0]):
        st, mt, rt, coef = _ssm_prep_call(lam_re[layer], lam_im[layer], log_dt[layer], b_re[layer],
                                          b_im[layer], c_re[layer], c_im[layer])
        nat, qkv4, qkv16 = _proj_call(xf, positions, norm_w[layer], w_in[layer], q_norm_w[layer],
                                      k_norm_w[layer])
        unat, u2 = _ssm_in_call(xf, norm_w[layer], w_in[layer][:, :D_MODEL])
        y2 = _ssm_call(u2, st, mt, rt, coef)
        ys = _ssm_out_call(y2, unat, nat, d_skip[layer], w_glu[layer], w_ssm_out[layer])
        attn = [_attn_call(nat[:, None], 0, (NAT_Q, NAT_K, NAT_V)),
                _attn_call(qkv4, 1, (0, 1, 2)),
                _attn_call(qkv16, 2, (0, 1, 2))]
        xf = _merge_call(xf, ys, nat, attn, w_attn_out[layer], w_o[layer])
    return xf.astype(x.dtype)
```

```python
import math

import jax
import jax.numpy as jnp
from jax import lax
from jax.experimental import pallas as pl
from jax.experimental.pallas import tpu as pltpu

F32 = jnp.float32
BF16 = jnp.bfloat16

D_MODEL = 1024
SSM_GROUP = 16
SSM_GROUPS = D_MODEL // SSM_GROUP
SSM_STATE = 64
CHUNK = 16
CHUNK_W = CHUNK * SSM_GROUP
HEAD_DIM = 64
HEADS = 8
ATTN_W = HEADS * HEAD_DIM
DILATIONS = (1, 4, 16)
WINDOW_KEYS = 128
ROPE_DIM = HEAD_DIM // 4
ROPE_THETA = 500000.0
EPS = 1e-6
NEG_INF = -1e30
LOG2E = math.log2(math.e)
GELU_K1 = -2.0 * math.sqrt(2.0 / math.pi) * LOG2E
GELU_K3 = GELU_K1 * 0.044715
COL = 512
ZS_BLK, Q_BLK, K_BLK, V_BLK, ZA_BLK, RS_BLK, RA_BLK = 2, 4, 7, 10, 13, 14, 16
N_BLK = 18
IN_WIDTH = N_BLK * COL
NAT_ZS, NAT_Q, NAT_K, NAT_V, NAT_ZA, NAT_RS, NAT_RA = 0, 2, 3, 4, 5, 6, 8
NAT_W = 10 * COL
QKV_W = 3 * COL
LANES = 128
MXU_DIM = 256
LSE_LANES = LANES // HEADS
LANE_BLKS = LANES // SSM_GROUP
SCAN_GROUPS = 2
SSM_CB = 8
SSM_SLOTS = 2
SSM_PITCH = 8 * 3
VMEM_LIMIT = 56 * 1024 * 1024


def _params(grid_rank):
    return pltpu.CompilerParams(dimension_semantics=("arbitrary",) * grid_rank, vmem_limit_bytes=VMEM_LIMIT)


def _full_spec(shape):
    nd = len(shape)
    return pl.BlockSpec(shape, lambda *_: (0,) * nd, pipeline_mode=pl.Buffered(1))


def _ssm_prep_kernel(lam_ref, btr_ref, bti_ref, cr_ref, ci_ref, dsk_ref, st_ref, mt_ref, rtt_ref, coef_ref):
    ns2 = 2 * SSM_STATE
    lam = lam_ref[0]
    lr, li, dt = lam[0:1, :], lam[1:2, :], jnp.exp(lam[2:3, :])
    mag = jnp.exp(lr * dt)
    ar = mag * jnp.cos(li * dt)
    ai = mag * jnp.sin(li * dt)
    den = lr * lr + li * li
    nr = ar - 1.0
    fr = (nr * lr + ai * li) / den
    fi = (ai * lr - nr * li) / den
    pr, pi = [jnp.ones_like(ar)], [jnp.zeros_like(ar)]
    for _ in range(CHUNK):
        pr, pi = pr + [pr[-1] * ar - pi[-1] * ai], pi + [pr[-1] * ai + pi[-1] * ar]
    by_step = lambda vals: jnp.concatenate([jnp.broadcast_to(v, (SSM_GROUP, ns2)) for v in vals], axis=0)
    per_step = lambda a: jnp.concatenate([a] * CHUNK, axis=0)
    im_part = lax.broadcasted_iota(jnp.int32, (CHUNK_W, ns2), 1) >= SSM_STATE

    btr, bti = per_step(btr_ref[0]), per_step(bti_ref[0])
    bbr = fr * btr - fi * bti
    bbi = fr * bti + fi * btr
    apr = by_step([pr[CHUNK - 1 - s] for s in range(CHUNK)])
    api = by_step([pi[CHUNK - 1 - s] for s in range(CHUNK)])
    wre = apr * bbr - api * bbi
    wim = apr * bbi + api * bbr
    st = jnp.where(im_part, wim, wre)
    st_sw = jnp.where(im_part, wre, wim)
    st_ref[0] = jnp.concatenate([st, st_sw], axis=1).astype(BF16)

    cr, ci = per_step(cr_ref[0]), per_step(ci_ref[0])
    qr = by_step([pr[t + 1] for t in range(CHUNK)])
    qi = by_step([pi[t + 1] for t in range(CHUNK)])
    rtt_ref[0] = jnp.where(im_part, -(cr * qi + ci * qr), cr * qr - ci * qi).astype(BF16)

    cct = jnp.where(im_part, -ci, cr)
    split = lambda v: (v.astype(BF16), (v - v.astype(BF16).astype(F32)).astype(BF16))
    (st_hi, st_lo), (cc_hi, cc_lo) = split(st), split(cct)
    nt = (((1,), (1,)), ((), ()))
    krw = (lax.dot_general(jnp.concatenate([st_hi, st_lo], axis=1), jnp.concatenate([cc_hi, cc_hi], axis=1), nt,
                           preferred_element_type=F32)
           + lax.dot_general(st_hi, cc_lo, nt, preferred_element_type=F32))
    lane_blk = lax.broadcasted_iota(jnp.int32, (CHUNK_W, CHUNK_W), 1) // SSM_GROUP
    mt = jnp.zeros((CHUNK_W, CHUNK_W), F32)
    for t in range(CHUNK):
        sh = SSM_GROUP * (CHUNK - 1 - t)
        if sh == 0:
            shifted = krw
        else:
            shifted = jnp.concatenate([krw[sh:, :], jnp.zeros((sh, CHUNK_W), F32)], axis=0)
        mt = jnp.where(lane_blk == t, shifted, mt)
    on_diag = (lax.broadcasted_iota(jnp.int32, (CHUNK_W, CHUNK_W), 0)
               == lax.broadcasted_iota(jnp.int32, (CHUNK_W, CHUNK_W), 1))
    mt_ref[0] = (mt + jnp.where(on_diag, dsk_ref[0, 0:1, :], 0.0)).astype(BF16)

    im_row = lax.broadcasted_iota(jnp.int32, (1, ns2), 1) >= SSM_STATE
    bco = jnp.where(im_row, pi[CHUNK], -pi[CHUNK])
    coef_ref[0] = jnp.concatenate([pr[CHUNK], bco, jnp.zeros((6, ns2), F32)], axis=0)


def _ssm_prep_call(lam_re, lam_im, log_dt, b_re, b_im, c_re, c_im, d_skip):
    g, n, p = SSM_GROUPS, SSM_STATE, SSM_GROUP
    dsk = jnp.tile(d_skip.astype(F32).reshape(g, 1, p), (1, 1, CHUNK))
    dsk = jnp.concatenate([dsk, jnp.zeros((g, 7, CHUNK_W), F32)], axis=1)
    f = lambda a: a.astype(F32)
    twice = lambda a: jnp.tile(f(a), (1,) * (a.ndim - 1) + (2,))
    ldt_b = jnp.broadcast_to(f(log_dt)[:, None], (g, n))
    lam = jnp.concatenate([twice(lam_re)[:, None], twice(lam_im)[:, None], twice(ldt_b)[:, None],
                           jnp.zeros((g, 5, 2 * n), F32)], axis=1)
    bt = lambda b: twice(b.transpose(0, 2, 1))
    gspec = lambda shape: pl.BlockSpec((1,) + shape, lambda i: (i, 0, 0))
    return pl.pallas_call(
        _ssm_prep_kernel,
        grid=(g,),
        in_specs=[gspec((8, 2 * n))] + [gspec((p, 2 * n))] * 4 + [gspec((8, CHUNK_W))],
        out_specs=[gspec((CHUNK_W, 4 * n)), gspec((CHUNK_W, CHUNK_W)), gspec((CHUNK_W, 2 * n)),
                   gspec((8, 2 * n))],
        out_shape=[jax.ShapeDtypeStruct((g, CHUNK_W, 4 * n), BF16),
                   jax.ShapeDtypeStruct((g, CHUNK_W, CHUNK_W), BF16),
                   jax.ShapeDtypeStruct((g, CHUNK_W, 2 * n), BF16),
                   jax.ShapeDtypeStruct((g, 8, 2 * n), F32)],
        compiler_params=_params(1),
        name="ssm_prep",
    )(lam, bt(b_re), bt(b_im), twice(c_re), twice(c_im), dsk)


def _rope_tables():
    lane = jnp.arange(LANES) % HEAD_DIM
    half = ROPE_DIM // 2
    inv = ROPE_THETA ** (-jnp.arange(0, ROPE_DIM, 2, dtype=F32) / ROPE_DIM)
    expand = ((lane[None, :] < ROPE_DIM) & (lane[None, :] % half == jnp.arange(half)[:, None])).astype(F32)
    s_lo = jnp.where(lane < half, -1.0, 0.0)
    s_hi = jnp.where((lane >= half) & (lane < ROPE_DIM), 1.0, 0.0)
    unrot = jnp.where(lane >= ROPE_DIM, 1.0, 0.0)
    rows = jnp.concatenate([s_lo[None], s_hi[None], unrot[None], jnp.zeros((5, LANES), F32)], axis=0)
    expand = jnp.concatenate([expand, expand], axis=0).astype(BF16)
    return inv[:, None].astype(F32), expand, rows.astype(F32)


def _sigmoid(v):
    return 1.0 / (1.0 + jnp.exp(-v))


_NAT_DST = {2: NAT_ZS, 3: NAT_ZS + 1, Q_BLK: NAT_Q, K_BLK: NAT_K, V_BLK: NAT_V,
            ZA_BLK: NAT_ZA, 14: NAT_RS, 15: NAT_RS + 1, 16: NAT_RA, 17: NAT_RA + 1}


_PROJ_ORDER = (2, 4, 3, 7, 13, 5, 14, 8, 15, 6, 16, 9, 17, 11, 12, 10)
_N_PERM = 6


def _proj_kernel(x_ref, pos_ref, nw_ref, w_ref, qkw_ref, ones_ref, freq_ref, expand_ref, rope_ref,
                 nat_ref, d4_ref, d16_ref, perm_ref, stage_ref, h_ref):
    tm = x_ref.shape[0]
    x = x_ref[...]
    ms = jnp.mean(x * x, axis=-1, keepdims=True)
    h_ref[...] = (x * lax.rsqrt(ms + EPS) * nw_ref[...]).astype(BF16)
    ang = freq_ref[...] * pos_ref[...].astype(F32)

    def spread(v):
        hi = v.astype(BF16)
        lo = (v - hi.astype(F32)).astype(BF16)
        return lax.dot_general(jnp.concatenate([hi, lo], axis=0), expand_ref[...], (((0,), (0,)), ((), ())),
                               preferred_element_type=F32)

    cosv = spread(jnp.cos(ang)) + rope_ref[2:3, :]
    sinv = spread(jnp.sin(ang))
    s_lo = sinv * rope_ref[0:1, :]
    s_hi = sinv * rope_ref[1:2, :]
    slot = 0
    for j in _PROJ_ORDER:
        acc = jnp.dot(h_ref[...], w_ref[:, j * COL:(j + 1) * COL], preferred_element_type=F32)
        if ZS_BLK <= j < Q_BLK or j == ZA_BLK:
            res = acc * _sigmoid(acc)
        elif j >= RS_BLK:
            res = _sigmoid(acc)
        elif Q_BLK <= j < V_BLK:
            sq = (acc * acc).astype(BF16)
            hw = ones_ref.shape[0]
            ss = jnp.concatenate([jnp.dot(sq[:, c * hw:(c + 1) * hw], ones_ref[...], preferred_element_type=F32)
                                  for c in range(COL // hw)], axis=1)
            y = acc * lax.rsqrt(ss * (1.0 / HEAD_DIM) + EPS) * qkw_ref[j - Q_BLK:j - Q_BLK + 1, :]
            parts = []
            for c in range(COL // LANES):
                yc = y[:, c * LANES:(c + 1) * LANES]
                parts.append(yc * cosv + pltpu.roll(yc, LANES - ROPE_DIM // 2, 1) * s_lo
                             + pltpu.roll(yc, ROPE_DIM // 2, 1) * s_hi)
            res = jnp.concatenate(parts, axis=1)
        else:
            res = acc
        if j in _NAT_DST:
            dst = _NAT_DST[j]
            nat_ref[:, dst * COL:(dst + 1) * COL] = res.astype(BF16)
        else:
            kind, group = divmod(j - Q_BLK, len(DILATIONS))
            d = DILATIONS[group]
            out_ref = d4_ref if group == 1 else d16_ref
            nslab = COL // LANES
            for c in range(nslab):
                perm_ref[slot, c] = res[:, c * LANES:(c + 1) * LANES]
            if d == DILATIONS[2]:
                d1 = DILATIONS[1]
                for c in range(nslab):
                    for r1 in range(d1):
                        stage_ref[kind, c, r1 * (tm // d1):(r1 + 1) * (tm // d1), :] = (
                            perm_ref[slot, c, pl.ds(r1, tm // d1, stride=d1), :])
                for r1, r2 in [(r1, r2) for r1 in range(d1) for r2 in range(d // d1)]:
                    rows = [stage_ref[kind, c, pl.ds(r1 * (tm // d1) + r2, tm // d, stride=d1), :]
                            for c in range(nslab)]
                    out_ref[r1 + d1 * r2, :, kind * COL:(kind + 1) * COL] = (
                        jnp.concatenate(rows, axis=1).astype(BF16))
            else:
                for r in range(d):
                    rows = [perm_ref[slot, c, pl.ds(r, tm // d, stride=d), :] for c in range(nslab)]
                    out_ref[r, :, kind * COL:(kind + 1) * COL] = jnp.concatenate(rows, axis=1).astype(BF16)
            slot += 1


def _proj_call(x, positions, norm_w, w_in, q_norm_w, k_norm_w, tm=512):
    bsz, length, _ = x.shape
    scale = LOG2E / math.sqrt(HEAD_DIM)
    qkw = jnp.concatenate([jnp.tile(q_norm_w.astype(F32) * scale, (1, HEADS)),
                           jnp.tile(k_norm_w.astype(F32), (1, HEADS)),
                           jnp.zeros((2, ATTN_W), F32)], axis=0)
    hid = jnp.arange(MXU_DIM) // HEAD_DIM
    ones = (hid[:, None] == hid[None, :]).astype(BF16)
    d4, d16 = DILATIONS[1], DILATIONS[2]
    freq, expand, rope_rows = _rope_tables()
    return pl.pallas_call(
        _proj_kernel,
        grid=(bsz, length // tm),
        in_specs=[pl.BlockSpec((None, tm, D_MODEL), lambda b, i: (b, i, 0)),
                  pl.BlockSpec((None, 1, tm), lambda b, i: (b, 0, i)),
                  _full_spec((1, D_MODEL)),
                  _full_spec((D_MODEL, IN_WIDTH)),
                  _full_spec((8, ATTN_W)),
                  _full_spec((MXU_DIM, MXU_DIM)),
                  _full_spec((ROPE_DIM // 2, 1)),
                  _full_spec((ROPE_DIM, LANES)),
                  _full_spec((8, LANES))],
        out_specs=[pl.BlockSpec((None, tm, NAT_W), lambda b, i: (b, i, 0)),
                   pl.BlockSpec((None, d4, tm // d4, QKV_W), lambda b, i: (b, 0, i, 0)),
                   pl.BlockSpec((None, d16, tm // d16, QKV_W), lambda b, i: (b, 0, i, 0))],
        out_shape=[jax.ShapeDtypeStruct((bsz, length, NAT_W), BF16),
                   jax.ShapeDtypeStruct((bsz, d4, length // d4, QKV_W), BF16),
                   jax.ShapeDtypeStruct((bsz, d16, length // d16, QKV_W), BF16)],
        scratch_shapes=[pltpu.VMEM((_N_PERM, COL // LANES, tm, LANES), F32),
                        pltpu.VMEM((3, COL // LANES, tm, LANES), F32),
                        pltpu.VMEM((tm, D_MODEL), BF16)],
        compiler_params=_params(2),
        name="proj",
    )(x.astype(F32), positions.astype(jnp.int32)[:, None, :], norm_w.astype(F32)[None, :],
      w_in.astype(BF16), qkw, ones, freq, expand, rope_rows)


def _block_transpose(xs):
    xs = list(xs)
    blk = lax.broadcasted_iota(jnp.int32, xs[0].shape, 1) // SSM_GROUP
    dist = 1
    while dist < LANE_BLKS:
        upper = (blk & dist) != 0
        shift = dist * SSM_GROUP
        for i in range(LANE_BLKS):
            if i & dist:
                continue
            a, b = xs[i], xs[i + dist]
            xs[i] = jnp.where(upper, pltpu.roll(b, shift, 1), a)
            xs[i + dist] = jnp.where(upper, b, pltpu.roll(a, LANES - shift, 1))
        dist *= 2
    return xs


def _ssm_in_kernel(x_ref, nw_ref, w_ref, u2_ref, rows_ref):
    bsz, tl = x_ref.shape[0], x_ref.shape[1]
    for c in range(tl // CHUNK):
        x = x_ref[:, c * CHUNK:(c + 1) * CHUNK, :].reshape(bsz * CHUNK, D_MODEL)
        ms = jnp.mean(x * x, axis=-1, keepdims=True)
        h = (x * lax.rsqrt(ms + EPS) * nw_ref[...]).astype(BF16)
        u = jnp.dot(h, w_ref[...], preferred_element_type=F32)
        for b in range(bsz):
            for s8 in range(D_MODEL // LANES):
                rows_ref[c % SSM_SLOTS, s8, b * SSM_PITCH:b * SSM_PITCH + CHUNK, :] = (
                    u[b * CHUNK:(b + 1) * CHUNK, s8 * LANES:(s8 + 1) * LANES])
        for s8 in range(D_MODEL // LANES):
            for half in range(CHUNK // LANE_BLKS):
                xs = []
                for k in range(LANE_BLKS):
                    v = rows_ref[c % SSM_SLOTS, s8, pl.ds(half * LANE_BLKS + k, bsz, stride=SSM_PITCH), :].astype(BF16)
                    xs.append(pltpu.bitcast(v, jnp.uint32))
                ys = _block_transpose(xs)
                for g in range(LANE_BLKS):
                    u2_ref[s8 * LANE_BLKS + g, c, :, half * LANES:(half + 1) * LANES] = pltpu.bitcast(ys[g], BF16)


def _ssm_in_call(x, norm_w, w_u):
    bsz, length, _ = x.shape
    tl = SSM_CB * CHUNK
    nc = length // CHUNK
    return pl.pallas_call(
        _ssm_in_kernel,
        grid=(length // tl,),
        in_specs=[pl.BlockSpec((bsz, tl, D_MODEL), lambda i: (0, i, 0)),
                  _full_spec((1, D_MODEL)), _full_spec((D_MODEL, D_MODEL))],
        out_specs=pl.BlockSpec((SSM_GROUPS, SSM_CB, bsz, CHUNK_W), lambda i: (0, i, 0, 0)),
        out_shape=jax.ShapeDtypeStruct((SSM_GROUPS, nc, bsz, CHUNK_W), BF16),
        scratch_shapes=[pltpu.VMEM((SSM_SLOTS, D_MODEL // LANES, bsz * SSM_PITCH, LANES), F32)],
        compiler_params=_params(1),
        name="ssm_in",
    )(x, norm_w.astype(F32)[None, :], w_u.astype(BF16))


def _ssm_out_kernel(y2_ref, zs_ref, rs_ref, wglu_ref, wso_ref, o_ref, rows_ref):
    bsz, tl = zs_ref.shape[0], zs_ref.shape[1]
    for c in range(tl // CHUNK):
        for s8 in range(D_MODEL // LANES):
            for half in range(CHUNK // LANE_BLKS):
                xs = [pltpu.bitcast(y2_ref[s8 * LANE_BLKS + g, c, :, half * LANES:(half + 1) * LANES], jnp.uint32)
                      for g in range(LANE_BLKS)]
                ys = _block_transpose(xs)
                for k in range(LANE_BLKS):
                    rows_ref[c % SSM_SLOTS, s8, pl.ds(half * LANE_BLKS + k, bsz, stride=SSM_PITCH), :] = (
                        pltpu.bitcast(ys[k], BF16).astype(F32))
        y = jnp.concatenate(
            [jnp.concatenate([rows_ref[c % SSM_SLOTS, s8, b * SSM_PITCH:b * SSM_PITCH + CHUNK, :]
                              for s8 in range(D_MODEL // LANES)], axis=1)
             for b in range(bsz)], axis=0)
        tok = slice(c * CHUNK, (c + 1) * CHUNK)
        zs = zs_ref[:, tok, :].reshape(bsz * CHUNK, D_MODEL).astype(F32)
        g = y / (1.0 + jnp.exp2(y * (GELU_K1 + GELU_K3 * (y * y))))
        gate = _sigmoid(jnp.dot(g.astype(BF16), wglu_ref[...], preferred_element_type=F32))
        ys_in = (g * gate * zs).astype(BF16)
        ys = jnp.dot(ys_in, wso_ref[...], preferred_element_type=F32)
        gated = ys * rs_ref[:, tok, :].reshape(bsz * CHUNK, D_MODEL).astype(F32)
        o_ref[:, tok, :] = gated.reshape(bsz, CHUNK, D_MODEL).astype(BF16)


def _ssm_out_call(y2, nat, w_glu, w_ssm_out):
    bsz, length, _ = nat.shape
    tl = SSM_CB * CHUNK
    tok = lambda blk: pl.BlockSpec((bsz, tl, D_MODEL), lambda i: (0, i, blk))
    return pl.pallas_call(
        _ssm_out_kernel,
        grid=(length // tl,),
        in_specs=[pl.BlockSpec((SSM_GROUPS, SSM_CB, bsz, CHUNK_W), lambda i: (0, i, 0, 0)),
                  tok(NAT_ZS // 2), tok(NAT_RS // 2),
                  _full_spec((D_MODEL, D_MODEL)), _full_spec((D_MODEL, D_MODEL))],
        out_specs=tok(0),
        out_shape=jax.ShapeDtypeStruct((bsz, length, D_MODEL), BF16),
        scratch_shapes=[pltpu.VMEM((SSM_SLOTS, D_MODEL // LANES, bsz * SSM_PITCH, LANES), F32)],
        compiler_params=_params(1),
        name="ssm_out",
    )(y2, nat, nat, w_glu.astype(BF16), w_ssm_out.astype(BF16))


def _ssm_kernel(u_ref, st_ref, mt_ref, rtt_ref, coef_ref, y_ref, hloc_ref, hprev_ref):
    ng, nc, bsz = u_ref.shape[0], u_ref.shape[1], u_ref.shape[2]
    ns2 = 2 * SSM_STATE
    us, coefs = [], []
    for g in range(ng):
        u = u_ref[g].reshape(nc * bsz, CHUNK_W)
        us.append(u)
        half = nc * bsz // 2
        hloc_ref[g, :half, :] = jnp.dot(u[:half], st_ref[g], preferred_element_type=F32)
        hloc_ref[g, half:, :] = jnp.dot(u[half:], st_ref[g], preferred_element_type=F32)
        coefs.append((jnp.broadcast_to(coef_ref[g, 0:1, :], (bsz, ns2)),
                      jnp.broadcast_to(coef_ref[g, 1:2, :], (bsz, ns2))))

    def step(c, carry):
        r = pl.multiple_of(c * bsz, bsz)
        out = []
        for g in range(ng):
            hx, hy = carry[2 * g], carry[2 * g + 1]
            a, b = coefs[g]
            hprev_ref[g, pl.ds(r, bsz), :] = hx.astype(BF16)
            px = hloc_ref[g, pl.ds(r, bsz), 0:ns2]
            py = hloc_ref[g, pl.ds(r, bsz), ns2:2 * ns2]
            out += [a * hx + b * hy + px, a * hy - b * hx + py]
        return tuple(out)

    zero = jnp.zeros((bsz, ns2), F32)
    lax.fori_loop(0, nc, step, (zero,) * (2 * ng), unroll=8)
    for g in range(ng):
        y = jnp.dot(us[g], mt_ref[g], preferred_element_type=F32)
        y = y + lax.dot_general(hprev_ref[g], rtt_ref[g], (((1,), (1,)), ((), ())), preferred_element_type=F32)
        y_ref[g] = y.reshape(nc, bsz, CHUNK_W).astype(BF16)


def _ssm_call(u2, st, mt, rt, coef):
    g, nc, bsz, _ = u2.shape
    ng = SCAN_GROUPS
    gspec = lambda shape: pl.BlockSpec((ng,) + shape, lambda i: (i,) + (0,) * len(shape))
    return pl.pallas_call(
        _ssm_kernel,
        grid=(g // ng,),
        in_specs=[gspec((nc, bsz, CHUNK_W)), gspec((CHUNK_W, 4 * SSM_STATE)), gspec((CHUNK_W, CHUNK_W)),
                  gspec((CHUNK_W, 2 * SSM_STATE)), gspec((8, 2 * SSM_STATE))],
        out_specs=gspec((nc, bsz, CHUNK_W)),
        out_shape=jax.ShapeDtypeStruct(u2.shape, BF16),
        scratch_shapes=[pltpu.VMEM((ng, nc * bsz, 4 * SSM_STATE), F32),
                        pltpu.VMEM((ng, nc * bsz, 2 * SSM_STATE), BF16)],
        compiler_params=_params(1),
        name="ssm_scan",
    )(u2, st, mt, rt, coef)


def _attn_kernel(q_ref, kc_ref, kp_ref, vc_ref, vp_ref, o_ref, lse_ref):
    nres, tq = q_ref.shape[0], q_ref.shape[1]
    qb = 128
    slab = pl.program_id(2)
    qi = lax.broadcasted_iota(jnp.int32, (qb, 2 * qb), 0)
    kk = lax.broadcasted_iota(jnp.int32, (qb, 2 * qb), 1)
    band = jnp.where((kk >= qi) & (kk <= qi + WINDOW_KEYS), 0.0, NEG_INF).astype(F32)
    band0 = band + jnp.where(jnp.logical_and(slab == 0, kk < qb), NEG_INF, 0.0).astype(F32)
    band = jnp.concatenate([band, band], axis=0)
    band0 = jnp.concatenate([band0, band0], axis=0)
    lo_q = lax.broadcasted_iota(jnp.int32, (qb, LANES), 1) < HEAD_DIM
    ones_kv = jnp.ones((2 * qb, LANES), BF16)

    head_slot = lax.broadcasted_iota(jnp.int32, (qb, LANES), 1) // LSE_LANES

    for r, j in [(r, j) for r in range(nres) for j in range(tq // qb)]:
        madd = band0 if j == 0 else band
        rows = slice(j * qb, (j + 1) * qb)
        m_all = jnp.zeros((qb, LANES), F32)
        den_all = jnp.ones((qb, LANES), F32)
        for hp in range(ATTN_W // LANES):
            cs = slice(hp * LANES, (hp + 1) * LANES)
            qp = q_ref[r, rows, cs]
            if j == 0:
                kp = jnp.concatenate([kp_ref[r, :, cs], kc_ref[r, 0:qb, cs]], axis=0)
                vp = jnp.concatenate([vp_ref[r, :, cs], vc_ref[r, 0:qb, cs]], axis=0)
            else:
                kp = kc_ref[r, (j - 1) * qb:(j + 1) * qb, cs]
                vp = vc_ref[r, (j - 1) * qb:(j + 1) * qb, cs]
            zero = jnp.zeros_like(qp)
            q2 = jnp.concatenate([jnp.where(lo_q, qp, zero), jnp.where(lo_q, zero, qp)], axis=0)
            s = lax.dot_general(q2, kp, (((1,), (1,)), ((), ())), preferred_element_type=F32)
            s = s + madd
            m = jnp.max(s, axis=1, keepdims=True)
            p = jnp.exp2(s - m)
            pv = jnp.dot(p.astype(BF16), jnp.concatenate([vp, ones_kv], axis=1), preferred_element_type=F32)
            num = jnp.where(lo_q, pv[:qb, :LANES], pv[qb:, :LANES])
            den = jnp.where(lo_q, pv[:qb, LANES:], pv[qb:, LANES:])
            o_ref[r, rows, cs] = (num / den).astype(BF16)
            for side in range(2):
                slot = head_slot == 2 * hp + side
                half = slice(side * qb, (side + 1) * qb)
                m_all = jnp.where(slot, m[half], m_all)
                den_all = jnp.where(slot, pv[half, LANES:], den_all)
        lse_ref[r, rows, :] = m_all + jnp.log2(den_all)


def _attn_call(qkv, group, blocks, rows_per_step=2048):
    bsz, d, ld, _ = qkv.shape
    tq = min(rows_per_step, ld)
    nres = min(d, rows_per_step // tq)
    nslab = ld // tq
    per = tq // 128
    cur = lambda blk: pl.BlockSpec((None, nres, tq, COL), lambda b, r, i: (b, r, i, blk))
    prev = lambda blk: pl.BlockSpec(
        (None, nres, 128, COL), lambda b, r, i: (b, r, jnp.maximum(i * per - 1, 0), blk))
    ospec = lambda w: pl.BlockSpec((None, nres, tq, w), lambda b, r, i: (b, r, i, 0))
    qb, kb, vb = blocks
    return pl.pallas_call(
        _attn_kernel,
        grid=(bsz, d // nres, nslab),
        in_specs=[cur(qb), cur(kb), prev(kb), cur(vb), prev(vb)],
        out_specs=[ospec(ATTN_W), ospec(LANES)],
        out_shape=[jax.ShapeDtypeStruct((bsz, d, ld, ATTN_W), BF16),
                   jax.ShapeDtypeStruct((bsz, d, ld, LANES), F32)],
        compiler_params=_params(3),
        name=f"attn_d{DILATIONS[group]}",
    )(qkv, qkv, qkv, qkv, qkv)


def _merge_kernel(x_ref, ys_ref, za_ref, ra_ref,
                  a0_ref, a1_ref, a2_ref, l0_ref, l1_ref, l2_ref,
                  spread_ref, wao_ref, wo_ref, o_ref, il_ref, stage_ref):
    tm = x_ref.shape[0]
    d1 = DILATIONS[1]

    def to_token_order(ref, base, stage_base=None):
        d, nslab = ref.shape[0], ref.shape[2] // LANES
        for r in range(d):
            blk = ref[r].astype(F32)
            for c in range(nslab):
                piece = blk[:, c * LANES:(c + 1) * LANES]
                if d == d1:
                    il_ref[base + c, pl.ds(r, tm // d, stride=d), :] = piece
                else:
                    r1, r2 = r % d1, r // d1
                    stage_ref[stage_base + c, pl.ds(r1 * (tm // d1) + r2, tm // d, stride=d1), :] = piece
        if d != d1:
            for c in range(nslab):
                for r1 in range(d1):
                    il_ref[base + c, pl.ds(r1, tm // d1, stride=d1), :] = (
                        stage_ref[stage_base + c, r1 * (tm // d1):(r1 + 1) * (tm // d1), :])
        return lambda rows: jnp.concatenate([il_ref[base + c, rows, :] for c in range(nslab)], axis=1)

    def per_head_to_lanes(w):
        hi = w.astype(BF16)
        lo = (w - hi.astype(F32)).astype(BF16)
        return jnp.dot(jnp.concatenate([hi, lo], axis=1), spread_ref[...], preferred_element_type=F32)

    wide_slabs = ATTN_W // LANES
    l1_at, l2_at = to_token_order(l1_ref, 0), to_token_order(l2_ref, 1, 0)
    a1_at, a2_at = to_token_order(a1_ref, 2), to_token_order(a2_ref, 2 + wide_slabs, 1)
    rows = slice(None)
    l0, l1, l2 = l0_ref[...], l1_at(rows), l2_at(rows)
    lm = jnp.maximum(jnp.maximum(l0, l1), l2)
    e0, e1, e2 = jnp.exp2(l0 - lm), jnp.exp2(l1 - lm), jnp.exp2(l2 - lm)
    inv = 1.0 / (e0 + e1 + e2)
    att = (per_head_to_lanes(e0 * inv) * a0_ref[...].astype(F32)
           + per_head_to_lanes(e1 * inv) * a1_at(rows)
           + per_head_to_lanes(e2 * inv) * a2_at(rows))
    ya_in = (att * za_ref[...].astype(F32)).astype(BF16)
    ya = jnp.dot(ya_in, wao_ref[...], preferred_element_type=F32)
    m = ys_ref[...].astype(F32) + ra_ref[...].astype(F32) * ya
    o_ref[...] = x_ref[...] + jnp.dot(m.astype(BF16), wo_ref[...], preferred_element_type=F32)


def _merge_call(x, ys, nat, attn, w_attn_out, w_o, tm=1024):
    bsz, length, _ = x.shape
    wide = lambda blk: pl.BlockSpec((None, tm, D_MODEL), lambda b, i: (b, i, blk))
    half = lambda blk: pl.BlockSpec((None, tm, ATTN_W), lambda b, i: (b, i, blk))
    res = lambda d, w: pl.BlockSpec((None, d, tm // d, w), lambda b, i: (b, 0, i, 0))
    (a0, l0), (a1, l1), (a2, l2) = attn
    d4, d16 = DILATIONS[1], DILATIONS[2]
    a0, l0 = a0.reshape(bsz, length, ATTN_W), l0.reshape(bsz, length, LANES)
    spread = (jnp.arange(LANES)[:, None] == (jnp.arange(ATTN_W)[None, :] // HEAD_DIM) * LSE_LANES).astype(BF16)
    spread = jnp.concatenate([spread, spread], axis=0)
    return pl.pallas_call(
        _merge_kernel,
        grid=(bsz, length // tm),
        in_specs=[wide(0), wide(0), half(NAT_ZA), wide(NAT_RA // 2),
                  half(0), res(d4, ATTN_W), res(d16, ATTN_W),
                  pl.BlockSpec((None, tm, LANES), lambda b, i: (b, i, 0)), res(d4, LANES), res(d16, LANES),
                  _full_spec((2 * LANES, ATTN_W)), _full_spec((ATTN_W, D_MODEL)), _full_spec((D_MODEL, D_MODEL))],
        out_specs=wide(0),
        out_shape=jax.ShapeDtypeStruct((bsz, length, D_MODEL), F32),
        scratch_shapes=[pltpu.VMEM((2 + 2 * (ATTN_W // LANES), tm, LANES), F32),
                        pltpu.VMEM((1 + ATTN_W // LANES, tm, LANES), F32)],
        compiler_params=_params(2),
        name="merge",
    )(x, ys, nat, nat, a0, a1, a2, l0, l1, l2, spread, w_attn_out.astype(BF16), w_o.astype(BF16))


def kernel(x, positions, norm_w, w_in, lam_re, lam_im, log_dt, b_re, b_im, c_re, c_im, d_skip, w_glu,
           q_norm_w, k_norm_w, w_ssm_out, w_attn_out, w_o):
    bsz, length, width = x.shape
    assert width == D_MODEL and w_in.shape[1:] == (D_MODEL, IN_WIDTH)
    assert bsz % 16 == 0, "the scan layout packs the batch into 16-row bf16 tiles"
    assert length % (128 * DILATIONS[-1]) == 0, "every residue class must hold whole 128-query blocks"
    xf = x.astype(F32)
    for layer in range(norm_w.shape[0]):
        st, mt, rt, coef = _ssm_prep_call(lam_re[layer], lam_im[layer], log_dt[layer], b_re[layer],
                                          b_im[layer], c_re[layer], c_im[layer], d_skip[layer])
        nat, qkv4, qkv16 = _proj_call(xf, positions, norm_w[layer], w_in[layer], q_norm_w[layer],
                                      k_norm_w[layer])
        u2 = _ssm_in_call(xf, norm_w[layer], w_in[layer][:, :D_MODEL])
        y2 = _ssm_call(u2, st, mt, rt, coef)
        ys = _ssm_out_call(y2, nat, w_glu[layer], w_ssm_out[layer])
        attn = [_attn_call(nat[:, None], 0, (NAT_Q, NAT_K, NAT_V)),
                _attn_call(qkv4, 1, (0, 1, 2)),
                _attn_call(qkv16, 2, (0, 1, 2))]
        xf = _merge_call(xf, ys, nat, attn, w_attn_out[layer], w_o[layer])
    return xf.astype(x.dtype)
```

```python
import math

import jax
import jax.numpy as jnp
from jax import lax
from jax.experimental import pallas as pl
from jax.experimental.pallas import tpu as pltpu

F32 = jnp.float32
BF16 = jnp.bfloat16

D_MODEL = 1024
SSM_GROUP = 16
SSM_GROUPS = D_MODEL // SSM_GROUP
SSM_STATE = 64
CHUNK = 16
CHUNK_W = CHUNK * SSM_GROUP
HEAD_DIM = 64
HEADS = 8
ATTN_W = HEADS * HEAD_DIM
DILATIONS = (1, 4, 16)
WINDOW_KEYS = 128
ROPE_DIM = HEAD_DIM // 4
ROPE_THETA = 500000.0
EPS = 1e-6
NEG_INF = -1e30
LOG2E = math.log2(math.e)
GELU_K1 = -2.0 * math.sqrt(2.0 / math.pi) * LOG2E
GELU_K3 = GELU_K1 * 0.044715
COL = 512
ZS_BLK, Q_BLK, K_BLK, V_BLK, ZA_BLK, RS_BLK, RA_BLK = 2, 4, 7, 10, 13, 14, 16
N_BLK = 18
IN_WIDTH = N_BLK * COL
NAT_ZS, NAT_Q, NAT_K, NAT_V, NAT_ZA, NAT_RS, NAT_RA = 0, 2, 3, 4, 5, 6, 8
NAT_W = 10 * COL
QKV_W = 3 * COL
LANES = 128
MXU_DIM = 256
LSE_LANES = LANES // HEADS
LANE_BLKS = LANES // SSM_GROUP
SCAN_GROUPS = 2
SSM_CB = 8
SSM_SLOTS = 2
SSM_PITCH = 8 * 3
VMEM_LIMIT = 56 * 1024 * 1024


def _params(grid_rank):
    return pltpu.CompilerParams(dimension_semantics=("arbitrary",) * grid_rank, vmem_limit_bytes=VMEM_LIMIT)


def _full_spec(shape):
    nd = len(shape)
    return pl.BlockSpec(shape, lambda *_: (0,) * nd, pipeline_mode=pl.Buffered(1))


def _ssm_prep_kernel(lam_ref, btr_ref, bti_ref, cr_ref, ci_ref, dsk_ref, st_ref, mt_ref, rtt_ref, coef_ref):
    ns2 = 2 * SSM_STATE
    lam = lam_ref[0]
    lr, li, dt = lam[0:1, :], lam[1:2, :], jnp.exp(lam[2:3, :])
    mag = jnp.exp(lr * dt)
    ar = mag * jnp.cos(li * dt)
    ai = mag * jnp.sin(li * dt)
    den = lr * lr + li * li
    nr = ar - 1.0
    fr = (nr * lr + ai * li) / den
    fi = (ai * lr - nr * li) / den
    pr, pi = [jnp.ones_like(ar)], [jnp.zeros_like(ar)]
    for _ in range(CHUNK):
        pr, pi = pr + [pr[-1] * ar - pi[-1] * ai], pi + [pr[-1] * ai + pi[-1] * ar]
    by_step = lambda vals: jnp.concatenate([jnp.broadcast_to(v, (SSM_GROUP, ns2)) for v in vals], axis=0)
    per_step = lambda a: jnp.concatenate([a] * CHUNK, axis=0)
    im_part = lax.broadcasted_iota(jnp.int32, (CHUNK_W, ns2), 1) >= SSM_STATE

    btr, bti = per_step(btr_ref[0]), per_step(bti_ref[0])
    bbr = fr * btr - fi * bti
    bbi = fr * bti + fi * btr
    apr = by_step([pr[CHUNK - 1 - s] for s in range(CHUNK)])
    api = by_step([pi[CHUNK - 1 - s] for s in range(CHUNK)])
    wre = apr * bbr - api * bbi
    wim = apr * bbi + api * bbr
    st = jnp.where(im_part, wim, wre)
    st_sw = jnp.where(im_part, wre, wim)
    st_ref[0] = jnp.concatenate([st, st_sw], axis=1).astype(BF16)

    cr, ci = per_step(cr_ref[0]), per_step(ci_ref[0])
    qr = by_step([pr[t + 1] for t in range(CHUNK)])
    qi = by_step([pi[t + 1] for t in range(CHUNK)])
    rtt_ref[0] = jnp.where(im_part, -(cr * qi + ci * qr), cr * qr - ci * qi).astype(BF16)

    cct = jnp.where(im_part, -ci, cr)
    split = lambda v: (v.astype(BF16), (v - v.astype(BF16).astype(F32)).astype(BF16))
    (st_hi, st_lo), (cc_hi, cc_lo) = split(st), split(cct)
    nt = (((1,), (1,)), ((), ()))
    krw = (lax.dot_general(jnp.concatenate([st_hi, st_lo], axis=1), jnp.concatenate([cc_hi, cc_hi], axis=1), nt,
                           preferred_element_type=F32)
           + lax.dot_general(st_hi, cc_lo, nt, preferred_element_type=F32))
    lane_blk = lax.broadcasted_iota(jnp.int32, (CHUNK_W, CHUNK_W), 1) // SSM_GROUP
    mt = jnp.zeros((CHUNK_W, CHUNK_W), F32)
    for t in range(CHUNK):
        sh = SSM_GROUP * (CHUNK - 1 - t)
        if sh == 0:
            shifted = krw
        else:
            shifted = jnp.concatenate([krw[sh:, :], jnp.zeros((sh, CHUNK_W), F32)], axis=0)
        mt = jnp.where(lane_blk == t, shifted, mt)
    on_diag = (lax.broadcasted_iota(jnp.int32, (CHUNK_W, CHUNK_W), 0)
               == lax.broadcasted_iota(jnp.int32, (CHUNK_W, CHUNK_W), 1))
    mt_ref[0] = (mt + jnp.where(on_diag, dsk_ref[0, 0:1, :], 0.0)).astype(BF16)

    im_row = lax.broadcasted_iota(jnp.int32, (1, ns2), 1) >= SSM_STATE
    bco = jnp.where(im_row, pi[CHUNK], -pi[CHUNK])
    coef_ref[0] = jnp.concatenate([pr[CHUNK], bco, jnp.zeros((6, ns2), F32)], axis=0)


def _ssm_prep_call(lam_re, lam_im, log_dt, b_re, b_im, c_re, c_im, d_skip):
    g, n, p = SSM_GROUPS, SSM_STATE, SSM_GROUP
    dsk = jnp.tile(d_skip.astype(F32).reshape(g, 1, p), (1, 1, CHUNK))
    dsk = jnp.concatenate([dsk, jnp.zeros((g, 7, CHUNK_W), F32)], axis=1)
    f = lambda a: a.astype(F32)
    twice = lambda a: jnp.tile(f(a), (1,) * (a.ndim - 1) + (2,))
    ldt_b = jnp.broadcast_to(f(log_dt)[:, None], (g, n))
    lam = jnp.concatenate([twice(lam_re)[:, None], twice(lam_im)[:, None], twice(ldt_b)[:, None],
                           jnp.zeros((g, 5, 2 * n), F32)], axis=1)
    bt = lambda b: twice(b.transpose(0, 2, 1))
    gspec = lambda shape: pl.BlockSpec((1,) + shape, lambda i: (i, 0, 0))
    return pl.pallas_call(
        _ssm_prep_kernel,
        grid=(g,),
        in_specs=[gspec((8, 2 * n))] + [gspec((p, 2 * n))] * 4 + [gspec((8, CHUNK_W))],
        out_specs=[gspec((CHUNK_W, 4 * n)), gspec((CHUNK_W, CHUNK_W)), gspec((CHUNK_W, 2 * n)),
                   gspec((8, 2 * n))],
        out_shape=[jax.ShapeDtypeStruct((g, CHUNK_W, 4 * n), BF16),
                   jax.ShapeDtypeStruct((g, CHUNK_W, CHUNK_W), BF16),
                   jax.ShapeDtypeStruct((g, CHUNK_W, 2 * n), BF16),
                   jax.ShapeDtypeStruct((g, 8, 2 * n), F32)],
        compiler_params=_params(1),
        name="ssm_prep",
    )(lam, bt(b_re), bt(b_im), twice(c_re), twice(c_im), dsk)


def _rope_tables():
    lane = jnp.arange(LANES) % HEAD_DIM
    half = ROPE_DIM // 2
    inv = ROPE_THETA ** (-jnp.arange(0, ROPE_DIM, 2, dtype=F32) / ROPE_DIM)
    expand = ((lane[None, :] < ROPE_DIM) & (lane[None, :] % half == jnp.arange(half)[:, None])).astype(F32)
    s_lo = jnp.where(lane < half, -1.0, 0.0)
    s_hi = jnp.where((lane >= half) & (lane < ROPE_DIM), 1.0, 0.0)
    unrot = jnp.where(lane >= ROPE_DIM, 1.0, 0.0)
    rows = jnp.concatenate([s_lo[None], s_hi[None], unrot[None], jnp.zeros((5, LANES), F32)], axis=0)
    expand = jnp.concatenate([expand, expand], axis=0).astype(BF16)
    return inv[:, None].astype(F32), expand, rows.astype(F32)


def _sigmoid(v):
    return 1.0 / (1.0 + jnp.exp(-v))


_NAT_DST = {2: NAT_ZS, 3: NAT_ZS + 1, Q_BLK: NAT_Q, K_BLK: NAT_K, V_BLK: NAT_V,
            ZA_BLK: NAT_ZA, 14: NAT_RS, 15: NAT_RS + 1, 16: NAT_RA, 17: NAT_RA + 1}


_PROJ_ORDER = (2, 4, 3, 7, 13, 5, 14, 8, 15, 6, 16, 9, 17, 11, 12, 10)
_N_PERM = 6


def _proj_kernel(x_ref, pos_ref, nw_ref, w_ref, qkw_ref, ones_ref, freq_ref, expand_ref, rope_ref,
                 nat_ref, d4_ref, d16_ref, perm_ref, stage_ref, h_ref):
    tm = x_ref.shape[0]
    x = x_ref[...]
    ms = jnp.mean(x * x, axis=-1, keepdims=True)
    h_ref[...] = (x * lax.rsqrt(ms + EPS) * nw_ref[...]).astype(BF16)
    ang = freq_ref[...] * pos_ref[...].astype(F32)

    def spread(v):
        hi = v.astype(BF16)
        lo = (v - hi.astype(F32)).astype(BF16)
        return lax.dot_general(jnp.concatenate([hi, lo], axis=0), expand_ref[...], (((0,), (0,)), ((), ())),
                               preferred_element_type=F32)

    cosv = spread(jnp.cos(ang)) + rope_ref[2:3, :]
    sinv = spread(jnp.sin(ang))
    s_lo = sinv * rope_ref[0:1, :]
    s_hi = sinv * rope_ref[1:2, :]
    slot = 0
    for j in _PROJ_ORDER:
        acc = jnp.dot(h_ref[...], w_ref[:, j * COL:(j + 1) * COL], preferred_element_type=F32)
        if ZS_BLK <= j < Q_BLK or j == ZA_BLK:
            res = acc * _sigmoid(acc)
        elif j >= RS_BLK:
            res = _sigmoid(acc)
        elif Q_BLK <= j < V_BLK:
            sq = (acc * acc).astype(BF16)
            hw = ones_ref.shape[0]
            ss = jnp.concatenate([jnp.dot(sq[:, c * hw:(c + 1) * hw], ones_ref[...], preferred_element_type=F32)
                                  for c in range(COL // hw)], axis=1)
            y = acc * lax.rsqrt(ss * (1.0 / HEAD_DIM) + EPS) * qkw_ref[j - Q_BLK:j - Q_BLK + 1, :]
            parts = []
            for c in range(COL // LANES):
                yc = y[:, c * LANES:(c + 1) * LANES]
                parts.append(yc * cosv + pltpu.roll(yc, LANES - ROPE_DIM // 2, 1) * s_lo
                             + pltpu.roll(yc, ROPE_DIM // 2, 1) * s_hi)
            res = jnp.concatenate(parts, axis=1)
        else:
            res = acc
        if j in _NAT_DST:
            dst = _NAT_DST[j]
            nat_ref[:, dst * COL:(dst + 1) * COL] = res.astype(BF16)
        else:
            kind, group = divmod(j - Q_BLK, len(DILATIONS))
            d = DILATIONS[group]
            out_ref = d4_ref if group == 1 else d16_ref
            nslab = COL // LANES
            for c in range(nslab):
                perm_ref[slot, c] = res[:, c * LANES:(c + 1) * LANES]
            if d == DILATIONS[2]:
                d1 = DILATIONS[1]
                for c in range(nslab):
                    for r1 in range(d1):
                        stage_ref[kind, c, r1 * (tm // d1):(r1 + 1) * (tm // d1), :] = (
                            perm_ref[slot, c, pl.ds(r1, tm // d1, stride=d1), :])
                for r1, r2 in [(r1, r2) for r1 in range(d1) for r2 in range(d // d1)]:
                    rows = [stage_ref[kind, c, pl.ds(r1 * (tm // d1) + r2, tm // d, stride=d1), :]
                            for c in range(nslab)]
                    out_ref[r1 + d1 * r2, :, kind * COL:(kind + 1) * COL] = (
                        jnp.concatenate(rows, axis=1).astype(BF16))
            else:
                for r in range(d):
                    rows = [perm_ref[slot, c, pl.ds(r, tm // d, stride=d), :] for c in range(nslab)]
                    out_ref[r, :, kind * COL:(kind + 1) * COL] = jnp.concatenate(rows, axis=1).astype(BF16)
            slot += 1


def _proj_call(x, positions, norm_w, w_in, q_norm_w, k_norm_w, tm=512):
    bsz, length, _ = x.shape
    scale = LOG2E / math.sqrt(HEAD_DIM)
    qkw = jnp.concatenate([jnp.tile(q_norm_w.astype(F32) * scale, (1, HEADS)),
                           jnp.tile(k_norm_w.astype(F32), (1, HEADS)),
                           jnp.zeros((2, ATTN_W), F32)], axis=0)
    hid = jnp.arange(MXU_DIM) // HEAD_DIM
    ones = (hid[:, None] == hid[None, :]).astype(BF16)
    d4, d16 = DILATIONS[1], DILATIONS[2]
    freq, expand, rope_rows = _rope_tables()
    return pl.pallas_call(
        _proj_kernel,
        grid=(bsz, length // tm),
        in_specs=[pl.BlockSpec((None, tm, D_MODEL), lambda b, i: (b, i, 0)),
                  pl.BlockSpec((None, 1, tm), lambda b, i: (b, 0, i)),
                  _full_spec((1, D_MODEL)),
                  _full_spec((D_MODEL, IN_WIDTH)),
                  _full_spec((8, ATTN_W)),
                  _full_spec((MXU_DIM, MXU_DIM)),
                  _full_spec((ROPE_DIM // 2, 1)),
                  _full_spec((ROPE_DIM, LANES)),
                  _full_spec((8, LANES))],
        out_specs=[pl.BlockSpec((None, tm, NAT_W), lambda b, i: (b, i, 0)),
                   pl.BlockSpec((None, d4, tm // d4, QKV_W), lambda b, i: (b, 0, i, 0)),
                   pl.BlockSpec((None, d16, tm // d16, QKV_W), lambda b, i: (b, 0, i, 0))],
        out_shape=[jax.ShapeDtypeStruct((bsz, length, NAT_W), BF16),
                   jax.ShapeDtypeStruct((bsz, d4, length // d4, QKV_W), BF16),
                   jax.ShapeDtypeStruct((bsz, d16, length // d16, QKV_W), BF16)],
        scratch_shapes=[pltpu.VMEM((_N_PERM, COL // LANES, tm, LANES), F32),
                        pltpu.VMEM((3, COL // LANES, tm, LANES), F32),
                        pltpu.VMEM((tm, D_MODEL), BF16)],
        compiler_params=_params(2),
        name="proj",
    )(x.astype(F32), positions.astype(jnp.int32)[:, None, :], norm_w.astype(F32)[None, :],
      w_in.astype(BF16), qkw, ones, freq, expand, rope_rows)


def _block_transpose(xs):
    xs = list(xs)
    blk = lax.broadcasted_iota(jnp.int32, xs[0].shape, 1) // SSM_GROUP
    dist = 1
    while dist < LANE_BLKS:
        upper = (blk & dist) != 0
        shift = dist * SSM_GROUP
        for i in range(LANE_BLKS):
            if i & dist:
                continue
            a, b = xs[i], xs[i + dist]
            xs[i] = jnp.where(upper, pltpu.roll(b, shift, 1), a)
            xs[i + dist] = jnp.where(upper, b, pltpu.roll(a, LANES - shift, 1))
        dist *= 2
    return xs


def _ssm_in_kernel(x_ref, nw_ref, w_ref, u2_ref, rows_ref):
    bsz, tl = x_ref.shape[0], x_ref.shape[1]
    for c in range(tl // CHUNK):
        x = x_ref[:, c * CHUNK:(c + 1) * CHUNK, :].reshape(bsz * CHUNK, D_MODEL)
        ms = jnp.mean(x * x, axis=-1, keepdims=True)
        h = (x * lax.rsqrt(ms + EPS) * nw_ref[...]).astype(BF16)
        u = jnp.dot(h, w_ref[...], preferred_element_type=F32)
        for b in range(bsz):
            for s8 in range(D_MODEL // LANES):
                rows_ref[c % SSM_SLOTS, s8, b * SSM_PITCH:b * SSM_PITCH + CHUNK, :] = (
                    u[b * CHUNK:(b + 1) * CHUNK, s8 * LANES:(s8 + 1) * LANES])
        for s8 in range(D_MODEL // LANES):
            for half in range(CHUNK // LANE_BLKS):
                xs = []
                for k in range(LANE_BLKS):
                    v = rows_ref[c % SSM_SLOTS, s8, pl.ds(half * LANE_BLKS + k, bsz, stride=SSM_PITCH), :].astype(BF16)
                    xs.append(pltpu.bitcast(v, jnp.uint32))
                ys = _block_transpose(xs)
                for g in range(LANE_BLKS):
                    u2_ref[s8 * LANE_BLKS + g, c, :, half * LANES:(half + 1) * LANES] = pltpu.bitcast(ys[g], BF16)


def _ssm_in_call(x, norm_w, w_u):
    bsz, length, _ = x.shape
    tl = SSM_CB * CHUNK
    nc = length // CHUNK
    return pl.pallas_call(
        _ssm_in_kernel,
        grid=(length // tl,),
        in_specs=[pl.BlockSpec((bsz, tl, D_MODEL), lambda i: (0, i, 0)),
                  _full_spec((1, D_MODEL)), _full_spec((D_MODEL, D_MODEL))],
        out_specs=pl.BlockSpec((SSM_GROUPS, SSM_CB, bsz, CHUNK_W), lambda i: (0, i, 0, 0)),
        out_shape=jax.ShapeDtypeStruct((SSM_GROUPS, nc, bsz, CHUNK_W), BF16),
        scratch_shapes=[pltpu.VMEM((SSM_SLOTS, D_MODEL // LANES, bsz * SSM_PITCH, LANES), F32)],
        compiler_params=_params(1),
        name="ssm_in",
    )(x, norm_w.astype(F32)[None, :], w_u.astype(BF16))


def _ssm_out_kernel(y2_ref, zs_ref, rs_ref, wglu_ref, wso_ref, o_ref, rows_ref):
    bsz, tl = zs_ref.shape[0], zs_ref.shape[1]
    for c in range(tl // CHUNK):
        for s8 in range(D_MODEL // LANES):
            for half in range(CHUNK // LANE_BLKS):
                xs = [pltpu.bitcast(y2_ref[s8 * LANE_BLKS + g, c, :, half * LANES:(half + 1) * LANES], jnp.uint32)
                      for g in range(LANE_BLKS)]
                ys = _block_transpose(xs)
                for k in range(LANE_BLKS):
                    rows_ref[c % SSM_SLOTS, s8, pl.ds(half * LANE_BLKS + k, bsz, stride=SSM_PITCH), :] = (
                        pltpu.bitcast(ys[k], BF16).astype(F32))
        y = jnp.concatenate(
            [jnp.concatenate([rows_ref[c % SSM_SLOTS, s8, b * SSM_PITCH:b * SSM_PITCH + CHUNK, :]
                              for s8 in range(D_MODEL // LANES)], axis=1)
             for b in range(bsz)], axis=0)
        tok = slice(c * CHUNK, (c + 1) * CHUNK)
        zs = zs_ref[:, tok, :].reshape(bsz * CHUNK, D_MODEL).astype(F32)
        g = y / (1.0 + jnp.exp2(y * (GELU_K1 + GELU_K3 * (y * y))))
        gate = _sigmoid(jnp.dot(g.astype(BF16), wglu_ref[...], preferred_element_type=F32))
        ys_in = (g * gate * zs).astype(BF16)
        ys = jnp.dot(ys_in, wso_ref[...], preferred_element_type=F32)
        gated = ys * rs_ref[:, tok, :].reshape(bsz * CHUNK, D_MODEL).astype(F32)
        o_ref[:, tok, :] = gated.reshape(bsz, CHUNK, D_MODEL).astype(BF16)


def _ssm_out_call(y2, nat, w_glu, w_ssm_out):
    bsz, length, _ = nat.shape
    tl = SSM_CB * CHUNK
    tok = lambda blk: pl.BlockSpec((bsz, tl, D_MODEL), lambda i: (0, i, blk))
    return pl.pallas_call(
        _ssm_out_kernel,
        grid=(length // tl,),
        in_specs=[pl.BlockSpec((SSM_GROUPS, SSM_CB, bsz, CHUNK_W), lambda i: (0, i, 0, 0)),
                  tok(NAT_ZS // 2), tok(NAT_RS // 2),
                  _full_spec((D_MODEL, D_MODEL)), _full_spec((D_MODEL, D_MODEL))],
        out_specs=tok(0),
        out_shape=jax.ShapeDtypeStruct((bsz, length, D_MODEL), BF16),
        scratch_shapes=[pltpu.VMEM((SSM_SLOTS, D_MODEL // LANES, bsz * SSM_PITCH, LANES), F32)],
        compiler_params=_params(1),
        name="ssm_out",
    )(y2, nat, nat, w_glu.astype(BF16), w_ssm_out.astype(BF16))


def _ssm_kernel(u_ref, st_ref, mt_ref, rtt_ref, coef_ref, y_ref, hloc_ref, hprev_ref):
    ng, nc, bsz = u_ref.shape[0], u_ref.shape[1], u_ref.shape[2]
    ns2 = 2 * SSM_STATE
    us, coefs = [], []
    for g in range(ng):
        u = u_ref[g].reshape(nc * bsz, CHUNK_W)
        us.append(u)
        half = nc * bsz // 2
        hloc_ref[g, :half, :] = jnp.dot(u[:half], st_ref[g], preferred_element_type=F32)
        hloc_ref[g, half:, :] = jnp.dot(u[half:], st_ref[g], preferred_element_type=F32)
        coefs.append((jnp.broadcast_to(coef_ref[g, 0:1, :], (bsz, ns2)),
                      jnp.broadcast_to(coef_ref[g, 1:2, :], (bsz, ns2))))

    def step(c, carry):
        r = pl.multiple_of(c * bsz, bsz)
        out = []
        for g in range(ng):
            hx, hy = carry[2 * g], carry[2 * g + 1]
            a, b = coefs[g]
            hprev_ref[g, pl.ds(r, bsz), :] = hx.astype(BF16)
            px = hloc_ref[g, pl.ds(r, bsz), 0:ns2]
            py = hloc_ref[g, pl.ds(r, bsz), ns2:2 * ns2]
            out += [a * hx + b * hy + px, a * hy - b * hx + py]
        return tuple(out)

    zero = jnp.zeros((bsz, ns2), F32)
    lax.fori_loop(0, nc, step, (zero,) * (2 * ng), unroll=8)
    for g in range(ng):
        y = jnp.dot(us[g], mt_ref[g], preferred_element_type=F32)
        y = y + lax.dot_general(hprev_ref[g], rtt_ref[g], (((1,), (1,)), ((), ())), preferred_element_type=F32)
        y_ref[g] = y.reshape(nc, bsz, CHUNK_W).astype(BF16)


def _ssm_call(u2, st, mt, rt, coef):
    g, nc, bsz, _ = u2.shape
    ng = SCAN_GROUPS
    gspec = lambda shape: pl.BlockSpec((ng,) + shape, lambda i: (i,) + (0,) * len(shape))
    return pl.pallas_call(
        _ssm_kernel,
        grid=(g // ng,),
        in_specs=[gspec((nc, bsz, CHUNK_W)), gspec((CHUNK_W, 4 * SSM_STATE)), gspec((CHUNK_W, CHUNK_W)),
                  gspec((CHUNK_W, 2 * SSM_STATE)), gspec((8, 2 * SSM_STATE))],
        out_specs=gspec((nc, bsz, CHUNK_W)),
        out_shape=jax.ShapeDtypeStruct(u2.shape, BF16),
        scratch_shapes=[pltpu.VMEM((ng, nc * bsz, 4 * SSM_STATE), F32),
                        pltpu.VMEM((ng, nc * bsz, 2 * SSM_STATE), BF16)],
        compiler_params=_params(1),
        name="ssm_scan",
    )(u2, st, mt, rt, coef)


def _attn_kernel(q_ref, kc_ref, kp_ref, vc_ref, vp_ref, o_ref, lse_ref):
    nres, tq = q_ref.shape[0], q_ref.shape[1]
    qb = 128
    slab = pl.program_id(2)
    qi = lax.broadcasted_iota(jnp.int32, (qb, 2 * qb), 0)
    kk = lax.broadcasted_iota(jnp.int32, (qb, 2 * qb), 1)
    band = jnp.where((kk >= qi) & (kk <= qi + WINDOW_KEYS), 0.0, NEG_INF).astype(F32)
    band0 = band + jnp.where(jnp.logical_and(slab == 0, kk < qb), NEG_INF, 0.0).astype(F32)
    band = jnp.concatenate([band, band], axis=0)
    band0 = jnp.concatenate([band0, band0], axis=0)
    lo_q = lax.broadcasted_iota(jnp.int32, (qb, LANES), 1) < HEAD_DIM
    ones_kv = jnp.ones((2 * qb, LANES), BF16)

    head_slot = lax.broadcasted_iota(jnp.int32, (qb, LANES), 1) // LSE_LANES

    for r, j in [(r, j) for r in range(nres) for j in range(tq // qb)]:
        madd = band0 if j == 0 else band
        rows = slice(j * qb, (j + 1) * qb)
        m_all = jnp.zeros((qb, LANES), F32)
        den_all = jnp.ones((qb, LANES), F32)
        for hp in range(ATTN_W // LANES):
            cs = slice(hp * LANES, (hp + 1) * LANES)
            qp = q_ref[r, rows, cs]
            if j == 0:
                kp = jnp.concatenate([kp_ref[r, :, cs], kc_ref[r, 0:qb, cs]], axis=0)
                vp = jnp.concatenate([vp_ref[r, :, cs], vc_ref[r, 0:qb, cs]], axis=0)
            else:
                kp = kc_ref[r, (j - 1) * qb:(j + 1) * qb, cs]
                vp = vc_ref[r, (j - 1) * qb:(j + 1) * qb, cs]
            zero = jnp.zeros_like(qp)
            q2 = jnp.concatenate([jnp.where(lo_q, qp, zero), jnp.where(lo_q, zero, qp)], axis=0)
            s = lax.dot_general(q2, kp, (((1,), (1,)), ((), ())), preferred_element_type=F32)
            s = s + madd
            m = jnp.max(s, axis=1, keepdims=True)
            p = jnp.exp2(s - m)
            pv = jnp.dot(p.astype(BF16), jnp.concatenate([vp, ones_kv], axis=1), preferred_element_type=F32)
            num = jnp.where(lo_q, pv[:qb, :LANES], pv[qb:, :LANES])
            den = jnp.where(lo_q, pv[:qb, LANES:], pv[qb:, LANES:])
            o_ref[r, rows, cs] = (num / den).astype(BF16)
            for side in range(2):
                slot = head_slot == 2 * hp + side
                half = slice(side * qb, (side + 1) * qb)
                m_all = jnp.where(slot, m[half], m_all)
                den_all = jnp.where(slot, pv[half, LANES:], den_all)
        lse_ref[r, rows, :] = m_all + jnp.log2(den_all)


def _attn_call(qkv, group, blocks, rows_per_step=4096):
    bsz, d, ld, _ = qkv.shape
    tq = min(rows_per_step, ld)
    nres = min(d, rows_per_step // tq)
    nslab = ld // tq
    per = tq // 128
    cur = lambda blk: pl.BlockSpec((None, nres, tq, COL), lambda b, r, i: (b, r, i, blk))
    prev = lambda blk: pl.BlockSpec(
        (None, nres, 128, COL), lambda b, r, i: (b, r, jnp.maximum(i * per - 1, 0), blk))
    ospec = lambda w: pl.BlockSpec((None, nres, tq, w), lambda b, r, i: (b, r, i, 0))
    qb, kb, vb = blocks
    return pl.pallas_call(
        _attn_kernel,
        grid=(bsz, d // nres, nslab),
        in_specs=[cur(qb), cur(kb), prev(kb), cur(vb), prev(vb)],
        out_specs=[ospec(ATTN_W), ospec(LANES)],
        out_shape=[jax.ShapeDtypeStruct((bsz, d, ld, ATTN_W), BF16),
                   jax.ShapeDtypeStruct((bsz, d, ld, LANES), F32)],
        compiler_params=_params(3),
        name=f"attn_d{DILATIONS[group]}",
    )(qkv, qkv, qkv, qkv, qkv)


def _merge_kernel(x_ref, ys_ref, za_ref, ra_ref,
                  a0_ref, a1_ref, a2_ref, l0_ref, l1_ref, l2_ref,
                  spread_ref, wao_ref, wo_ref, o_ref, il_ref, stage_ref):
    tm = x_ref.shape[0]
    d1 = DILATIONS[1]

    def to_token_order(ref, base, stage_base=None):
        d, nslab = ref.shape[0], ref.shape[2] // LANES
        for r in range(d):
            blk = ref[r].astype(F32)
            for c in range(nslab):
                piece = blk[:, c * LANES:(c + 1) * LANES]
                if d == d1:
                    il_ref[base + c, pl.ds(r, tm // d, stride=d), :] = piece
                else:
                    r1, r2 = r % d1, r // d1
                    stage_ref[stage_base + c, pl.ds(r1 * (tm // d1) + r2, tm // d, stride=d1), :] = piece
        if d != d1:
            for c in range(nslab):
                for r1 in range(d1):
                    il_ref[base + c, pl.ds(r1, tm // d1, stride=d1), :] = (
                        stage_ref[stage_base + c, r1 * (tm // d1):(r1 + 1) * (tm // d1), :])
        return lambda rows: jnp.concatenate([il_ref[base + c, rows, :] for c in range(nslab)], axis=1)

    def per_head_to_lanes(w):
        hi = w.astype(BF16)
        lo = (w - hi.astype(F32)).astype(BF16)
        return jnp.dot(jnp.concatenate([hi, lo], axis=1), spread_ref[...], preferred_element_type=F32)

    wide_slabs = ATTN_W // LANES
    l1_at, l2_at = to_token_order(l1_ref, 0), to_token_order(l2_ref, 1, 0)
    a1_at, a2_at = to_token_order(a1_ref, 2), to_token_order(a2_ref, 2 + wide_slabs, 1)
    rows = slice(None)
    l0, l1, l2 = l0_ref[...], l1_at(rows), l2_at(rows)
    lm = jnp.maximum(jnp.maximum(l0, l1), l2)
    e0, e1, e2 = jnp.exp2(l0 - lm), jnp.exp2(l1 - lm), jnp.exp2(l2 - lm)
    inv = 1.0 / (e0 + e1 + e2)
    att = (per_head_to_lanes(e0 * inv) * a0_ref[...].astype(F32)
           + per_head_to_lanes(e1 * inv) * a1_at(rows)
           + per_head_to_lanes(e2 * inv) * a2_at(rows))
    ya_in = (att * za_ref[...].astype(F32)).astype(BF16)
    ya = jnp.dot(ya_in, wao_ref[...], preferred_element_type=F32)
    m = ys_ref[...].astype(F32) + ra_ref[...].astype(F32) * ya
    o_ref[...] = x_ref[...] + jnp.dot(m.astype(BF16), wo_ref[...], preferred_element_type=F32)


def _merge_call(x, ys, nat, attn, w_attn_out, w_o, tm=1024):
    bsz, length, _ = x.shape
    wide = lambda blk: pl.BlockSpec((None, tm, D_MODEL), lambda b, i: (b, i, blk))
    half = lambda blk: pl.BlockSpec((None, tm, ATTN_W), lambda b, i: (b, i, blk))
    res = lambda d, w: pl.BlockSpec((None, d, tm // d, w), lambda b, i: (b, 0, i, 0))
    (a0, l0), (a1, l1), (a2, l2) = attn
    d4, d16 = DILATIONS[1], DILATIONS[2]
    a0, l0 = a0.reshape(bsz, length, ATTN_W), l0.reshape(bsz, length, LANES)
    spread = (jnp.arange(LANES)[:, None] == (jnp.arange(ATTN_W)[None, :] // HEAD_DIM) * LSE_LANES).astype(BF16)
    spread = jnp.concatenate([spread, spread], axis=0)
    return pl.pallas_call(
        _merge_kernel,
        grid=(bsz, length // tm),
        in_specs=[wide(0), wide(0), half(NAT_ZA), wide(NAT_RA // 2),
                  half(0), res(d4, ATTN_W), res(d16, ATTN_W),
                  pl.BlockSpec((None, tm, LANES), lambda b, i: (b, i, 0)), res(d4, LANES), res(d16, LANES),
                  _full_spec((2 * LANES, ATTN_W)), _full_spec((ATTN_W, D_MODEL)), _full_spec((D_MODEL, D_MODEL))],
        out_specs=wide(0),
        out_shape=jax.ShapeDtypeStruct((bsz, length, D_MODEL), F32),
        scratch_shapes=[pltpu.VMEM((2 + 2 * (ATTN_W // LANES), tm, LANES), F32),
                        pltpu.VMEM((1 + ATTN_W // LANES, tm, LANES), F32)],
        compiler_params=_params(2),
        name="merge",
    )(x, ys, nat, nat, a0, a1, a2, l0, l1, l2, spread, w_attn_out.astype(BF16), w_o.astype(BF16))


def kernel(x, positions, norm_w, w_in, lam_re, lam_im, log_dt, b_re, b_im, c_re, c_im, d_skip, w_glu,
           q_norm_w, k_norm_w, w_ssm_out, w_attn_out, w_o):
    bsz, length, width = x.shape
    assert width == D_MODEL and w_in.shape[1:] == (D_MODEL, IN_WIDTH)
    assert bsz % 16 == 0, "the scan layout packs the batch into 16-row bf16 tiles"
    assert length % (128 * DILATIONS[-1]) == 0, "every residue class must hold whole 128-query blocks"
    xf = x.astype(F32)
    for layer in range(norm_w.shape[0]):
        st, mt, rt, coef = _ssm_prep_call(lam_re[layer], lam_im[layer], log_dt[layer], b_re[layer],
                                          b_im[layer], c_re[layer], c_im[layer], d_skip[layer])
        nat, qkv4, qkv16 = _proj_call(xf, positions, norm_w[layer], w_in[layer], q_norm_w[layer],
                                      k_norm_w[layer])
        u2 = _ssm_in_call(xf, norm_w[layer], w_in[layer][:, :D_MODEL])
        y2 = _ssm_call(u2, st, mt, rt, coef)
        ys = _ssm_out_call(y2, nat, w_glu[layer], w_ssm_out[layer])
        attn = [_attn_call(nat[:, None], 0, (NAT_Q, NAT_K, NAT_V)),
                _attn_call(qkv4, 1, (0, 1, 2)),
                _attn_call(qkv16, 2, (0, 1, 2))]
        xf = _merge_call(xf, ys, nat, attn, w_attn_out[layer], w_o[layer])
    return xf.astype(x.dtype)
```

```python
import math

import jax
import jax.numpy as jnp
from jax import lax
from jax.experimental import pallas as pl
from jax.experimental.pallas import tpu as pltpu

F32 = jnp.float32
BF16 = jnp.bfloat16

D_MODEL = 1024
SSM_GROUP = 16
SSM_GROUPS = D_MODEL // SSM_GROUP
SSM_STATE = 64
CHUNK = 16
CHUNK_W = CHUNK * SSM_GROUP
HEAD_DIM = 64
HEADS = 8
ATTN_W = HEADS * HEAD_DIM
DILATIONS = (1, 4, 16)
WINDOW_KEYS = 128
ROPE_DIM = HEAD_DIM // 4
ROPE_THETA = 500000.0
EPS = 1e-6
NEG_INF = -1e30
LOG2E = math.log2(math.e)
GELU_K1 = -2.0 * math.sqrt(2.0 / math.pi) * LOG2E
GELU_K3 = GELU_K1 * 0.044715
COL = 512
ZS_BLK, Q_BLK, K_BLK, V_BLK, ZA_BLK, RS_BLK, RA_BLK = 2, 4, 7, 10, 13, 14, 16
N_BLK = 18
IN_WIDTH = N_BLK * COL
NAT_ZS, NAT_Q, NAT_K, NAT_V, NAT_ZA, NAT_RS, NAT_RA = 0, 2, 3, 4, 5, 6, 8
NAT_W = 10 * COL
QKV_W = 3 * COL
LANES = 128
MXU_DIM = 256
LSE_LANES = LANES // HEADS
LANE_BLKS = LANES // SSM_GROUP
SCAN_GROUPS = 2
SSM_CB = 8
SSM_SLOTS = 2
SSM_PITCH = 8 * 3
VMEM_LIMIT = 56 * 1024 * 1024


def _params(grid_rank):
    return pltpu.CompilerParams(dimension_semantics=("arbitrary",) * grid_rank, vmem_limit_bytes=VMEM_LIMIT)


def _full_spec(shape):
    nd = len(shape)
    return pl.BlockSpec(shape, lambda *_: (0,) * nd, pipeline_mode=pl.Buffered(1))


def _ssm_prep_kernel(lam_ref, btr_ref, bti_ref, cr_ref, ci_ref, dsk_ref, st_ref, mt_ref, rtt_ref, coef_ref):
    ns2 = 2 * SSM_STATE
    lam = lam_ref[0]
    lr, li, dt = lam[0:1, :], lam[1:2, :], jnp.exp(lam[2:3, :])
    mag = jnp.exp(lr * dt)
    ar = mag * jnp.cos(li * dt)
    ai = mag * jnp.sin(li * dt)
    den = lr * lr + li * li
    nr = ar - 1.0
    fr = (nr * lr + ai * li) / den
    fi = (ai * lr - nr * li) / den
    pr, pi = [jnp.ones_like(ar)], [jnp.zeros_like(ar)]
    for _ in range(CHUNK):
        pr, pi = pr + [pr[-1] * ar - pi[-1] * ai], pi + [pr[-1] * ai + pi[-1] * ar]
    by_step = lambda vals: jnp.concatenate([jnp.broadcast_to(v, (SSM_GROUP, ns2)) for v in vals], axis=0)
    per_step = lambda a: jnp.concatenate([a] * CHUNK, axis=0)
    im_part = lax.broadcasted_iota(jnp.int32, (CHUNK_W, ns2), 1) >= SSM_STATE

    btr, bti = per_step(btr_ref[0]), per_step(bti_ref[0])
    bbr = fr * btr - fi * bti
    bbi = fr * bti + fi * btr
    apr = by_step([pr[CHUNK - 1 - s] for s in range(CHUNK)])
    api = by_step([pi[CHUNK - 1 - s] for s in range(CHUNK)])
    wre = apr * bbr - api * bbi
    wim = apr * bbi + api * bbr
    st = jnp.where(im_part, wim, wre)
    st_sw = jnp.where(im_part, wre, wim)
    st_ref[0] = jnp.concatenate([st, st_sw], axis=1).astype(BF16)

    cr, ci = per_step(cr_ref[0]), per_step(ci_ref[0])
    qr = by_step([pr[t + 1] for t in range(CHUNK)])
    qi = by_step([pi[t + 1] for t in range(CHUNK)])
    rtt_ref[0] = jnp.where(im_part, -(cr * qi + ci * qr), cr * qr - ci * qi).astype(BF16)

    cct = jnp.where(im_part, -ci, cr)
    split = lambda v: (v.astype(BF16), (v - v.astype(BF16).astype(F32)).astype(BF16))
    (st_hi, st_lo), (cc_hi, cc_lo) = split(st), split(cct)
    nt = (((1,), (1,)), ((), ()))
    krw = (lax.dot_general(jnp.concatenate([st_hi, st_lo], axis=1), jnp.concatenate([cc_hi, cc_hi], axis=1), nt,
                           preferred_element_type=F32)
           + lax.dot_general(st_hi, cc_lo, nt, preferred_element_type=F32))
    lane_blk = lax.broadcasted_iota(jnp.int32, (CHUNK_W, CHUNK_W), 1) // SSM_GROUP
    mt = jnp.zeros((CHUNK_W, CHUNK_W), F32)
    for t in range(CHUNK):
        sh = SSM_GROUP * (CHUNK - 1 - t)
        if sh == 0:
            shifted = krw
        else:
            shifted = jnp.concatenate([krw[sh:, :], jnp.zeros((sh, CHUNK_W), F32)], axis=0)
        mt = jnp.where(lane_blk == t, shifted, mt)
    on_diag = (lax.broadcasted_iota(jnp.int32, (CHUNK_W, CHUNK_W), 0)
               == lax.broadcasted_iota(jnp.int32, (CHUNK_W, CHUNK_W), 1))
    mt_ref[0] = (mt + jnp.where(on_diag, dsk_ref[0, 0:1, :], 0.0)).astype(BF16)

    im_row = lax.broadcasted_iota(jnp.int32, (1, ns2), 1) >= SSM_STATE
    bco = jnp.where(im_row, pi[CHUNK], -pi[CHUNK])
    coef_ref[0] = jnp.concatenate([pr[CHUNK], bco, jnp.zeros((6, ns2), F32)], axis=0)


def _ssm_prep_call(lam_re, lam_im, log_dt, b_re, b_im, c_re, c_im, d_skip):
    g, n, p = SSM_GROUPS, SSM_STATE, SSM_GROUP
    dsk = jnp.tile(d_skip.astype(F32).reshape(g, 1, p), (1, 1, CHUNK))
    dsk = jnp.concatenate([dsk, jnp.zeros((g, 7, CHUNK_W), F32)], axis=1)
    f = lambda a: a.astype(F32)
    twice = lambda a: jnp.tile(f(a), (1,) * (a.ndim - 1) + (2,))
    ldt_b = jnp.broadcast_to(f(log_dt)[:, None], (g, n))
    lam = jnp.concatenate([twice(lam_re)[:, None], twice(lam_im)[:, None], twice(ldt_b)[:, None],
                           jnp.zeros((g, 5, 2 * n), F32)], axis=1)
    bt = lambda b: twice(b.transpose(0, 2, 1))
    gspec = lambda shape: pl.BlockSpec((1,) + shape, lambda i: (i, 0, 0))
    return pl.pallas_call(
        _ssm_prep_kernel,
        grid=(g,),
        in_specs=[gspec((8, 2 * n))] + [gspec((p, 2 * n))] * 4 + [gspec((8, CHUNK_W))],
        out_specs=[gspec((CHUNK_W, 4 * n)), gspec((CHUNK_W, CHUNK_W)), gspec((CHUNK_W, 2 * n)),
                   gspec((8, 2 * n))],
        out_shape=[jax.ShapeDtypeStruct((g, CHUNK_W, 4 * n), BF16),
                   jax.ShapeDtypeStruct((g, CHUNK_W, CHUNK_W), BF16),
                   jax.ShapeDtypeStruct((g, CHUNK_W, 2 * n), BF16),
                   jax.ShapeDtypeStruct((g, 8, 2 * n), F32)],
        compiler_params=_params(1),
        name="ssm_prep",
    )(lam, bt(b_re), bt(b_im), twice(c_re), twice(c_im), dsk)


def _rope_tables():
    lane = jnp.arange(LANES) % HEAD_DIM
    half = ROPE_DIM // 2
    inv = ROPE_THETA ** (-jnp.arange(0, ROPE_DIM, 2, dtype=F32) / ROPE_DIM)
    expand = ((lane[None, :] < ROPE_DIM) & (lane[None, :] % half == jnp.arange(half)[:, None])).astype(F32)
    s_lo = jnp.where(lane < half, -1.0, 0.0)
    s_hi = jnp.where((lane >= half) & (lane < ROPE_DIM), 1.0, 0.0)
    unrot = jnp.where(lane >= ROPE_DIM, 1.0, 0.0)
    rows = jnp.concatenate([s_lo[None], s_hi[None], unrot[None], jnp.zeros((5, LANES), F32)], axis=0)
    expand = jnp.concatenate([expand, expand], axis=0).astype(BF16)
    return inv[:, None].astype(F32), expand, rows.astype(F32)


def _sigmoid(v):
    return 1.0 / (1.0 + jnp.exp(-v))


_NAT_DST = {2: NAT_ZS, 3: NAT_ZS + 1, Q_BLK: NAT_Q, K_BLK: NAT_K, V_BLK: NAT_V,
            ZA_BLK: NAT_ZA, 14: NAT_RS, 15: NAT_RS + 1, 16: NAT_RA, 17: NAT_RA + 1}


_PROJ_ORDER = (2, 4, 3, 7, 13, 5, 14, 8, 15, 6, 16, 9, 17, 11, 12, 10)
_N_PERM = 6


def _proj_kernel(x_ref, pos_ref, nw_ref, w_ref, qkw_ref, ones_ref, freq_ref, expand_ref, rope_ref,
                 nat_ref, d4_ref, d16_ref, perm_ref, stage_ref, h_ref):
    tm = x_ref.shape[0]
    x = x_ref[...]
    ms = jnp.mean(x * x, axis=-1, keepdims=True)
    h_ref[...] = (x * lax.rsqrt(ms + EPS) * nw_ref[...]).astype(BF16)
    ang = freq_ref[...] * pos_ref[...].astype(F32)

    def spread(v):
        hi = v.astype(BF16)
        lo = (v - hi.astype(F32)).astype(BF16)
        return lax.dot_general(jnp.concatenate([hi, lo], axis=0), expand_ref[...], (((0,), (0,)), ((), ())),
                               preferred_element_type=F32)

    cosv = spread(jnp.cos(ang)) + rope_ref[2:3, :]
    sinv = spread(jnp.sin(ang))
    s_lo = sinv * rope_ref[0:1, :]
    s_hi = sinv * rope_ref[1:2, :]
    slot = 0
    for j in _PROJ_ORDER:
        acc = jnp.dot(h_ref[...], w_ref[:, j * COL:(j + 1) * COL], preferred_element_type=F32)
        if ZS_BLK <= j < Q_BLK or j == ZA_BLK:
            res = acc * _sigmoid(acc)
        elif j >= RS_BLK:
            res = _sigmoid(acc)
        elif Q_BLK <= j < V_BLK:
            sq = (acc * acc).astype(BF16)
            hw = ones_ref.shape[0]
            ss = jnp.concatenate([jnp.dot(sq[:, c * hw:(c + 1) * hw], ones_ref[...], preferred_element_type=F32)
                                  for c in range(COL // hw)], axis=1)
            y = acc * lax.rsqrt(ss * (1.0 / HEAD_DIM) + EPS) * qkw_ref[j - Q_BLK:j - Q_BLK + 1, :]
            parts = []
            for c in range(COL // LANES):
                yc = y[:, c * LANES:(c + 1) * LANES]
                parts.append(yc * cosv + pltpu.roll(yc, LANES - ROPE_DIM // 2, 1) * s_lo
                             + pltpu.roll(yc, ROPE_DIM // 2, 1) * s_hi)
            res = jnp.concatenate(parts, axis=1)
        else:
            res = acc
        if j in _NAT_DST:
            dst = _NAT_DST[j]
            nat_ref[:, dst * COL:(dst + 1) * COL] = res.astype(BF16)
        else:
            kind, group = divmod(j - Q_BLK, len(DILATIONS))
            d = DILATIONS[group]
            out_ref = d4_ref if group == 1 else d16_ref
            nslab = COL // LANES
            for c in range(nslab):
                perm_ref[slot, c] = res[:, c * LANES:(c + 1) * LANES]
            if d == DILATIONS[2]:
                d1 = DILATIONS[1]
                for c in range(nslab):
                    for r1 in range(d1):
                        stage_ref[kind, c, r1 * (tm // d1):(r1 + 1) * (tm // d1), :] = (
                            perm_ref[slot, c, pl.ds(r1, tm // d1, stride=d1), :])
                for r1, r2 in [(r1, r2) for r1 in range(d1) for r2 in range(d // d1)]:
                    rows = [stage_ref[kind, c, pl.ds(r1 * (tm // d1) + r2, tm // d, stride=d1), :]
                            for c in range(nslab)]
                    out_ref[r1 + d1 * r2, :, kind * COL:(kind + 1) * COL] = (
                        jnp.concatenate(rows, axis=1).astype(BF16))
            else:
                for r in range(d):
                    rows = [perm_ref[slot, c, pl.ds(r, tm // d, stride=d), :] for c in range(nslab)]
                    out_ref[r, :, kind * COL:(kind + 1) * COL] = jnp.concatenate(rows, axis=1).astype(BF16)
            slot += 1


def _proj_call(x, positions, norm_w, w_in, q_norm_w, k_norm_w, tm=512):
    bsz, length, _ = x.shape
    scale = LOG2E / math.sqrt(HEAD_DIM)
    qkw = jnp.concatenate([jnp.tile(q_norm_w.astype(F32) * scale, (1, HEADS)),
                           jnp.tile(k_norm_w.astype(F32), (1, HEADS)),
                           jnp.zeros((2, ATTN_W), F32)], axis=0)
    hid = jnp.arange(MXU_DIM) // HEAD_DIM
    ones = (hid[:, None] == hid[None, :]).astype(BF16)
    d4, d16 = DILATIONS[1], DILATIONS[2]
    freq, expand, rope_rows = _rope_tables()
    return pl.pallas_call(
        _proj_kernel,
        grid=(bsz, length // tm),
        in_specs=[pl.BlockSpec((None, tm, D_MODEL), lambda b, i: (b, i, 0)),
                  pl.BlockSpec((None, 1, tm), lambda b, i: (b, 0, i)),
                  _full_spec((1, D_MODEL)),
                  _full_spec((D_MODEL, IN_WIDTH)),
                  _full_spec((8, ATTN_W)),
                  _full_spec((MXU_DIM, MXU_DIM)),
                  _full_spec((ROPE_DIM // 2, 1)),
                  _full_spec((ROPE_DIM, LANES)),
                  _full_spec((8, LANES))],
        out_specs=[pl.BlockSpec((None, tm, NAT_W), lambda b, i: (b, i, 0)),
                   pl.BlockSpec((None, d4, tm // d4, QKV_W), lambda b, i: (b, 0, i, 0)),
                   pl.BlockSpec((None, d16, tm // d16, QKV_W), lambda b, i: (b, 0, i, 0))],
        out_shape=[jax.ShapeDtypeStruct((bsz, length, NAT_W), BF16),
                   jax.ShapeDtypeStruct((bsz, d4, length // d4, QKV_W), BF16),
                   jax.ShapeDtypeStruct((bsz, d16, length // d16, QKV_W), BF16)],
        scratch_shapes=[pltpu.VMEM((_N_PERM, COL // LANES, tm, LANES), F32),
                        pltpu.VMEM((3, COL // LANES, tm, LANES), F32),
                        pltpu.VMEM((tm, D_MODEL), BF16)],
        compiler_params=_params(2),
        name="proj",
    )(x.astype(F32), positions.astype(jnp.int32)[:, None, :], norm_w.astype(F32)[None, :],
      w_in.astype(BF16), qkw, ones, freq, expand, rope_rows)


def _block_transpose(xs):
    xs = list(xs)
    blk = lax.broadcasted_iota(jnp.int32, xs[0].shape, 1) // SSM_GROUP
    dist = 1
    while dist < LANE_BLKS:
        upper = (blk & dist) != 0
        shift = dist * SSM_GROUP
        for i in range(LANE_BLKS):
            if i & dist:
                continue
            a, b = xs[i], xs[i + dist]
            xs[i] = jnp.where(upper, pltpu.roll(b, shift, 1), a)
            xs[i + dist] = jnp.where(upper, b, pltpu.roll(a, LANES - shift, 1))
        dist *= 2
    return xs


def _ssm_in_kernel(x_ref, nw_ref, w_ref, u2_ref, rows_ref):
    bsz, tl = x_ref.shape[0], x_ref.shape[1]
    for c in range(tl // CHUNK):
        x = x_ref[:, c * CHUNK:(c + 1) * CHUNK, :].reshape(bsz * CHUNK, D_MODEL)
        ms = jnp.mean(x * x, axis=-1, keepdims=True)
        h = (x * lax.rsqrt(ms + EPS) * nw_ref[...]).astype(BF16)
        u = jnp.dot(h, w_ref[...], preferred_element_type=F32)
        for b in range(bsz):
            for s8 in range(D_MODEL // LANES):
                rows_ref[c % SSM_SLOTS, s8, b * SSM_PITCH:b * SSM_PITCH + CHUNK, :] = (
                    u[b * CHUNK:(b + 1) * CHUNK, s8 * LANES:(s8 + 1) * LANES])
        for s8 in range(D_MODEL // LANES):
            for half in range(CHUNK // LANE_BLKS):
                xs = []
                for k in range(LANE_BLKS):
                    v = rows_ref[c % SSM_SLOTS, s8, pl.ds(half * LANE_BLKS + k, bsz, stride=SSM_PITCH), :].astype(BF16)
                    xs.append(pltpu.bitcast(v, jnp.uint32))
                ys = _block_transpose(xs)
                for g in range(LANE_BLKS):
                    u2_ref[s8 * LANE_BLKS + g, c, :, half * LANES:(half + 1) * LANES] = pltpu.bitcast(ys[g], BF16)


def _ssm_in_call(x, norm_w, w_u):
    bsz, length, _ = x.shape
    tl = SSM_CB * CHUNK
    nc = length // CHUNK
    return pl.pallas_call(
        _ssm_in_kernel,
        grid=(length // tl,),
        in_specs=[pl.BlockSpec((bsz, tl, D_MODEL), lambda i: (0, i, 0)),
                  _full_spec((1, D_MODEL)), _full_spec((D_MODEL, D_MODEL))],
        out_specs=pl.BlockSpec((SSM_GROUPS, SSM_CB, bsz, CHUNK_W), lambda i: (0, i, 0, 0)),
        out_shape=jax.ShapeDtypeStruct((SSM_GROUPS, nc, bsz, CHUNK_W), BF16),
        scratch_shapes=[pltpu.VMEM((SSM_SLOTS, D_MODEL // LANES, bsz * SSM_PITCH, LANES), F32)],
        compiler_params=_params(1),
        name="ssm_in",
    )(x, norm_w.astype(F32)[None, :], w_u.astype(BF16))


def _ssm_out_kernel(y2_ref, zs_ref, rs_ref, wglu_ref, wso_ref, o_ref, rows_ref):
    bsz, tl = zs_ref.shape[0], zs_ref.shape[1]
    for c in range(tl // CHUNK):
        for s8 in range(D_MODEL // LANES):
            for half in range(CHUNK // LANE_BLKS):
                xs = [pltpu.bitcast(y2_ref[s8 * LANE_BLKS + g, c, :, half * LANES:(half + 1) * LANES], jnp.uint32)
                      for g in range(LANE_BLKS)]
                ys = _block_transpose(xs)
                for k in range(LANE_BLKS):
                    rows_ref[c % SSM_SLOTS, s8, pl.ds(half * LANE_BLKS + k, bsz, stride=SSM_PITCH), :] = (
                        pltpu.bitcast(ys[k], BF16).astype(F32))
        y = jnp.concatenate(
            [jnp.concatenate([rows_ref[c % SSM_SLOTS, s8, b * SSM_PITCH:b * SSM_PITCH + CHUNK, :]
                              for s8 in range(D_MODEL // LANES)], axis=1)
             for b in range(bsz)], axis=0)
        tok = slice(c * CHUNK, (c + 1) * CHUNK)
        zs = zs_ref[:, tok, :].reshape(bsz * CHUNK, D_MODEL).astype(F32)
        g = y / (1.0 + jnp.exp2(y * (GELU_K1 + GELU_K3 * (y * y))))
        gate = _sigmoid(jnp.dot(g.astype(BF16), wglu_ref[...], preferred_element_type=F32))
        ys_in = (g * gate * zs).astype(BF16)
        ys = jnp.dot(ys_in, wso_ref[...], preferred_element_type=F32)
        gated = ys * rs_ref[:, tok, :].reshape(bsz * CHUNK, D_MODEL).astype(F32)
        o_ref[:, tok, :] = gated.reshape(bsz, CHUNK, D_MODEL).astype(BF16)


def _ssm_out_call(y2, nat, w_glu, w_ssm_out):
    bsz, length, _ = nat.shape
    tl = SSM_CB * CHUNK
    tok = lambda blk: pl.BlockSpec((bsz, tl, D_MODEL), lambda i: (0, i, blk))
    return pl.pallas_call(
        _ssm_out_kernel,
        grid=(length // tl,),
        in_specs=[pl.BlockSpec((SSM_GROUPS, SSM_CB, bsz, CHUNK_W), lambda i: (0, i, 0, 0)),
                  tok(NAT_ZS // 2), tok(NAT_RS // 2),
                  _full_spec((D_MODEL, D_MODEL)), _full_spec((D_MODEL, D_MODEL))],
        out_specs=tok(0),
        out_shape=jax.ShapeDtypeStruct((bsz, length, D_MODEL), BF16),
        scratch_shapes=[pltpu.VMEM((SSM_SLOTS, D_MODEL // LANES, bsz * SSM_PITCH, LANES), F32)],
        compiler_params=_params(1),
        name="ssm_out",
    )(y2, nat, nat, w_glu.astype(BF16), w_ssm_out.astype(BF16))


def _ssm_kernel(u_ref, st_ref, mt_ref, rtt_ref, coef_ref, y_ref, hloc_ref, hprev_ref):
    ng, nc, bsz = u_ref.shape[0], u_ref.shape[1], u_ref.shape[2]
    ns2 = 2 * SSM_STATE
    us, coefs = [], []
    for g in range(ng):
        u = u_ref[g].reshape(nc * bsz, CHUNK_W)
        us.append(u)
        half = nc * bsz // 2
        hloc_ref[g, :half, :] = jnp.dot(u[:half], st_ref[g], preferred_element_type=F32)
        hloc_ref[g, half:, :] = jnp.dot(u[half:], st_ref[g], preferred_element_type=F32)
        coefs.append((jnp.broadcast_to(coef_ref[g, 0:1, :], (bsz, ns2)),
                      jnp.broadcast_to(coef_ref[g, 1:2, :], (bsz, ns2))))

    def step(c, carry):
        r = pl.multiple_of(c * bsz, bsz)
        out = []
        for g in range(ng):
            hx, hy = carry[2 * g], carry[2 * g + 1]
            a, b = coefs[g]
            hprev_ref[g, pl.ds(r, bsz), :] = hx.astype(BF16)
            px = hloc_ref[g, pl.ds(r, bsz), 0:ns2]
            py = hloc_ref[g, pl.ds(r, bsz), ns2:2 * ns2]
            out += [a * hx + b * hy + px, a * hy - b * hx + py]
        return tuple(out)

    zero = jnp.zeros((bsz, ns2), F32)
    lax.fori_loop(0, nc, step, (zero,) * (2 * ng), unroll=8)
    for g in range(ng):
        y = jnp.dot(us[g], mt_ref[g], preferred_element_type=F32)
        y = y + lax.dot_general(hprev_ref[g], rtt_ref[g], (((1,), (1,)), ((), ())), preferred_element_type=F32)
        y_ref[g] = y.reshape(nc, bsz, CHUNK_W).astype(BF16)


def _ssm_call(u2, st, mt, rt, coef):
    g, nc, bsz, _ = u2.shape
    ng = SCAN_GROUPS
    gspec = lambda shape: pl.BlockSpec((ng,) + shape, lambda i: (i,) + (0,) * len(shape))
    return pl.pallas_call(
        _ssm_kernel,
        grid=(g // ng,),
        in_specs=[gspec((nc, bsz, CHUNK_W)), gspec((CHUNK_W, 4 * SSM_STATE)), gspec((CHUNK_W, CHUNK_W)),
                  gspec((CHUNK_W, 2 * SSM_STATE)), gspec((8, 2 * SSM_STATE))],
        out_specs=gspec((nc, bsz, CHUNK_W)),
        out_shape=jax.ShapeDtypeStruct(u2.shape, BF16),
        scratch_shapes=[pltpu.VMEM((ng, nc * bsz, 4 * SSM_STATE), F32),
                        pltpu.VMEM((ng, nc * bsz, 2 * SSM_STATE), BF16)],
        compiler_params=_params(1),
        name="ssm_scan",
    )(u2, st, mt, rt, coef)


def _attn_kernel(q_ref, kc_ref, kp_ref, vc_ref, vp_ref, o_ref, lse_ref):
    nres, tq = q_ref.shape[0], q_ref.shape[1]
    qb = 128
    slab = pl.program_id(2)
    qi = lax.broadcasted_iota(jnp.int32, (qb, 2 * qb), 0)
    kk = lax.broadcasted_iota(jnp.int32, (qb, 2 * qb), 1)
    band = jnp.where((kk >= qi) & (kk <= qi + WINDOW_KEYS), 0.0, NEG_INF).astype(F32)
    band0 = band + jnp.where(jnp.logical_and(slab == 0, kk < qb), NEG_INF, 0.0).astype(F32)
    band = jnp.concatenate([band, band], axis=0)
    band0 = jnp.concatenate([band0, band0], axis=0)
    lo_q = lax.broadcasted_iota(jnp.int32, (qb, LANES), 1) < HEAD_DIM
    ones_kv = jnp.ones((2 * qb, LANES), BF16)

    head_slot = lax.broadcasted_iota(jnp.int32, (qb, LANES), 1) // LSE_LANES

    for r, j in [(r, j) for r in range(nres) for j in range(tq // qb)]:
        madd = band0 if j == 0 else band
        rows = slice(j * qb, (j + 1) * qb)
        m_all = jnp.zeros((qb, LANES), F32)
        den_all = jnp.ones((qb, LANES), F32)
        for hp in range(ATTN_W // LANES):
            cs = slice(hp * LANES, (hp + 1) * LANES)
            qp = q_ref[r, rows, cs]
            if j == 0:
                kp = jnp.concatenate([kp_ref[r, :, cs], kc_ref[r, 0:qb, cs]], axis=0)
                vp = jnp.concatenate([vp_ref[r, :, cs], vc_ref[r, 0:qb, cs]], axis=0)
            else:
                kp = kc_ref[r, (j - 1) * qb:(j + 1) * qb, cs]
                vp = vc_ref[r, (j - 1) * qb:(j + 1) * qb, cs]
            zero = jnp.zeros_like(qp)
            q2 = jnp.concatenate([jnp.where(lo_q, qp, zero), jnp.where(lo_q, zero, qp)], axis=0)
            s = lax.dot_general(q2, kp, (((1,), (1,)), ((), ())), preferred_element_type=F32)
            s = s + madd
            m = jnp.max(s, axis=1, keepdims=True)
            p = jnp.exp2(s - m)
            pv = jnp.dot(p.astype(BF16), jnp.concatenate([vp, ones_kv], axis=1), preferred_element_type=F32)
            num = jnp.where(lo_q, pv[:qb, :LANES], pv[qb:, :LANES])
            den = jnp.where(lo_q, pv[:qb, LANES:], pv[qb:, LANES:])
            o_ref[r, rows, cs] = (num / den).astype(BF16)
            for side in range(2):
                slot = head_slot == 2 * hp + side
                half = slice(side * qb, (side + 1) * qb)
                m_all = jnp.where(slot, m[half], m_all)
                den_all = jnp.where(slot, pv[half, LANES:], den_all)
        lse_ref[r, rows, :] = m_all + jnp.log2(den_all)


def _attn_call(qkv, group, blocks, rows_per_step=4096):
    bsz, d, ld, _ = qkv.shape
    tq = min(rows_per_step, ld)
    nres = min(d, rows_per_step // tq)
    assert ld % tq == 0 and d % nres == 0 and tq % 128 == 0
    nslab = ld // tq
    per = tq // 128
    cur = lambda blk: pl.BlockSpec((None, nres, tq, COL), lambda b, r, i: (b, r, i, blk))
    prev = lambda blk: pl.BlockSpec(
        (None, nres, 128, COL), lambda b, r, i: (b, r, jnp.maximum(i * per - 1, 0), blk))
    ospec = lambda w: pl.BlockSpec((None, nres, tq, w), lambda b, r, i: (b, r, i, 0))
    qb, kb, vb = blocks
    return pl.pallas_call(
        _attn_kernel,
        grid=(bsz, d // nres, nslab),
        in_specs=[cur(qb), cur(kb), prev(kb), cur(vb), prev(vb)],
        out_specs=[ospec(ATTN_W), ospec(LANES)],
        out_shape=[jax.ShapeDtypeStruct((bsz, d, ld, ATTN_W), BF16),
                   jax.ShapeDtypeStruct((bsz, d, ld, LANES), F32)],
        compiler_params=_params(3),
        name=f"attn_d{DILATIONS[group]}",
    )(qkv, qkv, qkv, qkv, qkv)


def _merge_kernel(x_ref, ys_ref, za_ref, ra_ref,
                  a0_ref, a1_ref, a2_ref, l0_ref, l1_ref, l2_ref,
                  spread_ref, wao_ref, wo_ref, o_ref, il_ref, stage_ref):
    tm = x_ref.shape[0]
    d1 = DILATIONS[1]

    def to_token_order(ref, base, stage_base=None):
        d, nslab = ref.shape[0], ref.shape[2] // LANES
        for r in range(d):
            blk = ref[r].astype(F32)
            for c in range(nslab):
                piece = blk[:, c * LANES:(c + 1) * LANES]
                if d == d1:
                    il_ref[base + c, pl.ds(r, tm // d, stride=d), :] = piece
                else:
                    r1, r2 = r % d1, r // d1
                    stage_ref[stage_base + c, pl.ds(r1 * (tm // d1) + r2, tm // d, stride=d1), :] = piece
        if d != d1:
            for c in range(nslab):
                for r1 in range(d1):
                    il_ref[base + c, pl.ds(r1, tm // d1, stride=d1), :] = (
                        stage_ref[stage_base + c, r1 * (tm // d1):(r1 + 1) * (tm // d1), :])
        return jnp.concatenate([il_ref[base + c] for c in range(nslab)], axis=1)

    def per_head_to_lanes(w):
        hi = w.astype(BF16)
        lo = (w - hi.astype(F32)).astype(BF16)
        return jnp.dot(jnp.concatenate([hi, lo], axis=1), spread_ref[...], preferred_element_type=F32)

    wide_slabs = ATTN_W // LANES
    l1, l2 = to_token_order(l1_ref, 0), to_token_order(l2_ref, 1, 0)
    a1, a2 = to_token_order(a1_ref, 2), to_token_order(a2_ref, 2 + wide_slabs, 1)
    l0 = l0_ref[...]
    lm = jnp.maximum(jnp.maximum(l0, l1), l2)
    e0, e1, e2 = jnp.exp2(l0 - lm), jnp.exp2(l1 - lm), jnp.exp2(l2 - lm)
    inv = 1.0 / (e0 + e1 + e2)
    att = (per_head_to_lanes(e0 * inv) * a0_ref[...].astype(F32)
           + per_head_to_lanes(e1 * inv) * a1
           + per_head_to_lanes(e2 * inv) * a2)
    ya_in = (att * za_ref[...].astype(F32)).astype(BF16)
    ya = jnp.dot(ya_in, wao_ref[...], preferred_element_type=F32)
    m = ys_ref[...].astype(F32) + ra_ref[...].astype(F32) * ya
    o_ref[...] = x_ref[...] + jnp.dot(m.astype(BF16), wo_ref[...], preferred_element_type=F32)


def _merge_call(x, ys, nat, attn, w_attn_out, w_o, tm=1024):
    bsz, length, _ = x.shape
    wide = lambda blk: pl.BlockSpec((None, tm, D_MODEL), lambda b, i: (b, i, blk))
    half = lambda blk: pl.BlockSpec((None, tm, ATTN_W), lambda b, i: (b, i, blk))
    res = lambda d, w: pl.BlockSpec((None, d, tm // d, w), lambda b, i: (b, 0, i, 0))
    (a0, l0), (a1, l1), (a2, l2) = attn
    d4, d16 = DILATIONS[1], DILATIONS[2]
    a0, l0 = a0.reshape(bsz, length, ATTN_W), l0.reshape(bsz, length, LANES)
    spread = (jnp.arange(LANES)[:, None] == (jnp.arange(ATTN_W)[None, :] // HEAD_DIM) * LSE_LANES).astype(BF16)
    spread = jnp.concatenate([spread, spread], axis=0)
    return pl.pallas_call(
        _merge_kernel,
        grid=(bsz, length // tm),
        in_specs=[wide(0), wide(0), half(NAT_ZA), wide(NAT_RA // 2),
                  half(0), res(d4, ATTN_W), res(d16, ATTN_W),
                  pl.BlockSpec((None, tm, LANES), lambda b, i: (b, i, 0)), res(d4, LANES), res(d16, LANES),
                  _full_spec((2 * LANES, ATTN_W)), _full_spec((ATTN_W, D_MODEL)), _full_spec((D_MODEL, D_MODEL))],
        out_specs=wide(0),
        out_shape=jax.ShapeDtypeStruct((bsz, length, D_MODEL), F32),
        scratch_shapes=[pltpu.VMEM((2 + 2 * (ATTN_W // LANES), tm, LANES), F32),
                        pltpu.VMEM((1 + ATTN_W // LANES, tm, LANES), F32)],
        compiler_params=_params(2),
        name="merge",
    )(x, ys, nat, nat, a0, a1, a2, l0, l1, l2, spread, w_attn_out.astype(BF16), w_o.astype(BF16))


def kernel(x, positions, norm_w, w_in, lam_re, lam_im, log_dt, b_re, b_im, c_re, c_im, d_skip, w_glu,
           q_norm_w, k_norm_w, w_ssm_out, w_attn_out, w_o):
    bsz, length, width = x.shape
    assert width == D_MODEL and w_in.shape[1:] == (D_MODEL, IN_WIDTH)
    assert bsz % 16 == 0, "the scan layout packs the batch into 16-row bf16 tiles"
    assert length % (128 * DILATIONS[-1]) == 0, "every residue class must hold whole 128-query blocks"
    xf = x.astype(F32)
    for layer in range(norm_w.shape[0]):
        st, mt, rt, coef = _ssm_prep_call(lam_re[layer], lam_im[layer], log_dt[layer], b_re[layer],
                                          b_im[layer], c_re[layer], c_im[layer], d_skip[layer])
        nat, qkv4, qkv16 = _proj_call(xf, positions, norm_w[layer], w_in[layer], q_norm_w[layer],
                                      k_norm_w[layer])
        u2 = _ssm_in_call(xf, norm_w[layer], w_in[layer][:, :D_MODEL])
        y2 = _ssm_call(u2, st, mt, rt, coef)
        ys = _ssm_out_call(y2, nat, w_glu[layer], w_ssm_out[layer])
        attn = [_attn_call(nat[:, None], 0, (NAT_Q, NAT_K, NAT_V)),
                _attn_call(qkv4, 1, (0, 1, 2)),
                _attn_call(qkv16, 2, (0, 1, 2))]
        xf = _merge_call(xf, ys, nat, attn, w_attn_out[layer], w_o[layer])
    return xf.astype(x.dtype)
```

```python
import math

import jax
import jax.numpy as jnp
from jax import lax
from jax.experimental import pallas as pl
from jax.experimental.pallas import tpu as pltpu

F32 = jnp.float32
BF16 = jnp.bfloat16

D_MODEL = 1024
SSM_GROUP = 16
SSM_GROUPS = D_MODEL // SSM_GROUP
SSM_STATE = 64
CHUNK = 16
CHUNK_W = CHUNK * SSM_GROUP
HEAD_DIM = 64
HEADS = 8
ATTN_W = HEADS * HEAD_DIM
DILATIONS = (1, 4, 16)
WINDOW_KEYS = 128
ROPE_DIM = HEAD_DIM // 4
ROPE_THETA = 500000.0
EPS = 1e-6
NEG_INF = -1e30
LOG2E = math.log2(math.e)
GELU_K1 = -2.0 * math.sqrt(2.0 / math.pi) * LOG2E
GELU_K3 = GELU_K1 * 0.044715
COL = 512
ZS_BLK, Q_BLK, K_BLK, V_BLK, ZA_BLK, RS_BLK, RA_BLK = 2, 4, 7, 10, 13, 14, 16
N_BLK = 18
IN_WIDTH = N_BLK * COL
NAT_ZS, NAT_Q, NAT_K, NAT_V, NAT_ZA, NAT_RS, NAT_RA = 0, 2, 3, 4, 5, 6, 8
NAT_W = 10 * COL
QKV_W = 3 * COL
LANES = 128
MXU_DIM = 256
LSE_LANES = LANES // HEADS
LANE_BLKS = LANES // SSM_GROUP
SCAN_GROUPS = 2
SSM_CB = 8
SSM_SLOTS = 2
SSM_PITCH = 4 * 5
VMEM_LIMIT = 56 * 1024 * 1024


def _params(grid_rank):
    return pltpu.CompilerParams(dimension_semantics=("arbitrary",) * grid_rank, vmem_limit_bytes=VMEM_LIMIT)


def _full_spec(shape):
    nd = len(shape)
    return pl.BlockSpec(shape, lambda *_: (0,) * nd, pipeline_mode=pl.Buffered(1))


def _ssm_prep_kernel(lam_ref, btr_ref, bti_ref, cr_ref, ci_ref, dsk_ref, st_ref, mt_ref, rtt_ref, coef_ref):
    ns2 = 2 * SSM_STATE
    lam = lam_ref[0]
    lr, li, dt = lam[0:1, :], lam[1:2, :], jnp.exp(lam[2:3, :])
    mag = jnp.exp(lr * dt)
    ar = mag * jnp.cos(li * dt)
    ai = mag * jnp.sin(li * dt)
    den = lr * lr + li * li
    nr = ar - 1.0
    fr = (nr * lr + ai * li) / den
    fi = (ai * lr - nr * li) / den
    pr, pi = [jnp.ones_like(ar)], [jnp.zeros_like(ar)]
    for _ in range(CHUNK):
        pr, pi = pr + [pr[-1] * ar - pi[-1] * ai], pi + [pr[-1] * ai + pi[-1] * ar]
    by_step = lambda vals: jnp.concatenate([jnp.broadcast_to(v, (SSM_GROUP, ns2)) for v in vals], axis=0)
    per_step = lambda a: jnp.concatenate([a] * CHUNK, axis=0)
    im_part = lax.broadcasted_iota(jnp.int32, (CHUNK_W, ns2), 1) >= SSM_STATE

    btr, bti = per_step(btr_ref[0]), per_step(bti_ref[0])
    bbr = fr * btr - fi * bti
    bbi = fr * bti + fi * btr
    apr = by_step([pr[CHUNK - 1 - s] for s in range(CHUNK)])
    api = by_step([pi[CHUNK - 1 - s] for s in range(CHUNK)])
    wre = apr * bbr - api * bbi
    wim = apr * bbi + api * bbr
    st = jnp.where(im_part, wim, wre)
    st_sw = jnp.where(im_part, wre, wim)
    st_ref[0] = jnp.concatenate([st, st_sw], axis=1).astype(BF16)

    cr, ci = per_step(cr_ref[0]), per_step(ci_ref[0])
    qr = by_step([pr[t + 1] for t in range(CHUNK)])
    qi = by_step([pi[t + 1] for t in range(CHUNK)])
    rtt_ref[0] = jnp.where(im_part, -(cr * qi + ci * qr), cr * qr - ci * qi).astype(BF16)

    cct = jnp.where(im_part, -ci, cr)
    split = lambda v: (v.astype(BF16), (v - v.astype(BF16).astype(F32)).astype(BF16))
    (st_hi, st_lo), (cc_hi, cc_lo) = split(st), split(cct)
    nt = (((1,), (1,)), ((), ()))
    krw = (lax.dot_general(jnp.concatenate([st_hi, st_lo], axis=1), jnp.concatenate([cc_hi, cc_hi], axis=1), nt,
                           preferred_element_type=F32)
           + lax.dot_general(st_hi, cc_lo, nt, preferred_element_type=F32))
    lane_blk = lax.broadcasted_iota(jnp.int32, (CHUNK_W, CHUNK_W), 1) // SSM_GROUP
    mt = jnp.zeros((CHUNK_W, CHUNK_W), F32)
    for t in range(CHUNK):
        sh = SSM_GROUP * (CHUNK - 1 - t)
        if sh == 0:
            shifted = krw
        else:
            shifted = jnp.concatenate([krw[sh:, :], jnp.zeros((sh, CHUNK_W), F32)], axis=0)
        mt = jnp.where(lane_blk == t, shifted, mt)
    on_diag = (lax.broadcasted_iota(jnp.int32, (CHUNK_W, CHUNK_W), 0)
               == lax.broadcasted_iota(jnp.int32, (CHUNK_W, CHUNK_W), 1))
    mt_ref[0] = (mt + jnp.where(on_diag, dsk_ref[0, 0:1, :], 0.0)).astype(BF16)

    im_row = lax.broadcasted_iota(jnp.int32, (1, ns2), 1) >= SSM_STATE
    bco = jnp.where(im_row, pi[CHUNK], -pi[CHUNK])
    coef_ref[0] = jnp.concatenate([pr[CHUNK], bco, jnp.zeros((6, ns2), F32)], axis=0)


def _ssm_prep_call(lam_re, lam_im, log_dt, b_re, b_im, c_re, c_im, d_skip):
    g, n, p = SSM_GROUPS, SSM_STATE, SSM_GROUP
    dsk = jnp.tile(d_skip.astype(F32).reshape(g, 1, p), (1, 1, CHUNK))
    dsk = jnp.concatenate([dsk, jnp.zeros((g, 7, CHUNK_W), F32)], axis=1)
    f = lambda a: a.astype(F32)
    twice = lambda a: jnp.tile(f(a), (1,) * (a.ndim - 1) + (2,))
    ldt_b = jnp.broadcast_to(f(log_dt)[:, None], (g, n))
    lam = jnp.concatenate([twice(lam_re)[:, None], twice(lam_im)[:, None], twice(ldt_b)[:, None],
                           jnp.zeros((g, 5, 2 * n), F32)], axis=1)
    bt = lambda b: twice(b.transpose(0, 2, 1))
    gspec = lambda shape: pl.BlockSpec((1,) + shape, lambda i: (i, 0, 0))
    return pl.pallas_call(
        _ssm_prep_kernel,
        grid=(g,),
        in_specs=[gspec((8, 2 * n))] + [gspec((p, 2 * n))] * 4 + [gspec((8, CHUNK_W))],
        out_specs=[gspec((CHUNK_W, 4 * n)), gspec((CHUNK_W, CHUNK_W)), gspec((CHUNK_W, 2 * n)),
                   gspec((8, 2 * n))],
        out_shape=[jax.ShapeDtypeStruct((g, CHUNK_W, 4 * n), BF16),
                   jax.ShapeDtypeStruct((g, CHUNK_W, CHUNK_W), BF16),
                   jax.ShapeDtypeStruct((g, CHUNK_W, 2 * n), BF16),
                   jax.ShapeDtypeStruct((g, 8, 2 * n), F32)],
        compiler_params=_params(1),
        name="ssm_prep",
    )(lam, bt(b_re), bt(b_im), twice(c_re), twice(c_im), dsk)


def _rope_tables():
    lane = jnp.arange(LANES) % HEAD_DIM
    half = ROPE_DIM // 2
    inv = ROPE_THETA ** (-jnp.arange(0, ROPE_DIM, 2, dtype=F32) / ROPE_DIM)
    expand = ((lane[None, :] < ROPE_DIM) & (lane[None, :] % half == jnp.arange(half)[:, None])).astype(F32)
    s_lo = jnp.where(lane < half, -1.0, 0.0)
    s_hi = jnp.where((lane >= half) & (lane < ROPE_DIM), 1.0, 0.0)
    unrot = jnp.where(lane >= ROPE_DIM, 1.0, 0.0)
    rows = jnp.concatenate([s_lo[None], s_hi[None], unrot[None], jnp.zeros((5, LANES), F32)], axis=0)
    expand = jnp.concatenate([expand, expand], axis=0).astype(BF16)
    return inv[:, None].astype(F32), expand, rows.astype(F32)


def _sigmoid(v):
    return 1.0 / (1.0 + jnp.exp(-v))


_NAT_DST = {2: NAT_ZS, 3: NAT_ZS + 1, Q_BLK: NAT_Q, K_BLK: NAT_K, V_BLK: NAT_V,
            ZA_BLK: NAT_ZA, 14: NAT_RS, 15: NAT_RS + 1, 16: NAT_RA, 17: NAT_RA + 1}


_PROJ_ORDER = (2, 4, 3, 7, 13, 5, 14, 8, 15, 6, 16, 9, 17, 11, 12, 10)
_N_PERM = 6


def _proj_kernel(x_ref, pos_ref, nw_ref, w_ref, qkw_ref, ones_ref, freq_ref, expand_ref, rope_ref,
                 nat_ref, d4_ref, d16_ref, perm_ref, stage_ref, h_ref):
    tm = x_ref.shape[0]
    x = x_ref[...]
    ms = jnp.mean(x * x, axis=-1, keepdims=True)
    h_ref[...] = (x * lax.rsqrt(ms + EPS) * nw_ref[...]).astype(BF16)
    ang = freq_ref[...] * pos_ref[...].astype(F32)

    def spread(v):
        hi = v.astype(BF16)
        lo = (v - hi.astype(F32)).astype(BF16)
        return lax.dot_general(jnp.concatenate([hi, lo], axis=0), expand_ref[...], (((0,), (0,)), ((), ())),
                               preferred_element_type=F32)

    cosv = spread(jnp.cos(ang)) + rope_ref[2:3, :]
    sinv = spread(jnp.sin(ang))
    s_lo = sinv * rope_ref[0:1, :]
    s_hi = sinv * rope_ref[1:2, :]
    slot = 0
    for j in _PROJ_ORDER:
        acc = jnp.dot(h_ref[...], w_ref[:, j * COL:(j + 1) * COL], preferred_element_type=F32)
        if ZS_BLK <= j < Q_BLK or j == ZA_BLK:
            res = acc * _sigmoid(acc)
        elif j >= RS_BLK:
            res = _sigmoid(acc)
        elif Q_BLK <= j < V_BLK:
            sq = (acc * acc).astype(BF16)
            hw = ones_ref.shape[0]
            ss = jnp.concatenate([jnp.dot(sq[:, c * hw:(c + 1) * hw], ones_ref[...], preferred_element_type=F32)
                                  for c in range(COL // hw)], axis=1)
            y = acc * lax.rsqrt(ss * (1.0 / HEAD_DIM) + EPS) * qkw_ref[j - Q_BLK:j - Q_BLK + 1, :]
            parts = []
            for c in range(COL // LANES):
                yc = y[:, c * LANES:(c + 1) * LANES]
                parts.append(yc * cosv + pltpu.roll(yc, LANES - ROPE_DIM // 2, 1) * s_lo
                             + pltpu.roll(yc, ROPE_DIM // 2, 1) * s_hi)
            res = jnp.concatenate(parts, axis=1)
        else:
            res = acc
        if j in _NAT_DST:
            dst = _NAT_DST[j]
            nat_ref[:, dst * COL:(dst + 1) * COL] = res.astype(BF16)
        else:
            kind, group = divmod(j - Q_BLK, len(DILATIONS))
            d = DILATIONS[group]
            out_ref = d4_ref if group == 1 else d16_ref
            nslab = COL // LANES
            for c in range(nslab):
                perm_ref[slot, c] = res[:, c * LANES:(c + 1) * LANES]
            if d == DILATIONS[2]:
                d1 = DILATIONS[1]
                for c in range(nslab):
                    for r1 in range(d1):
                        stage_ref[kind, c, r1 * (tm // d1):(r1 + 1) * (tm // d1), :] = (
                            perm_ref[slot, c, pl.ds(r1, tm // d1, stride=d1), :])
                for r1, r2 in [(r1, r2) for r1 in range(d1) for r2 in range(d // d1)]:
                    rows = [stage_ref[kind, c, pl.ds(r1 * (tm // d1) + r2, tm // d, stride=d1), :]
                            for c in range(nslab)]
                    out_ref[r1 + d1 * r2, :, kind * COL:(kind + 1) * COL] = (
                        jnp.concatenate(rows, axis=1).astype(BF16))
            else:
                for r in range(d):
                    rows = [perm_ref[slot, c, pl.ds(r, tm // d, stride=d), :] for c in range(nslab)]
                    out_ref[r, :, kind * COL:(kind + 1) * COL] = jnp.concatenate(rows, axis=1).astype(BF16)
            slot += 1


def _proj_call(x, positions, norm_w, w_in, q_norm_w, k_norm_w, tm=512):
    bsz, length, _ = x.shape
    scale = LOG2E / math.sqrt(HEAD_DIM)
    qkw = jnp.concatenate([jnp.tile(q_norm_w.astype(F32) * scale, (1, HEADS)),
                           jnp.tile(k_norm_w.astype(F32), (1, HEADS)),
                           jnp.zeros((2, ATTN_W), F32)], axis=0)
    hid = jnp.arange(MXU_DIM) // HEAD_DIM
    ones = (hid[:, None] == hid[None, :]).astype(BF16)
    d4, d16 = DILATIONS[1], DILATIONS[2]
    freq, expand, rope_rows = _rope_tables()
    return pl.pallas_call(
        _proj_kernel,
        grid=(bsz, length // tm),
        in_specs=[pl.BlockSpec((None, tm, D_MODEL), lambda b, i: (b, i, 0)),
                  pl.BlockSpec((None, 1, tm), lambda b, i: (b, 0, i)),
                  _full_spec((1, D_MODEL)),
                  _full_spec((D_MODEL, IN_WIDTH)),
                  _full_spec((8, ATTN_W)),
                  _full_spec((MXU_DIM, MXU_DIM)),
                  _full_spec((ROPE_DIM // 2, 1)),
                  _full_spec((ROPE_DIM, LANES)),
                  _full_spec((8, LANES))],
        out_specs=[pl.BlockSpec((None, tm, NAT_W), lambda b, i: (b, i, 0)),
                   pl.BlockSpec((None, d4, tm // d4, QKV_W), lambda b, i: (b, 0, i, 0)),
                   pl.BlockSpec((None, d16, tm // d16, QKV_W), lambda b, i: (b, 0, i, 0))],
        out_shape=[jax.ShapeDtypeStruct((bsz, length, NAT_W), BF16),
                   jax.ShapeDtypeStruct((bsz, d4, length // d4, QKV_W), BF16),
                   jax.ShapeDtypeStruct((bsz, d16, length // d16, QKV_W), BF16)],
        scratch_shapes=[pltpu.VMEM((_N_PERM, COL // LANES, tm, LANES), F32),
                        pltpu.VMEM((3, COL // LANES, tm, LANES), F32),
                        pltpu.VMEM((tm, D_MODEL), BF16)],
        compiler_params=_params(2),
        name="proj",
    )(x.astype(F32), positions.astype(jnp.int32)[:, None, :], norm_w.astype(F32)[None, :],
      w_in.astype(BF16), qkw, ones, freq, expand, rope_rows)


def _block_transpose(xs):
    xs = list(xs)
    blk = lax.broadcasted_iota(jnp.int32, xs[0].shape, 1) // SSM_GROUP
    dist = 1
    while dist < LANE_BLKS:
        upper = (blk & dist) != 0
        shift = dist * SSM_GROUP
        for i in range(LANE_BLKS):
            if i & dist:
                continue
            a, b = xs[i], xs[i + dist]
            xs[i] = jnp.where(upper, pltpu.roll(b, shift, 1), a)
            xs[i + dist] = jnp.where(upper, b, pltpu.roll(a, LANES - shift, 1))
        dist *= 2
    return xs


def _ssm_in_kernel(x_ref, nw_ref, w_ref, u2_ref, rows_ref):
    bsz, tl = x_ref.shape[0], x_ref.shape[1]
    for c in range(tl // CHUNK):
        x = x_ref[:, c * CHUNK:(c + 1) * CHUNK, :].reshape(bsz * CHUNK, D_MODEL)
        ms = jnp.mean(x * x, axis=-1, keepdims=True)
        h = (x * lax.rsqrt(ms + EPS) * nw_ref[...]).astype(BF16)
        u = jnp.dot(h, w_ref[...], preferred_element_type=F32)
        for b in range(bsz):
            for s8 in range(D_MODEL // LANES):
                rows_ref[c % SSM_SLOTS, s8, b * SSM_PITCH:b * SSM_PITCH + CHUNK, :] = (
                    u[b * CHUNK:(b + 1) * CHUNK, s8 * LANES:(s8 + 1) * LANES])
        for s8 in range(D_MODEL // LANES):
            for half in range(CHUNK // LANE_BLKS):
                xs = []
                for k in range(LANE_BLKS):
                    v = rows_ref[c % SSM_SLOTS, s8, pl.ds(half * LANE_BLKS + k, bsz, stride=SSM_PITCH), :].astype(BF16)
                    xs.append(pltpu.bitcast(v, jnp.uint32))
                ys = _block_transpose(xs)
                for g in range(LANE_BLKS):
                    u2_ref[s8 * LANE_BLKS + g, c, :, half * LANES:(half + 1) * LANES] = pltpu.bitcast(ys[g], BF16)


def _ssm_in_call(x, norm_w, w_u):
    bsz, length, _ = x.shape
    tl = SSM_CB * CHUNK
    nc = length // CHUNK
    return pl.pallas_call(
        _ssm_in_kernel,
        grid=(length // tl,),
        in_specs=[pl.BlockSpec((bsz, tl, D_MODEL), lambda i: (0, i, 0)),
                  _full_spec((1, D_MODEL)), _full_spec((D_MODEL, D_MODEL))],
        out_specs=pl.BlockSpec((SSM_GROUPS, SSM_CB, bsz, CHUNK_W), lambda i: (0, i, 0, 0)),
        out_shape=jax.ShapeDtypeStruct((SSM_GROUPS, nc, bsz, CHUNK_W), BF16),
        scratch_shapes=[pltpu.VMEM((SSM_SLOTS, D_MODEL // LANES, bsz * SSM_PITCH, LANES), F32)],
        compiler_params=_params(1),
        name="ssm_in",
    )(x, norm_w.astype(F32)[None, :], w_u.astype(BF16))


def _ssm_out_kernel(y2_ref, zs_ref, rs_ref, wglu_ref, wso_ref, o_ref, rows_ref):
    bsz, tl = zs_ref.shape[0], zs_ref.shape[1]
    for c in range(tl // CHUNK):
        for s8 in range(D_MODEL // LANES):
            for half in range(CHUNK // LANE_BLKS):
                xs = [pltpu.bitcast(y2_ref[s8 * LANE_BLKS + g, c, :, half * LANES:(half + 1) * LANES], jnp.uint32)
                      for g in range(LANE_BLKS)]
                ys = _block_transpose(xs)
                for k in range(LANE_BLKS):
                    rows_ref[c % SSM_SLOTS, s8, pl.ds(half * LANE_BLKS + k, bsz, stride=SSM_PITCH), :] = (
                        pltpu.bitcast(ys[k], BF16).astype(F32))
        y = jnp.concatenate(
            [jnp.concatenate([rows_ref[c % SSM_SLOTS, s8, b * SSM_PITCH:b * SSM_PITCH + CHUNK, :]
                              for s8 in range(D_MODEL // LANES)], axis=1)
             for b in range(bsz)], axis=0)
        tok = slice(c * CHUNK, (c + 1) * CHUNK)
        zs = zs_ref[:, tok, :].reshape(bsz * CHUNK, D_MODEL).astype(F32)
        g = y / (1.0 + jnp.exp2(y * (GELU_K1 + GELU_K3 * (y * y))))
        gate = _sigmoid(jnp.dot(g.astype(BF16), wglu_ref[...], preferred_element_type=F32))
        ys_in = (g * gate * zs).astype(BF16)
        ys = jnp.dot(ys_in, wso_ref[...], preferred_element_type=F32)
        gated = ys * rs_ref[:, tok, :].reshape(bsz * CHUNK, D_MODEL).astype(F32)
        o_ref[:, tok, :] = gated.reshape(bsz, CHUNK, D_MODEL).astype(BF16)


def _ssm_out_call(y2, nat, w_glu, w_ssm_out):
    bsz, length, _ = nat.shape
    tl = SSM_CB * CHUNK
    tok = lambda blk: pl.BlockSpec((bsz, tl, D_MODEL), lambda i: (0, i, blk))
    return pl.pallas_call(
        _ssm_out_kernel,
        grid=(length // tl,),
        in_specs=[pl.BlockSpec((SSM_GROUPS, SSM_CB, bsz, CHUNK_W), lambda i: (0, i, 0, 0)),
                  tok(NAT_ZS // 2), tok(NAT_RS // 2),
                  _full_spec((D_MODEL, D_MODEL)), _full_spec((D_MODEL, D_MODEL))],
        out_specs=tok(0),
        out_shape=jax.ShapeDtypeStruct((bsz, length, D_MODEL), BF16),
        scratch_shapes=[pltpu.VMEM((SSM_SLOTS, D_MODEL // LANES, bsz * SSM_PITCH, LANES), F32)],
        compiler_params=_params(1),
        name="ssm_out",
    )(y2, nat, nat, w_glu.astype(BF16), w_ssm_out.astype(BF16))


def _ssm_kernel(u_ref, st_ref, mt_ref, rtt_ref, coef_ref, y_ref, hloc_ref, hprev_ref):
    ng, nc, bsz = u_ref.shape[0], u_ref.shape[1], u_ref.shape[2]
    ns2 = 2 * SSM_STATE
    us, coefs = [], []
    for g in range(ng):
        u = u_ref[g].reshape(nc * bsz, CHUNK_W)
        us.append(u)
        half = nc * bsz // 2
        hloc_ref[g, :half, :] = jnp.dot(u[:half], st_ref[g], preferred_element_type=F32)
        hloc_ref[g, half:, :] = jnp.dot(u[half:], st_ref[g], preferred_element_type=F32)
        coefs.append((jnp.broadcast_to(coef_ref[g, 0:1, :], (bsz, ns2)),
                      jnp.broadcast_to(coef_ref[g, 1:2, :], (bsz, ns2))))

    def step(c, carry):
        r = pl.multiple_of(c * bsz, bsz)
        out = []
        for g in range(ng):
            hx, hy = carry[2 * g], carry[2 * g + 1]
            a, b = coefs[g]
            hprev_ref[g, pl.ds(r, bsz), :] = hx.astype(BF16)
            px = hloc_ref[g, pl.ds(r, bsz), 0:ns2]
            py = hloc_ref[g, pl.ds(r, bsz), ns2:2 * ns2]
            out += [a * hx + b * hy + px, a * hy - b * hx + py]
        return tuple(out)

    zero = jnp.zeros((bsz, ns2), F32)
    lax.fori_loop(0, nc, step, (zero,) * (2 * ng), unroll=8)
    for g in range(ng):
        y = jnp.dot(us[g], mt_ref[g], preferred_element_type=F32)
        y = y + lax.dot_general(hprev_ref[g], rtt_ref[g], (((1,), (1,)), ((), ())), preferred_element_type=F32)
        y_ref[g] = y.reshape(nc, bsz, CHUNK_W).astype(BF16)


def _ssm_call(u2, st, mt, rt, coef):
    g, nc, bsz, _ = u2.shape
    ng = SCAN_GROUPS
    gspec = lambda shape: pl.BlockSpec((ng,) + shape, lambda i: (i,) + (0,) * len(shape))
    return pl.pallas_call(
        _ssm_kernel,
        grid=(g // ng,),
        in_specs=[gspec((nc, bsz, CHUNK_W)), gspec((CHUNK_W, 4 * SSM_STATE)), gspec((CHUNK_W, CHUNK_W)),
                  gspec((CHUNK_W, 2 * SSM_STATE)), gspec((8, 2 * SSM_STATE))],
        out_specs=gspec((nc, bsz, CHUNK_W)),
        out_shape=jax.ShapeDtypeStruct(u2.shape, BF16),
        scratch_shapes=[pltpu.VMEM((ng, nc * bsz, 4 * SSM_STATE), F32),
                        pltpu.VMEM((ng, nc * bsz, 2 * SSM_STATE), BF16)],
        compiler_params=_params(1),
        name="ssm_scan",
    )(u2, st, mt, rt, coef)


def _attn_kernel(q_ref, kc_ref, kp_ref, vc_ref, vp_ref, o_ref, lse_ref):
    nres, tq = q_ref.shape[0], q_ref.shape[1]
    qb = 128
    slab = pl.program_id(2)
    qi = lax.broadcasted_iota(jnp.int32, (qb, 2 * qb), 0)
    kk = lax.broadcasted_iota(jnp.int32, (qb, 2 * qb), 1)
    band = jnp.where((kk >= qi) & (kk <= qi + WINDOW_KEYS), 0.0, NEG_INF).astype(F32)
    band0 = band + jnp.where(jnp.logical_and(slab == 0, kk < qb), NEG_INF, 0.0).astype(F32)
    band = jnp.concatenate([band, band], axis=0)
    band0 = jnp.concatenate([band0, band0], axis=0)
    lo_q = lax.broadcasted_iota(jnp.int32, (qb, LANES), 1) < HEAD_DIM
    ones_kv = jnp.ones((2 * qb, LANES), BF16)

    head_slot = lax.broadcasted_iota(jnp.int32, (qb, LANES), 1) // LSE_LANES

    for r, j in [(r, j) for r in range(nres) for j in range(tq // qb)]:
        madd = band0 if j == 0 else band
        rows = slice(j * qb, (j + 1) * qb)
        m_all = jnp.zeros((qb, LANES), F32)
        den_all = jnp.ones((qb, LANES), F32)
        for hp in range(ATTN_W // LANES):
            cs = slice(hp * LANES, (hp + 1) * LANES)
            qp = q_ref[r, rows, cs]
            if j == 0:
                kp = jnp.concatenate([kp_ref[r, :, cs], kc_ref[r, 0:qb, cs]], axis=0)
                vp = jnp.concatenate([vp_ref[r, :, cs], vc_ref[r, 0:qb, cs]], axis=0)
            else:
                kp = kc_ref[r, (j - 1) * qb:(j + 1) * qb, cs]
                vp = vc_ref[r, (j - 1) * qb:(j + 1) * qb, cs]
            zero = jnp.zeros_like(qp)
            q2 = jnp.concatenate([jnp.where(lo_q, qp, zero), jnp.where(lo_q, zero, qp)], axis=0)
            s = lax.dot_general(q2, kp, (((1,), (1,)), ((), ())), preferred_element_type=F32)
            s = s + madd
            m = jnp.max(s, axis=1, keepdims=True)
            p = jnp.exp2(s - m)
            pv = jnp.dot(p.astype(BF16), jnp.concatenate([vp, ones_kv], axis=1), preferred_element_type=F32)
            num = jnp.where(lo_q, pv[:qb, :LANES], pv[qb:, :LANES])
            den = jnp.where(lo_q, pv[:qb, LANES:], pv[qb:, LANES:])
            o_ref[r, rows, cs] = (num / den).astype(BF16)
            for side in range(2):
                slot = head_slot == 2 * hp + side
                half = slice(side * qb, (side + 1) * qb)
                m_all = jnp.where(slot, m[half], m_all)
                den_all = jnp.where(slot, pv[half, LANES:], den_all)
        lse_ref[r, rows, :] = m_all + jnp.log2(den_all)


def _attn_call(qkv, group, blocks, rows_per_step=4096):
    bsz, d, ld, _ = qkv.shape
    tq = min(rows_per_step, ld)
    nres = min(d, rows_per_step // tq)
    nslab = ld // tq
    per = tq // 128
    cur = lambda blk: pl.BlockSpec((None, nres, tq, COL), lambda b, r, i: (b, r, i, blk))
    prev = lambda blk: pl.BlockSpec(
        (None, nres, 128, COL), lambda b, r, i: (b, r, jnp.maximum(i * per - 1, 0), blk))
    ospec = lambda w: pl.BlockSpec((None, nres, tq, w), lambda b, r, i: (b, r, i, 0))
    qb, kb, vb = blocks
    return pl.pallas_call(
        _attn_kernel,
        grid=(bsz, d // nres, nslab),
        in_specs=[cur(qb), cur(kb), prev(kb), cur(vb), prev(vb)],
        out_specs=[ospec(ATTN_W), ospec(LANES)],
        out_shape=[jax.ShapeDtypeStruct((bsz, d, ld, ATTN_W), BF16),
                   jax.ShapeDtypeStruct((bsz, d, ld, LANES), F32)],
        compiler_params=_params(3),
        name=f"attn_d{DILATIONS[group]}",
    )(qkv, qkv, qkv, qkv, qkv)


def _merge_kernel(x_ref, ys_ref, za_ref, ra_ref,
                  a0_ref, a1_ref, a2_ref, l0_ref, l1_ref, l2_ref,
                  spread_ref, wao_ref, wo_ref, o_ref, il_ref, stage_ref):
    tm = x_ref.shape[0]
    d1 = DILATIONS[1]

    def to_token_order(ref, base, stage_base=None):
        d, nslab = ref.shape[0], ref.shape[2] // LANES
        for r in range(d):
            blk = ref[r].astype(F32)
            for c in range(nslab):
                piece = blk[:, c * LANES:(c + 1) * LANES]
                if d == d1:
                    il_ref[base + c, pl.ds(r, tm // d, stride=d), :] = piece
                else:
                    r1, r2 = r % d1, r // d1
                    stage_ref[stage_base + c, pl.ds(r1 * (tm // d1) + r2, tm // d, stride=d1), :] = piece
        if d != d1:
            for c in range(nslab):
                for r1 in range(d1):
                    il_ref[base + c, pl.ds(r1, tm // d1, stride=d1), :] = (
                        stage_ref[stage_base + c, r1 * (tm // d1):(r1 + 1) * (tm // d1), :])
        return lambda rows: jnp.concatenate([il_ref[base + c, rows, :] for c in range(nslab)], axis=1)

    def per_head_to_lanes(w):
        hi = w.astype(BF16)
        lo = (w - hi.astype(F32)).astype(BF16)
        return jnp.dot(jnp.concatenate([hi, lo], axis=1), spread_ref[...], preferred_element_type=F32)

    wide_slabs = ATTN_W // LANES
    l1_at, l2_at = to_token_order(l1_ref, 0), to_token_order(l2_ref, 1, 0)
    a1_at, a2_at = to_token_order(a1_ref, 2), to_token_order(a2_ref, 2 + wide_slabs, 1)
    rows = slice(None)
    l0, l1, l2 = l0_ref[...], l1_at(rows), l2_at(rows)
    lm = jnp.maximum(jnp.maximum(l0, l1), l2)
    e0, e1, e2 = jnp.exp2(l0 - lm), jnp.exp2(l1 - lm), jnp.exp2(l2 - lm)
    inv = 1.0 / (e0 + e1 + e2)
    att = (per_head_to_lanes(e0 * inv) * a0_ref[...].astype(F32)
           + per_head_to_lanes(e1 * inv) * a1_at(rows)
           + per_head_to_lanes(e2 * inv) * a2_at(rows))
    ya_in = (att * za_ref[...].astype(F32)).astype(BF16)
    ya = jnp.dot(ya_in, wao_ref[...], preferred_element_type=F32)
    m = ys_ref[...].astype(F32) + ra_ref[...].astype(F32) * ya
    o_ref[...] = x_ref[...] + jnp.dot(m.astype(BF16), wo_ref[...], preferred_element_type=F32)


def _merge_call(x, ys, nat, attn, w_attn_out, w_o, tm=1024):
    bsz, length, _ = x.shape
    wide = lambda blk: pl.BlockSpec((None, tm, D_MODEL), lambda b, i: (b, i, blk))
    half = lambda blk: pl.BlockSpec((None, tm, ATTN_W), lambda b, i: (b, i, blk))
    res = lambda d, w: pl.BlockSpec((None, d, tm // d, w), lambda b, i: (b, 0, i, 0))
    (a0, l0), (a1, l1), (a2, l2) = attn
    d4, d16 = DILATIONS[1], DILATIONS[2]
    a0, l0 = a0.reshape(bsz, length, ATTN_W), l0.reshape(bsz, length, LANES)
    spread = (jnp.arange(LANES)[:, None] == (jnp.arange(ATTN_W)[None, :] // HEAD_DIM) * LSE_LANES).astype(BF16)
    spread = jnp.concatenate([spread, spread], axis=0)
    return pl.pallas_call(
        _merge_kernel,
        grid=(bsz, length // tm),
        in_specs=[wide(0), wide(0), half(NAT_ZA), wide(NAT_RA // 2),
                  half(0), res(d4, ATTN_W), res(d16, ATTN_W),
                  pl.BlockSpec((None, tm, LANES), lambda b, i: (b, i, 0)), res(d4, LANES), res(d16, LANES),
                  _full_spec((2 * LANES, ATTN_W)), _full_spec((ATTN_W, D_MODEL)), _full_spec((D_MODEL, D_MODEL))],
        out_specs=wide(0),
        out_shape=jax.ShapeDtypeStruct((bsz, length, D_MODEL), F32),
        scratch_shapes=[pltpu.VMEM((2 + 2 * (ATTN_W // LANES), tm, LANES), F32),
                        pltpu.VMEM((1 + ATTN_W // LANES, tm, LANES), F32)],
        compiler_params=_params(2),
        name="merge",
    )(x, ys, nat, nat, a0, a1, a2, l0, l1, l2, spread, w_attn_out.astype(BF16), w_o.astype(BF16))


def kernel(x, positions, norm_w, w_in, lam_re, lam_im, log_dt, b_re, b_im, c_re, c_im, d_skip, w_glu,
           q_norm_w, k_norm_w, w_ssm_out, w_attn_out, w_o):
    bsz, length, width = x.shape
    assert width == D_MODEL and w_in.shape[1:] == (D_MODEL, IN_WIDTH)
    assert bsz % 16 == 0, "the scan layout packs the batch into 16-row bf16 tiles"
    assert length % (128 * DILATIONS[-1]) == 0, "every residue class must hold whole 128-query blocks"
    xf = x.astype(F32)
    for layer in range(norm_w.shape[0]):
        st, mt, rt, coef = _ssm_prep_call(lam_re[layer], lam_im[layer], log_dt[layer], b_re[layer],
                                          b_im[layer], c_re[layer], c_im[layer], d_skip[layer])
        nat, qkv4, qkv16 = _proj_call(xf, positions, norm_w[layer], w_in[layer], q_norm_w[layer],
                                      k_norm_w[layer])
        u2 = _ssm_in_call(xf, norm_w[layer], w_in[layer][:, :D_MODEL])
        y2 = _ssm_call(u2, st, mt, rt, coef)
        ys = _ssm_out_call(y2, nat, w_glu[layer], w_ssm_out[layer])
        attn = [_attn_call(nat[:, None], 0, (NAT_Q, NAT_K, NAT_V)),
                _attn_call(qkv4, 1, (0, 1, 2)),
                _attn_call(qkv16, 2, (0, 1, 2))]
        xf = _merge_call(xf, ys, nat, attn, w_attn_out[layer], w_o[layer])
    return xf.astype(x.dtype)
```

```python
import math

import jax
import jax.numpy as jnp
from jax import lax
from jax.experimental import pallas as pl
from jax.experimental.pallas import tpu as pltpu

F32 = jnp.float32
BF16 = jnp.bfloat16

D_MODEL = 1024
SSM_GROUP = 16
SSM_GROUPS = D_MODEL // SSM_GROUP
SSM_STATE = 64
CHUNK = 16
CHUNK_W = CHUNK * SSM_GROUP
HEAD_DIM = 64
HEADS = 8
ATTN_W = HEADS * HEAD_DIM
DILATIONS = (1, 4, 16)
WINDOW_KEYS = 128
ROPE_DIM = HEAD_DIM // 4
ROPE_THETA = 500000.0
EPS = 1e-6
NEG_INF = -1e30
LOG2E = math.log2(math.e)
GELU_K1 = -2.0 * math.sqrt(2.0 / math.pi) * LOG2E
GELU_K3 = GELU_K1 * 0.044715
COL = 512
ZS_BLK, Q_BLK, K_BLK, V_BLK, ZA_BLK, RS_BLK, RA_BLK = 2, 4, 7, 10, 13, 14, 16
N_BLK = 18
IN_WIDTH = N_BLK * COL
NAT_ZS, NAT_Q, NAT_K, NAT_V, NAT_ZA, NAT_RS, NAT_RA = 0, 2, 3, 4, 5, 6, 8
NAT_W = 10 * COL
QKV_W = 3 * COL
LANES = 128
MXU_DIM = 256
LSE_LANES = LANES // HEADS
LANE_BLKS = LANES // SSM_GROUP
SCAN_GROUPS = 2
SSM_CB = 8
SSM_SLOTS = 2
SSM_PITCH = 4 * 5
VMEM_LIMIT = 56 * 1024 * 1024


def _params(grid_rank):
    return pltpu.CompilerParams(dimension_semantics=("arbitrary",) * grid_rank, vmem_limit_bytes=VMEM_LIMIT)


def _full_spec(shape):
    nd = len(shape)
    return pl.BlockSpec(shape, lambda *_: (0,) * nd, pipeline_mode=pl.Buffered(1))


def _ssm_prep_kernel(lam_ref, btr_ref, bti_ref, cr_ref, ci_ref, dsk_ref, st_ref, mt_ref, rtt_ref, coef_ref):
    ns2 = 2 * SSM_STATE
    lam = lam_ref[0]
    lr, li, dt = lam[0:1, :], lam[1:2, :], jnp.exp(lam[2:3, :])
    mag = jnp.exp(lr * dt)
    ar = mag * jnp.cos(li * dt)
    ai = mag * jnp.sin(li * dt)
    den = lr * lr + li * li
    nr = ar - 1.0
    fr = (nr * lr + ai * li) / den
    fi = (ai * lr - nr * li) / den
    pr, pi = [jnp.ones_like(ar)], [jnp.zeros_like(ar)]
    for _ in range(CHUNK):
        pr, pi = pr + [pr[-1] * ar - pi[-1] * ai], pi + [pr[-1] * ai + pi[-1] * ar]
    by_step = lambda vals: jnp.concatenate([jnp.broadcast_to(v, (SSM_GROUP, ns2)) for v in vals], axis=0)
    per_step = lambda a: jnp.concatenate([a] * CHUNK, axis=0)
    im_part = lax.broadcasted_iota(jnp.int32, (CHUNK_W, ns2), 1) >= SSM_STATE

    btr, bti = per_step(btr_ref[0]), per_step(bti_ref[0])
    bbr = fr * btr - fi * bti
    bbi = fr * bti + fi * btr
    apr = by_step([pr[CHUNK - 1 - s] for s in range(CHUNK)])
    api = by_step([pi[CHUNK - 1 - s] for s in range(CHUNK)])
    wre = apr * bbr - api * bbi
    wim = apr * bbi + api * bbr
    st = jnp.where(im_part, wim, wre)
    st_sw = jnp.where(im_part, wre, wim)
    st_ref[0] = jnp.concatenate([st, st_sw], axis=1).astype(BF16)

    cr, ci = per_step(cr_ref[0]), per_step(ci_ref[0])
    qr = by_step([pr[t + 1] for t in range(CHUNK)])
    qi = by_step([pi[t + 1] for t in range(CHUNK)])
    rtt_ref[0] = jnp.where(im_part, -(cr * qi + ci * qr), cr * qr - ci * qi).astype(BF16)

    cct = jnp.where(im_part, -ci, cr)
    split = lambda v: (v.astype(BF16), (v - v.astype(BF16).astype(F32)).astype(BF16))
    (st_hi, st_lo), (cc_hi, cc_lo) = split(st), split(cct)
    nt = (((1,), (1,)), ((), ()))
    krw = (lax.dot_general(jnp.concatenate([st_hi, st_lo], axis=1), jnp.concatenate([cc_hi, cc_hi], axis=1), nt,
                           preferred_element_type=F32)
           + lax.dot_general(st_hi, cc_lo, nt, preferred_element_type=F32))
    lane_blk = lax.broadcasted_iota(jnp.int32, (CHUNK_W, CHUNK_W), 1) // SSM_GROUP
    mt = jnp.zeros((CHUNK_W, CHUNK_W), F32)
    for t in range(CHUNK):
        sh = SSM_GROUP * (CHUNK - 1 - t)
        if sh == 0:
            shifted = krw
        else:
            shifted = jnp.concatenate([krw[sh:, :], jnp.zeros((sh, CHUNK_W), F32)], axis=0)
        mt = jnp.where(lane_blk == t, shifted, mt)
    on_diag = (lax.broadcasted_iota(jnp.int32, (CHUNK_W, CHUNK_W), 0)
               == lax.broadcasted_iota(jnp.int32, (CHUNK_W, CHUNK_W), 1))
    mt_ref[0] = (mt + jnp.where(on_diag, dsk_ref[0, 0:1, :], 0.0)).astype(BF16)

    im_row = lax.broadcasted_iota(jnp.int32, (1, ns2), 1) >= SSM_STATE
    bco = jnp.where(im_row, pi[CHUNK], -pi[CHUNK])
    coef_ref[0] = jnp.concatenate([pr[CHUNK], bco, jnp.zeros((6, ns2), F32)], axis=0)


def _ssm_prep_call(lam_re, lam_im, log_dt, b_re, b_im, c_re, c_im, d_skip):
    g, n, p = SSM_GROUPS, SSM_STATE, SSM_GROUP
    dsk = jnp.tile(d_skip.astype(F32).reshape(g, 1, p), (1, 1, CHUNK))
    dsk = jnp.concatenate([dsk, jnp.zeros((g, 7, CHUNK_W), F32)], axis=1)
    f = lambda a: a.astype(F32)
    twice = lambda a: jnp.tile(f(a), (1,) * (a.ndim - 1) + (2,))
    ldt_b = jnp.broadcast_to(f(log_dt)[:, None], (g, n))
    lam = jnp.concatenate([twice(lam_re)[:, None], twice(lam_im)[:, None], twice(ldt_b)[:, None],
                           jnp.zeros((g, 5, 2 * n), F32)], axis=1)
    bt = lambda b: twice(b.transpose(0, 2, 1))
    gspec = lambda shape: pl.BlockSpec((1,) + shape, lambda i: (i, 0, 0))
    return pl.pallas_call(
        _ssm_prep_kernel,
        grid=(g,),
        in_specs=[gspec((8, 2 * n))] + [gspec((p, 2 * n))] * 4 + [gspec((8, CHUNK_W))],
        out_specs=[gspec((CHUNK_W, 4 * n)), gspec((CHUNK_W, CHUNK_W)), gspec((CHUNK_W, 2 * n)),
                   gspec((8, 2 * n))],
        out_shape=[jax.ShapeDtypeStruct((g, CHUNK_W, 4 * n), BF16),
                   jax.ShapeDtypeStruct((g, CHUNK_W, CHUNK_W), BF16),
                   jax.ShapeDtypeStruct((g, CHUNK_W, 2 * n), BF16),
                   jax.ShapeDtypeStruct((g, 8, 2 * n), F32)],
        compiler_params=_params(1),
        name="ssm_prep",
    )(lam, bt(b_re), bt(b_im), twice(c_re), twice(c_im), dsk)


def _rope_tables():
    lane = jnp.arange(LANES) % HEAD_DIM
    half = ROPE_DIM // 2
    inv = ROPE_THETA ** (-jnp.arange(0, ROPE_DIM, 2, dtype=F32) / ROPE_DIM)
    expand = ((lane[None, :] < ROPE_DIM) & (lane[None, :] % half == jnp.arange(half)[:, None])).astype(F32)
    s_lo = jnp.where(lane < half, -1.0, 0.0)
    s_hi = jnp.where((lane >= half) & (lane < ROPE_DIM), 1.0, 0.0)
    unrot = jnp.where(lane >= ROPE_DIM, 1.0, 0.0)
    rows = jnp.concatenate([s_lo[None], s_hi[None], unrot[None], jnp.zeros((5, LANES), F32)], axis=0)
    expand = jnp.concatenate([expand, expand], axis=0).astype(BF16)
    return inv[:, None].astype(F32), expand, rows.astype(F32)


def _sigmoid(v):
    return 1.0 / (1.0 + jnp.exp(-v))


_NAT_DST = {2: NAT_ZS, 3: NAT_ZS + 1, Q_BLK: NAT_Q, K_BLK: NAT_K, V_BLK: NAT_V,
            ZA_BLK: NAT_ZA, 14: NAT_RS, 15: NAT_RS + 1, 16: NAT_RA, 17: NAT_RA + 1}


_PROJ_ORDER = (2, 4, 3, 7, 13, 5, 14, 8, 15, 6, 16, 9, 17, 11, 12, 10)
_N_PERM = 6


def _proj_kernel(x_ref, pos_ref, nw_ref, w_ref, qkw_ref, ones_ref, freq_ref, expand_ref, rope_ref,
                 nat_ref, d4_ref, d16_ref, perm_ref, stage_ref, h_ref, acc_ref):
    tm = x_ref.shape[0]
    x = x_ref[...]
    ms = jnp.mean(x * x, axis=-1, keepdims=True)
    h_ref[...] = (x * lax.rsqrt(ms + EPS) * nw_ref[...]).astype(BF16)
    ang = freq_ref[...] * pos_ref[...].astype(F32)

    def spread(v):
        hi = v.astype(BF16)
        lo = (v - hi.astype(F32)).astype(BF16)
        return lax.dot_general(jnp.concatenate([hi, lo], axis=0), expand_ref[...], (((0,), (0,)), ((), ())),
                               preferred_element_type=F32)

    cosv = spread(jnp.cos(ang)) + rope_ref[2:3, :]
    sinv = spread(jnp.sin(ang))
    s_lo = sinv * rope_ref[0:1, :]
    s_hi = sinv * rope_ref[1:2, :]
    def epilogue(j, acc, rows):
        if ZS_BLK <= j < Q_BLK or j == ZA_BLK:
            return acc * _sigmoid(acc)
        if j >= RS_BLK:
            return _sigmoid(acc)
        if Q_BLK <= j < V_BLK:
            sq = (acc * acc).astype(BF16)
            hw = ones_ref.shape[0]
            ss = jnp.concatenate([jnp.dot(sq[:, c * hw:(c + 1) * hw], ones_ref[...], preferred_element_type=F32)
                                  for c in range(COL // hw)], axis=1)
            y = acc * lax.rsqrt(ss * (1.0 / HEAD_DIM) + EPS) * qkw_ref[j - Q_BLK:j - Q_BLK + 1, :]
            parts = []
            for c in range(COL // LANES):
                yc = y[:, c * LANES:(c + 1) * LANES]
                parts.append(yc * cosv[rows] + pltpu.roll(yc, LANES - ROPE_DIM // 2, 1) * s_lo[rows]
                             + pltpu.roll(yc, ROPE_DIM // 2, 1) * s_hi[rows])
            return jnp.concatenate(parts, axis=1)
        return acc

    slot = 0
    for idx, j in enumerate(_PROJ_ORDER):
        acc_ref[idx % 2] = jnp.dot(h_ref[...], w_ref[:, j * COL:(j + 1) * COL], preferred_element_type=F32)
        halves = [slice(0, tm // 2), slice(tm // 2, tm)]
        res = jnp.concatenate([epilogue(j, acc_ref[idx % 2, rows, :], rows) for rows in halves], axis=0)
        if j in _NAT_DST:
            dst = _NAT_DST[j]
            nat_ref[:, dst * COL:(dst + 1) * COL] = res.astype(BF16)
        else:
            kind, group = divmod(j - Q_BLK, len(DILATIONS))
            d = DILATIONS[group]
            out_ref = d4_ref if group == 1 else d16_ref
            nslab = COL // LANES
            for c in range(nslab):
                perm_ref[slot, c] = res[:, c * LANES:(c + 1) * LANES]
            if d == DILATIONS[2]:
                d1 = DILATIONS[1]
                for c in range(nslab):
                    for r1 in range(d1):
                        stage_ref[kind, c, r1 * (tm // d1):(r1 + 1) * (tm // d1), :] = (
                            perm_ref[slot, c, pl.ds(r1, tm // d1, stride=d1), :])
                for r1, r2 in [(r1, r2) for r1 in range(d1) for r2 in range(d // d1)]:
                    rows = [stage_ref[kind, c, pl.ds(r1 * (tm // d1) + r2, tm // d, stride=d1), :]
                            for c in range(nslab)]
                    out_ref[r1 + d1 * r2, :, kind * COL:(kind + 1) * COL] = (
                        jnp.concatenate(rows, axis=1).astype(BF16))
            else:
                for r in range(d):
                    rows = [perm_ref[slot, c, pl.ds(r, tm // d, stride=d), :] for c in range(nslab)]
                    out_ref[r, :, kind * COL:(kind + 1) * COL] = jnp.concatenate(rows, axis=1).astype(BF16)
            slot += 1


def _proj_call(x, positions, norm_w, w_in, q_norm_w, k_norm_w, tm=512):
    bsz, length, _ = x.shape
    scale = LOG2E / math.sqrt(HEAD_DIM)
    qkw = jnp.concatenate([jnp.tile(q_norm_w.astype(F32) * scale, (1, HEADS)),
                           jnp.tile(k_norm_w.astype(F32), (1, HEADS)),
                           jnp.zeros((2, ATTN_W), F32)], axis=0)
    hid = jnp.arange(MXU_DIM) // HEAD_DIM
    ones = (hid[:, None] == hid[None, :]).astype(BF16)
    d4, d16 = DILATIONS[1], DILATIONS[2]
    freq, expand, rope_rows = _rope_tables()
    return pl.pallas_call(
        _proj_kernel,
        grid=(bsz, length // tm),
        in_specs=[pl.BlockSpec((None, tm, D_MODEL), lambda b, i: (b, i, 0)),
                  pl.BlockSpec((None, 1, tm), lambda b, i: (b, 0, i)),
                  _full_spec((1, D_MODEL)),
                  _full_spec((D_MODEL, IN_WIDTH)),
                  _full_spec((8, ATTN_W)),
                  _full_spec((MXU_DIM, MXU_DIM)),
                  _full_spec((ROPE_DIM // 2, 1)),
                  _full_spec((ROPE_DIM, LANES)),
                  _full_spec((8, LANES))],
        out_specs=[pl.BlockSpec((None, tm, NAT_W), lambda b, i: (b, i, 0)),
                   pl.BlockSpec((None, d4, tm // d4, QKV_W), lambda b, i: (b, 0, i, 0)),
                   pl.BlockSpec((None, d16, tm // d16, QKV_W), lambda b, i: (b, 0, i, 0))],
        out_shape=[jax.ShapeDtypeStruct((bsz, length, NAT_W), BF16),
                   jax.ShapeDtypeStruct((bsz, d4, length // d4, QKV_W), BF16),
                   jax.ShapeDtypeStruct((bsz, d16, length // d16, QKV_W), BF16)],
        scratch_shapes=[pltpu.VMEM((_N_PERM, COL // LANES, tm, LANES), F32),
                        pltpu.VMEM((3, COL // LANES, tm, LANES), F32),
                        pltpu.VMEM((tm, D_MODEL), BF16), pltpu.VMEM((2, tm, COL), F32)],
        compiler_params=_params(2),
        name="proj",
    )(x.astype(F32), positions.astype(jnp.int32)[:, None, :], norm_w.astype(F32)[None, :],
      w_in.astype(BF16), qkw, ones, freq, expand, rope_rows)


def _block_transpose(xs):
    xs = list(xs)
    blk = lax.broadcasted_iota(jnp.int32, xs[0].shape, 1) // SSM_GROUP
    dist = 1
    while dist < LANE_BLKS:
        upper = (blk & dist) != 0
        shift = dist * SSM_GROUP
        for i in range(LANE_BLKS):
            if i & dist:
                continue
            a, b = xs[i], xs[i + dist]
            xs[i] = jnp.where(upper, pltpu.roll(b, shift, 1), a)
            xs[i + dist] = jnp.where(upper, b, pltpu.roll(a, LANES - shift, 1))
        dist *= 2
    return xs


def _ssm_in_kernel(x_ref, nw_ref, w_ref, u2_ref, rows_ref):
    bsz, tl = x_ref.shape[0], x_ref.shape[1]
    for c in range(tl // CHUNK):
        x = x_ref[:, c * CHUNK:(c + 1) * CHUNK, :].reshape(bsz * CHUNK, D_MODEL)
        ms = jnp.mean(x * x, axis=-1, keepdims=True)
        h = (x * lax.rsqrt(ms + EPS) * nw_ref[...]).astype(BF16)
        u = jnp.dot(h, w_ref[...], preferred_element_type=F32)
        for b in range(bsz):
            for s8 in range(D_MODEL // LANES):
                rows_ref[c % SSM_SLOTS, s8, b * SSM_PITCH:b * SSM_PITCH + CHUNK, :] = (
                    u[b * CHUNK:(b + 1) * CHUNK, s8 * LANES:(s8 + 1) * LANES])
        for s8 in range(D_MODEL // LANES):
            for half in range(CHUNK // LANE_BLKS):
                xs = []
                for k in range(LANE_BLKS):
                    v = rows_ref[c % SSM_SLOTS, s8, pl.ds(half * LANE_BLKS + k, bsz, stride=SSM_PITCH), :].astype(BF16)
                    xs.append(pltpu.bitcast(v, jnp.uint32))
                ys = _block_transpose(xs)
                for g in range(LANE_BLKS):
                    u2_ref[s8 * LANE_BLKS + g, c, :, half * LANES:(half + 1) * LANES] = pltpu.bitcast(ys[g], BF16)


def _ssm_in_call(x, norm_w, w_u):
    bsz, length, _ = x.shape
    tl = SSM_CB * CHUNK
    nc = length // CHUNK
    return pl.pallas_call(
        _ssm_in_kernel,
        grid=(length // tl,),
        in_specs=[pl.BlockSpec((bsz, tl, D_MODEL), lambda i: (0, i, 0)),
                  _full_spec((1, D_MODEL)), _full_spec((D_MODEL, D_MODEL))],
        out_specs=pl.BlockSpec((SSM_GROUPS, SSM_CB, bsz, CHUNK_W), lambda i: (0, i, 0, 0)),
        out_shape=jax.ShapeDtypeStruct((SSM_GROUPS, nc, bsz, CHUNK_W), BF16),
        scratch_shapes=[pltpu.VMEM((SSM_SLOTS, D_MODEL // LANES, bsz * SSM_PITCH, LANES), F32)],
        compiler_params=_params(1),
        name="ssm_in",
    )(x, norm_w.astype(F32)[None, :], w_u.astype(BF16))


def _ssm_out_kernel(y2_ref, zs_ref, rs_ref, wglu_ref, wso_ref, o_ref, rows_ref):
    bsz, tl = zs_ref.shape[0], zs_ref.shape[1]
    for c in range(tl // CHUNK):
        for s8 in range(D_MODEL // LANES):
            for half in range(CHUNK // LANE_BLKS):
                xs = [pltpu.bitcast(y2_ref[s8 * LANE_BLKS + g, c, :, half * LANES:(half + 1) * LANES], jnp.uint32)
                      for g in range(LANE_BLKS)]
                ys = _block_transpose(xs)
                for k in range(LANE_BLKS):
                    rows_ref[c % SSM_SLOTS, s8, pl.ds(half * LANE_BLKS + k, bsz, stride=SSM_PITCH), :] = (
                        pltpu.bitcast(ys[k], BF16).astype(F32))
        y = jnp.concatenate(
            [jnp.concatenate([rows_ref[c % SSM_SLOTS, s8, b * SSM_PITCH:b * SSM_PITCH + CHUNK, :]
                              for s8 in range(D_MODEL // LANES)], axis=1)
             for b in range(bsz)], axis=0)
        tok = slice(c * CHUNK, (c + 1) * CHUNK)
        zs = zs_ref[:, tok, :].reshape(bsz * CHUNK, D_MODEL).astype(F32)
        g = y / (1.0 + jnp.exp2(y * (GELU_K1 + GELU_K3 * (y * y))))
        gate = _sigmoid(jnp.dot(g.astype(BF16), wglu_ref[...], preferred_element_type=F32))
        ys_in = (g * gate * zs).astype(BF16)
        ys = jnp.dot(ys_in, wso_ref[...], preferred_element_type=F32)
        gated = ys * rs_ref[:, tok, :].reshape(bsz * CHUNK, D_MODEL).astype(F32)
        o_ref[:, tok, :] = gated.reshape(bsz, CHUNK, D_MODEL).astype(BF16)


def _ssm_out_call(y2, nat, w_glu, w_ssm_out):
    bsz, length, _ = nat.shape
    tl = SSM_CB * CHUNK
    tok = lambda blk: pl.BlockSpec((bsz, tl, D_MODEL), lambda i: (0, i, blk))
    return pl.pallas_call(
        _ssm_out_kernel,
        grid=(length // tl,),
        in_specs=[pl.BlockSpec((SSM_GROUPS, SSM_CB, bsz, CHUNK_W), lambda i: (0, i, 0, 0)),
                  tok(NAT_ZS // 2), tok(NAT_RS // 2),
                  _full_spec((D_MODEL, D_MODEL)), _full_spec((D_MODEL, D_MODEL))],
        out_specs=tok(0),
        out_shape=jax.ShapeDtypeStruct((bsz, length, D_MODEL), BF16),
        scratch_shapes=[pltpu.VMEM((SSM_SLOTS, D_MODEL // LANES, bsz * SSM_PITCH, LANES), F32)],
        compiler_params=_params(1),
        name="ssm_out",
    )(y2, nat, nat, w_glu.astype(BF16), w_ssm_out.astype(BF16))


def _ssm_kernel(u_ref, st_ref, mt_ref, rtt_ref, coef_ref, y_ref, hloc_ref, hprev_ref):
    ng, nc, bsz = u_ref.shape[0], u_ref.shape[1], u_ref.shape[2]
    ns2 = 2 * SSM_STATE
    us, coefs = [], []
    for g in range(ng):
        u = u_ref[g].reshape(nc * bsz, CHUNK_W)
        us.append(u)
        half = nc * bsz // 2
        hloc_ref[g, :half, :] = jnp.dot(u[:half], st_ref[g], preferred_element_type=F32)
        hloc_ref[g, half:, :] = jnp.dot(u[half:], st_ref[g], preferred_element_type=F32)
        coefs.append((jnp.broadcast_to(coef_ref[g, 0:1, :], (bsz, ns2)),
                      jnp.broadcast_to(coef_ref[g, 1:2, :], (bsz, ns2))))

    def step(c, carry):
        r = pl.multiple_of(c * bsz, bsz)
        out = []
        for g in range(ng):
            hx, hy = carry[2 * g], carry[2 * g + 1]
            a, b = coefs[g]
            hprev_ref[g, pl.ds(r, bsz), :] = hx.astype(BF16)
            px = hloc_ref[g, pl.ds(r, bsz), 0:ns2]
            py = hloc_ref[g, pl.ds(r, bsz), ns2:2 * ns2]
            out += [a * hx + b * hy + px, a * hy - b * hx + py]
        return tuple(out)

    zero = jnp.zeros((bsz, ns2), F32)
    lax.fori_loop(0, nc, step, (zero,) * (2 * ng), unroll=8)
    for g in range(ng):
        y = jnp.dot(us[g], mt_ref[g], preferred_element_type=F32)
        y = y + lax.dot_general(hprev_ref[g], rtt_ref[g], (((1,), (1,)), ((), ())), preferred_element_type=F32)
        y_ref[g] = y.reshape(nc, bsz, CHUNK_W).astype(BF16)


def _ssm_call(u2, st, mt, rt, coef):
    g, nc, bsz, _ = u2.shape
    ng = SCAN_GROUPS
    gspec = lambda shape: pl.BlockSpec((ng,) + shape, lambda i: (i,) + (0,) * len(shape))
    return pl.pallas_call(
        _ssm_kernel,
        grid=(g // ng,),
        in_specs=[gspec((nc, bsz, CHUNK_W)), gspec((CHUNK_W, 4 * SSM_STATE)), gspec((CHUNK_W, CHUNK_W)),
                  gspec((CHUNK_W, 2 * SSM_STATE)), gspec((8, 2 * SSM_STATE))],
        out_specs=gspec((nc, bsz, CHUNK_W)),
        out_shape=jax.ShapeDtypeStruct(u2.shape, BF16),
        scratch_shapes=[pltpu.VMEM((ng, nc * bsz, 4 * SSM_STATE), F32),
                        pltpu.VMEM((ng, nc * bsz, 2 * SSM_STATE), BF16)],
        compiler_params=_params(1),
        name="ssm_scan",
    )(u2, st, mt, rt, coef)


def _attn_kernel(q_ref, kc_ref, kp_ref, vc_ref, vp_ref, o_ref, lse_ref):
    nres, tq = q_ref.shape[0], q_ref.shape[1]
    qb = 128
    slab = pl.program_id(2)
    qi = lax.broadcasted_iota(jnp.int32, (qb, 2 * qb), 0)
    kk = lax.broadcasted_iota(jnp.int32, (qb, 2 * qb), 1)
    band = jnp.where((kk >= qi) & (kk <= qi + WINDOW_KEYS), 0.0, NEG_INF).astype(F32)
    band0 = band + jnp.where(jnp.logical_and(slab == 0, kk < qb), NEG_INF, 0.0).astype(F32)
    band = jnp.concatenate([band, band], axis=0)
    band0 = jnp.concatenate([band0, band0], axis=0)
    lo_q = lax.broadcasted_iota(jnp.int32, (qb, LANES), 1) < HEAD_DIM
    ones_kv = jnp.ones((2 * qb, LANES), BF16)

    head_slot = lax.broadcasted_iota(jnp.int32, (qb, LANES), 1) // LSE_LANES

    for r, j in [(r, j) for r in range(nres) for j in range(tq // qb)]:
        madd = band0 if j == 0 else band
        rows = slice(j * qb, (j + 1) * qb)
        m_all = jnp.zeros((qb, LANES), F32)
        den_all = jnp.ones((qb, LANES), F32)
        for hp in range(ATTN_W // LANES):
            cs = slice(hp * LANES, (hp + 1) * LANES)
            qp = q_ref[r, rows, cs]
            if j == 0:
                kp = jnp.concatenate([kp_ref[r, :, cs], kc_ref[r, 0:qb, cs]], axis=0)
                vp = jnp.concatenate([vp_ref[r, :, cs], vc_ref[r, 0:qb, cs]], axis=0)
            else:
                kp = kc_ref[r, (j - 1) * qb:(j + 1) * qb, cs]
                vp = vc_ref[r, (j - 1) * qb:(j + 1) * qb, cs]
            zero = jnp.zeros_like(qp)
            q2 = jnp.concatenate([jnp.where(lo_q, qp, zero), jnp.where(lo_q, zero, qp)], axis=0)
            s = lax.dot_general(q2, kp, (((1,), (1,)), ((), ())), preferred_element_type=F32)
            s = s + madd
            m = jnp.max(s, axis=1, keepdims=True)
            p = jnp.exp2(s - m)
            pv = jnp.dot(p.astype(BF16), jnp.concatenate([vp, ones_kv], axis=1), preferred_element_type=F32)
            num = jnp.where(lo_q, pv[:qb, :LANES], pv[qb:, :LANES])
            den = jnp.where(lo_q, pv[:qb, LANES:], pv[qb:, LANES:])
            o_ref[r, rows, cs] = (num / den).astype(BF16)
            for side in range(2):
                slot = head_slot == 2 * hp + side
                half = slice(side * qb, (side + 1) * qb)
                m_all = jnp.where(slot, m[half], m_all)
                den_all = jnp.where(slot, pv[half, LANES:], den_all)
        lse_ref[r, rows, :] = m_all + jnp.log2(den_all)


def _attn_call(qkv, group, blocks, rows_per_step=4096):
    bsz, d, ld, _ = qkv.shape
    tq = min(rows_per_step, ld)
    nres = min(d, rows_per_step // tq)
    nslab = ld // tq
    per = tq // 128
    cur = lambda blk: pl.BlockSpec((None, nres, tq, COL), lambda b, r, i: (b, r, i, blk))
    prev = lambda blk: pl.BlockSpec(
        (None, nres, 128, COL), lambda b, r, i: (b, r, jnp.maximum(i * per - 1, 0), blk))
    ospec = lambda w: pl.BlockSpec((None, nres, tq, w), lambda b, r, i: (b, r, i, 0))
    qb, kb, vb = blocks
    return pl.pallas_call(
        _attn_kernel,
        grid=(bsz, d // nres, nslab),
        in_specs=[cur(qb), cur(kb), prev(kb), cur(vb), prev(vb)],
        out_specs=[ospec(ATTN_W), ospec(LANES)],
        out_shape=[jax.ShapeDtypeStruct((bsz, d, ld, ATTN_W), BF16),
                   jax.ShapeDtypeStruct((bsz, d, ld, LANES), F32)],
        compiler_params=_params(3),
        name=f"attn_d{DILATIONS[group]}",
    )(qkv, qkv, qkv, qkv, qkv)


def _merge_kernel(x_ref, ys_ref, za_ref, ra_ref,
                  a0_ref, a1_ref, a2_ref, l0_ref, l1_ref, l2_ref,
                  spread_ref, wao_ref, wo_ref, o_ref, il_ref, stage_ref):
    tm = x_ref.shape[0]
    d1 = DILATIONS[1]

    def to_token_order(ref, base, stage_base=None):
        d, nslab = ref.shape[0], ref.shape[2] // LANES
        for r in range(d):
            blk = ref[r].astype(F32)
            for c in range(nslab):
                piece = blk[:, c * LANES:(c + 1) * LANES]
                if d == d1:
                    il_ref[base + c, pl.ds(r, tm // d, stride=d), :] = piece
                else:
                    r1, r2 = r % d1, r // d1
                    stage_ref[stage_base + c, pl.ds(r1 * (tm // d1) + r2, tm // d, stride=d1), :] = piece
        if d != d1:
            for c in range(nslab):
                for r1 in range(d1):
                    il_ref[base + c, pl.ds(r1, tm // d1, stride=d1), :] = (
                        stage_ref[stage_base + c, r1 * (tm // d1):(r1 + 1) * (tm // d1), :])
        return lambda rows: jnp.concatenate([il_ref[base + c, rows, :] for c in range(nslab)], axis=1)

    def per_head_to_lanes(w):
        hi = w.astype(BF16)
        lo = (w - hi.astype(F32)).astype(BF16)
        return jnp.dot(jnp.concatenate([hi, lo], axis=1), spread_ref[...], preferred_element_type=F32)

    wide_slabs = ATTN_W // LANES
    l1_at, l2_at = to_token_order(l1_ref, 0), to_token_order(l2_ref, 1, 0)
    a1_at, a2_at = to_token_order(a1_ref, 2), to_token_order(a2_ref, 2 + wide_slabs, 1)
    rows = slice(None)
    l0, l1, l2 = l0_ref[...], l1_at(rows), l2_at(rows)
    lm = jnp.maximum(jnp.maximum(l0, l1), l2)
    e0, e1, e2 = jnp.exp2(l0 - lm), jnp.exp2(l1 - lm), jnp.exp2(l2 - lm)
    inv = 1.0 / (e0 + e1 + e2)
    att = (per_head_to_lanes(e0 * inv) * a0_ref[...].astype(F32)
           + per_head_to_lanes(e1 * inv) * a1_at(rows)
           + per_head_to_lanes(e2 * inv) * a2_at(rows))
    ya_in = (att * za_ref[...].astype(F32)).astype(BF16)
    ya = jnp.dot(ya_in, wao_ref[...], preferred_element_type=F32)
    m = ys_ref[...].astype(F32) + ra_ref[...].astype(F32) * ya
    o_ref[...] = x_ref[...] + jnp.dot(m.astype(BF16), wo_ref[...], preferred_element_type=F32)


def _merge_call(x, ys, nat, attn, w_attn_out, w_o, tm=1024):
    bsz, length, _ = x.shape
    wide = lambda blk: pl.BlockSpec((None, tm, D_MODEL), lambda b, i: (b, i, blk))
    half = lambda blk: pl.BlockSpec((None, tm, ATTN_W), lambda b, i: (b, i, blk))
    res = lambda d, w: pl.BlockSpec((None, d, tm // d, w), lambda b, i: (b, 0, i, 0))
    (a0, l0), (a1, l1), (a2, l2) = attn
    d4, d16 = DILATIONS[1], DILATIONS[2]
    a0, l0 = a0.reshape(bsz, length, ATTN_W), l0.reshape(bsz, length, LANES)
    spread = (jnp.arange(LANES)[:, None] == (jnp.arange(ATTN_W)[None, :] // HEAD_DIM) * LSE_LANES).astype(BF16)
    spread = jnp.concatenate([spread, spread], axis=0)
    return pl.pallas_call(
        _merge_kernel,
        grid=(bsz, length // tm),
        in_specs=[wide(0), wide(0), half(NAT_ZA), wide(NAT_RA // 2),
                  half(0), res(d4, ATTN_W), res(d16, ATTN_W),
                  pl.BlockSpec((None, tm, LANES), lambda b, i: (b, i, 0)), res(d4, LANES), res(d16, LANES),
                  _full_spec((2 * LANES, ATTN_W)), _full_spec((ATTN_W, D_MODEL)), _full_spec((D_MODEL, D_MODEL))],
        out_specs=wide(0),
        out_shape=jax.ShapeDtypeStruct((bsz, length, D_MODEL), F32),
        scratch_shapes=[pltpu.VMEM((2 + 2 * (ATTN_W // LANES), tm, LANES), F32),
                        pltpu.VMEM((1 + ATTN_W // LANES, tm, LANES), F32)],
        compiler_params=_params(2),
        name="merge",
    )(x, ys, nat, nat, a0, a1, a2, l0, l1, l2, spread, w_attn_out.astype(BF16), w_o.astype(BF16))


def kernel(x, positions, norm_w, w_in, lam_re, lam_im, log_dt, b_re, b_im, c_re, c_im, d_skip, w_glu,
           q_norm_w, k_norm_w, w_ssm_out, w_attn_out, w_o):
    bsz, length, width = x.shape
    assert width == D_MODEL and w_in.shape[1:] == (D_MODEL, IN_WIDTH)
    assert bsz % 16 == 0, "the scan layout packs the batch into 16-row bf16 tiles"
    assert length % (128 * DILATIONS[-1]) == 0, "every residue class must hold whole 128-query blocks"
    xf = x.astype(F32)
    for layer in range(norm_w.shape[0]):
        st, mt, rt, coef = _ssm_prep_call(lam_re[layer], lam_im[layer], log_dt[layer], b_re[layer],
                                          b_im[layer], c_re[layer], c_im[layer], d_skip[layer])
        nat, qkv4, qkv16 = _proj_call(xf, positions, norm_w[layer], w_in[layer], q_norm_w[layer],
                                      k_norm_w[layer])
        u2 = _ssm_in_call(xf, norm_w[layer], w_in[layer][:, :D_MODEL])
        y2 = _ssm_call(u2, st, mt, rt, coef)
        ys = _ssm_out_call(y2, nat, w_glu[layer], w_ssm_out[layer])
        attn = [_attn_call(nat[:, None], 0, (NAT_Q, NAT_K, NAT_V)),
                _attn_call(qkv4, 1, (0, 1, 2)),
                _attn_call(qkv16, 2, (0, 1, 2))]
        xf = _merge_call(xf, ys, nat, attn, w_attn_out[layer], w_o[layer])
    return xf.astype(x.dtype)
```
